```python
import math
import jax, jax.numpy as jnp
from jax import lax
import numpy as np

D_MODEL = 1024
BATCH = 32
SEQ = 256
DEPTH = 2
DEC_BATCH = 8
DEC_SEQ = 1024
PAST_LEN = 512

GRID_W = 64
D_FF = 2816
S5_WIDTH = 512
S5_GROUP = 16
S5_GROUPS = S5_WIDTH // S5_GROUP
S5_STATE = 64
CONV_WIDTH = 256
CONV_K = 31
SG_WIDTH = 256
SG_CHUNK = 128
SG_HEADS = 4
SG_HEAD_DIM = SG_WIDTH // SG_HEADS
N_BRANCH = 3
IN_COLS = S5_WIDTH + 2 * CONV_WIDTH + 2 * SG_WIDTH
N_MOD = 9
EPS = 1e-6

kernel_name = 'hybrid_s5_conv_sgmlp_diffusion_step'


def _rmsnorm(x, g):
    xf = x.astype(jnp.float32)
    y = xf * lax.rsqrt(jnp.mean(xf * xf, axis=-1, keepdims=True) + EPS)
    return (y * g.astype(jnp.float32)).astype(x.dtype)


def _layernorm(x, g, b):
    xf = x.astype(jnp.float32)
    mu = jnp.mean(xf, axis=-1, keepdims=True)
    var = jnp.mean(jnp.square(xf - mu), axis=-1, keepdims=True)
    y = (xf - mu) * lax.rsqrt(var + EPS)
    return (y * g.astype(jnp.float32) + b.astype(jnp.float32)).astype(x.dtype)


def _swiglu(h, w1, w2):
    g, u = jnp.split(h @ w1, 2, axis=-1)
    return (jax.nn.silu(g) * u) @ w2


def _cmul(ar, ai, br, bi):
    return ar * br - ai * bi, ar * bi + ai * br


def _ssm_combine(e1, e2):
    a1r, a1i, b1r, b1i = e1
    a2r, a2i, b2r, b2i = e2
    ar, ai = _cmul(a2r, a2i, a1r, a1i)
    br, bi = _cmul(a2r, a2i, b1r, b1i)
    return ar, ai, br + b2r, bi + b2i


def _s5_direction(u, a_re, a_im, log_dt, b_re, b_im, c_re, c_im, h0, reverse):
    f32 = jnp.float32
    a_re = a_re.astype(f32); a_im = a_im.astype(f32)
    b_re = b_re.astype(f32); b_im = b_im.astype(f32)
    c_re = c_re.astype(f32); c_im = c_im.astype(f32)
    dt = jnp.exp(log_dt.astype(f32))[:, None]
    mag = jnp.exp(a_re * dt)
    abar_r, abar_i = mag * jnp.cos(a_im * dt), mag * jnp.sin(a_im * dt)
    den = a_re * a_re + a_im * a_im
    nr, ni = abar_r - 1.0, abar_i
    qr = (nr * a_re + ni * a_im) / den
    qi = (ni * a_re - nr * a_im) / den
    bbar_r = qr[..., None] * b_re - qi[..., None] * b_im
    bbar_i = qr[..., None] * b_im + qi[..., None] * b_re
    bu_r = jnp.einsum('gph,blgh->blgp', bbar_r, u)
    bu_i = jnp.einsum('gph,blgh->blgp', bbar_i, u)
    first = -1 if reverse else 0
    ir, ii = _cmul(abar_r, abar_i, h0[..., 0].astype(f32), h0[..., 1].astype(f32))
    bu_r = bu_r.at[:, first].add(ir)
    bu_i = bu_i.at[:, first].add(ii)
    ar = jnp.broadcast_to(abar_r, bu_r.shape)
    ai = jnp.broadcast_to(abar_i, bu_i.shape)
    _, _, sr, si = lax.associative_scan(_ssm_combine, (ar, ai, bu_r, bu_i), reverse=reverse, axis=1)
    y = jnp.einsum('ghp,blgp->blgh', c_re, sr) - jnp.einsum('ghp,blgp->blgh', c_im, si)
    last = 0 if reverse else -1
    h_final = jnp.stack([sr[:, last], si[:, last]], axis=-1)
    return y, h_final


def _s5_branch(z, P, l, h0):
    bn, L, _ = z.shape
    zf = z.astype(jnp.float32)
    u = zf.reshape(bn, L, S5_GROUPS, S5_GROUP)
    ys, hs = [], []
    for d, rev in ((0, False), (1, True)):
        y, hT = _s5_direction(u, P['s5_a_re'][l, d], P['s5_a_im'][l, d], P['s5_log_dt'][l, d],
                              P['s5_b_re'][l, d], P['s5_b_im'][l, d],
                              P['s5_c_re'][l, d], P['s5_c_im'][l, d], h0[:, d], rev)
        ys.append(y)
        hs.append(hT)
    y = (ys[0] + ys[1]).reshape(bn, L, S5_WIDTH) + P['s5_d'][l].astype(jnp.float32) * zf
    y = jax.nn.gelu(y)
    y = y * jax.nn.sigmoid(y @ P['s5_w_glu'][l].astype(jnp.float32))
    return y.astype(z.dtype), jnp.stack(hs, axis=1)


def _conv_branch(z, P, l):
    a, b = jnp.split(z, 2, axis=-1)
    g = a * jax.nn.sigmoid(b)
    w = P['conv_w'][l].astype(g.dtype)[:, None, :]
    y = lax.conv_general_dilated(g, w, window_strides=(1,), padding=[(CONV_K // 2, CONV_K // 2)],
                                 dimension_numbers=('NWC', 'WIO', 'NWC'),
                                 feature_group_count=CONV_WIDTH)
    y = y + P['conv_b'][l]
    y = _layernorm(y, P['conv_ln_g'][l], P['conv_ln_b'][l])
    return jax.nn.silu(y)


def _sgmlp_branch(z, P, l):
    bn, L, _ = z.shape
    z = jax.nn.gelu(z)
    u, v = jnp.split(z, 2, axis=-1)
    v = _layernorm(v, P['sg_ln_g'][l], P['sg_ln_b'][l])
    v = v.reshape(bn, L // SG_CHUNK, SG_CHUNK, SG_HEADS, SG_HEAD_DIM)
    s = jnp.einsum('hqk,bnkhc->bnqhc', P['sg_w'][l], v) + P['sg_b'][l].T[None, None, :, :, None]
    return u * s.reshape(bn, L, SG_WIDTH)


def _mixer(h, P, l, h0):
    z = h @ P['w_in'][l]
    z_a, z_b, z_c = jnp.split(z, [S5_WIDTH, S5_WIDTH + 2 * CONV_WIDTH], axis=-1)
    y_a, h_final = _s5_branch(z_a, P, l, h0)
    y_b = _conv_branch(z_b, P, l)
    y_c = _sgmlp_branch(z_c, P, l)
    gates = jax.nn.sigmoid(h @ P['w_gate'][l] + P['b_gate'][l])
    g_a, g_b, g_c = jnp.split(gates, N_BRANCH, axis=-1)
    merged = (g_a * (y_a @ P['w_br_a'][l]) + g_b * (y_b @ P['w_br_b'][l])
              + g_c * (y_c @ P['w_br_c'][l]))
    return merged @ P['w_out'][l], h_final


def _layer(x, cond, P, l, h0):
    mod = (jax.nn.silu(cond) @ P['w_mod'][l] + P['b_mod'][l]).reshape(cond.shape[0], N_MOD, 1, D_MODEL)
    sh1, sc1, gt1, sh2, sc2, gt2, sh3, sc3, gt3 = [mod[:, i] for i in range(N_MOD)]
    h = _rmsnorm(x, P['norm_g'][l, 0]) * (1.0 + sc1) + sh1
    x = x + 0.5 * gt1 * _swiglu(h, P['ffn_w1'][l, 0], P['ffn_w2'][l, 0])
    h = _rmsnorm(x, P['norm_g'][l, 1]) * (1.0 + sc2) + sh2
    y, h_final = _mixer(h, P, l, h0)
    x = x + gt2 * y
    h = _rmsnorm(x, P['norm_g'][l, 2]) * (1.0 + sc3) + sh3
    x = x + 0.5 * gt3 * _swiglu(h, P['ffn_w1'][l, 1], P['ffn_w2'][l, 1])
    return x, h_final


def _grid_pos_embed(n_tokens, dim):
    rows = n_tokens // GRID_W
    rr, cc = jnp.meshgrid(jnp.arange(rows, dtype=jnp.float32), jnp.arange(GRID_W, dtype=jnp.float32), indexing='ij')
    quarter = dim // 4
    omega = 1.0 / (10000.0 ** (jnp.arange(quarter, dtype=jnp.float32) / quarter))
    def emb(p):
        ang = p.reshape(-1)[:, None] * omega[None, :]
        return jnp.concatenate([jnp.sin(ang), jnp.cos(ang)], axis=-1)
    return jnp.concatenate([emb(rr), emb(cc)], axis=-1)


def setup_inputs(seed: int = 0) -> dict:
    key = jax.random.key(seed)
    ks = jax.random.split(key, 40)
    f32 = jnp.float32
    def nrm(k, shape, s):
        return jax.random.normal(k, shape, f32) * s
    n_idx = jnp.arange(S5_STATE, dtype=f32)
    s5a = (DEPTH, 2, S5_GROUPS, S5_STATE)
    s5b = (DEPTH, 2, S5_GROUPS, S5_STATE, S5_GROUP)
    s5c = (DEPTH, 2, S5_GROUPS, S5_GROUP, S5_STATE)
    return {
        'x_prompt': nrm(ks[0], (BATCH, SEQ, D_MODEL), 1.0),
        'x_sample': nrm(ks[1], (DEC_BATCH, DEC_SEQ, D_MODEL), 1.0),
        'state_ssm': nrm(ks[2], (DEC_BATCH, DEPTH, 2, S5_GROUPS, S5_STATE, 2), 0.1),
        'c': nrm(ks[3], (DEC_BATCH, D_MODEL), 1.0),
        'c_ctx': nrm(ks[4], (D_MODEL,), 1.0),
        'w_mod': nrm(ks[5], (DEPTH, D_MODEL, N_MOD * D_MODEL), 0.5 * D_MODEL ** -0.5),
        'b_mod': nrm(ks[6], (DEPTH, N_MOD * D_MODEL), 0.02),
        'norm_g': 1.0 + nrm(ks[7], (DEPTH, 3, D_MODEL), 0.02),
        'ffn_w1': nrm(ks[8], (DEPTH, 2, D_MODEL, 2 * D_FF), D_MODEL ** -0.5),
        'ffn_w2': nrm(ks[9], (DEPTH, 2, D_FF, D_MODEL), D_FF ** -0.5),
        'w_in': nrm(ks[10], (DEPTH, D_MODEL, IN_COLS), D_MODEL ** -0.5),
        'w_gate': nrm(ks[11], (DEPTH, D_MODEL, N_BRANCH * D_MODEL), D_MODEL ** -0.5),
        'b_gate': nrm(ks[12], (DEPTH, N_BRANCH * D_MODEL), 0.02),
        's5_a_re': -0.5 + nrm(ks[13], s5a, 0.01),
        's5_a_im': math.pi * n_idx + nrm(ks[14], s5a, 0.01),
        's5_log_dt': jax.random.uniform(ks[15], (DEPTH, 2, S5_GROUPS), f32, math.log(1e-3), math.log(1e-1)),
        's5_b_re': nrm(ks[16], s5b, (2 * S5_GROUP) ** -0.5),
        's5_b_im': nrm(ks[17], s5b, (2 * S5_GROUP) ** -0.5),
        's5_c_re': nrm(ks[18], s5c, S5_STATE ** -0.5),
        's5_c_im': nrm(ks[19], s5c, S5_STATE ** -0.5),
        's5_d': nrm(ks[20], (DEPTH, S5_WIDTH), 1.0),
        's5_w_glu': nrm(ks[21], (DEPTH, S5_WIDTH, S5_WIDTH), S5_WIDTH ** -0.5),
        'w_br_a': nrm(ks[22], (DEPTH, S5_WIDTH, D_MODEL), S5_WIDTH ** -0.5),
        'conv_w': nrm(ks[23], (DEPTH, CONV_K, CONV_WIDTH), CONV_K ** -0.5),
        'conv_b': nrm(ks[24], (DEPTH, CONV_WIDTH), 0.02),
        'conv_ln_g': 1.0 + nrm(ks[25], (DEPTH, CONV_WIDTH), 0.02),
        'conv_ln_b': nrm(ks[26], (DEPTH, CONV_WIDTH), 0.02),
        'w_br_b': nrm(ks[27], (DEPTH, CONV_WIDTH, D_MODEL), CONV_WIDTH ** -0.5),
        'sg_ln_g': 1.0 + nrm(ks[28], (DEPTH, SG_WIDTH), 0.02),
        'sg_ln_b': nrm(ks[29], (DEPTH, SG_WIDTH), 0.02),
        'sg_w': nrm(ks[30], (DEPTH, SG_HEADS, SG_CHUNK, SG_CHUNK), SG_CHUNK ** -0.5),
        'sg_b': 1.0 + nrm(ks[31], (DEPTH, SG_HEADS, SG_CHUNK), 0.02),
        'w_br_c': nrm(ks[32], (DEPTH, SG_WIDTH, D_MODEL), SG_WIDTH ** -0.5),
        'w_out': nrm(ks[33], (DEPTH, D_MODEL, D_MODEL), D_MODEL ** -0.5),
        'final_g': 1.0 + nrm(ks[34], (D_MODEL,), 0.02),
    }


def reference(x_prompt, x_sample, state_ssm, c, c_ctx, w_mod, b_mod, norm_g, ffn_w1, ffn_w2,
              w_in, w_gate, b_gate, s5_a_re, s5_a_im, s5_log_dt, s5_b_re, s5_b_im, s5_c_re, s5_c_im,
              s5_d, s5_w_glu, w_br_a, conv_w, conv_b, conv_ln_g, conv_ln_b, w_br_b,
              sg_ln_g, sg_ln_b, sg_w, sg_b, w_br_c, w_out, final_g):
    P = dict(w_mod=w_mod, b_mod=b_mod, norm_g=norm_g, ffn_w1=ffn_w1, ffn_w2=ffn_w2,
             w_in=w_in, w_gate=w_gate, b_gate=b_gate, s5_a_re=s5_a_re, s5_a_im=s5_a_im,
             s5_log_dt=s5_log_dt, s5_b_re=s5_b_re, s5_b_im=s5_b_im, s5_c_re=s5_c_re, s5_c_im=s5_c_im,
             s5_d=s5_d, s5_w_glu=s5_w_glu, w_br_a=w_br_a, conv_w=conv_w, conv_b=conv_b,
             conv_ln_g=conv_ln_g, conv_ln_b=conv_ln_b, w_br_b=w_br_b, sg_ln_g=sg_ln_g,
             sg_ln_b=sg_ln_b, sg_w=sg_w, sg_b=sg_b, w_br_c=w_br_c, w_out=w_out)
    xc = x_prompt
    h_zero = jnp.zeros((x_prompt.shape[0], 2, S5_GROUPS, S5_STATE, 2), jnp.float32)
    ctx_states = []
    for l in range(DEPTH):
        xc, h_final = _layer(xc, c_ctx[None, :], P, l, h_zero)
        ctx_states.append(h_final)
    y_prompt = _rmsnorm(xc, final_g)
    new_state_ssm = jnp.stack(ctx_states, axis=1)
    xs = x_sample + _grid_pos_embed(x_sample.shape[1], D_MODEL).astype(x_sample.dtype)[None]
    for l in range(DEPTH):
        xs, _ = _layer(xs, c, P, l, state_ssm[:, l])
    y_sample = _rmsnorm(xs, final_g)
    return (y_prompt, y_sample, new_state_ssm)
```

```python
import functools
import math

import jax
import jax.numpy as jnp
from jax import lax
from jax.experimental import pallas as pl
from jax.experimental.pallas import tpu as pltpu

D_MODEL = 1024
DEPTH = 2
GRID_W = 64
D_FF = 2816
S5_WIDTH = 512
S5_GROUP = 16
S5_GROUPS = 32
S5_STATE = 64
CONV_WIDTH = 256
CONV_K = 31
SG_WIDTH = 256
SG_CHUNK = 128
SG_HEADS = 4
SG_HEAD_DIM = SG_WIDTH // SG_HEADS
IN_COLS = S5_WIDTH + 2 * CONV_WIDTH + 2 * SG_WIDTH
N_MOD = 9
EPS = 1e-6

S5_T = 16
S5_CL = S5_T * S5_GROUP
S5_GB = 8
MOD_ROWS = 16
MOD_TN = 1152
TOK_BLOCK = 1024
FFN_TM = 512
CONV_RC = 64
CONV_PAD = 16
SUBLANES = 8
CONV_SHIFT_ROWS = SUBLANES * ((CONV_PAD - CONV_K // 2 + CONV_K - 1) // SUBLANES)
VMEM_LIMIT = 56 * 1024 * 1024

BF = jnp.bfloat16
F32 = jnp.float32


def _dot(a, b):
    return jnp.dot(a, b, preferred_element_type=F32)


def _const_spec(shape):
    n = len(shape)
    return pl.BlockSpec(shape, lambda *_: (0,) * n, pipeline_mode=pl.Buffered(1))


def _params(n_axes=1):
    return pltpu.CompilerParams(dimension_semantics=("arbitrary",) * n_axes,
                                vmem_limit_bytes=VMEM_LIMIT)


def _mod_norm(x, g, sc, sh):
    var = jnp.mean(x * x, axis=-1, keepdims=True)
    return (x * lax.rsqrt(var + EPS) * g) * (1.0 + sc) + sh


def _layernorm(x, g, b):
    mu = jnp.mean(x, axis=-1, keepdims=True)
    xc = x - mu
    var = jnp.mean(xc * xc, axis=-1, keepdims=True)
    return xc * lax.rsqrt(var + EPS) * g + b


def _mod_kernel(c_ref, w_ref, b_ref, o_ref):
    c = c_ref[...]
    a = (c * jax.nn.sigmoid(c)).astype(BF)
    o_ref[...] = _dot(a, w_ref[...].astype(BF)) + b_ref[...]


def _modulation(cond, w_mod, b_mod):
    n = N_MOD * D_MODEL
    return pl.pallas_call(
        _mod_kernel,
        grid=(DEPTH, n // MOD_TN),
        in_specs=[
            pl.BlockSpec((MOD_ROWS, D_MODEL), lambda l, j: (0, 0)),
            pl.BlockSpec((None, D_MODEL, MOD_TN), lambda l, j: (l, 0, j)),
            pl.BlockSpec((None, 1, MOD_TN), lambda l, j: (l, 0, j)),
        ],
        out_specs=pl.BlockSpec((None, MOD_ROWS, MOD_TN), lambda l, j: (l, 0, j)),
        out_shape=jax.ShapeDtypeStruct((DEPTH, MOD_ROWS, n), F32),
        compiler_params=_params(2),
        name="modulation",
    )(cond, w_mod, b_mod.reshape(DEPTH, 1, n))


def _ffn_kernel(*refs, mod_base, add_pos, final):
    refs = list(refs)
    x_ref = refs.pop(0)
    pos_ref = refs.pop(0) if add_pos else None
    mod_ref, g_ref, w1g_ref, w1u_ref, w2_ref = refs[:5]
    refs = refs[5:]
    fg_ref = refs.pop(0) if final else None
    o_ref = refs.pop(0)

    x = x_ref[...]
    if add_pos:
        x = x + pos_ref[...]
    sh = mod_ref[mod_base:mod_base + 1, :]
    sc = mod_ref[mod_base + 1:mod_base + 2, :]
    gt = mod_ref[mod_base + 2:mod_base + 3, :]
    h = _mod_norm(x, g_ref[...], sc, sh).astype(BF)
    g = _dot(h, w1g_ref[...])
    u = _dot(h, w1u_ref[...])
    a = (g * jax.nn.sigmoid(g) * u).astype(BF)
    y = _dot(a, w2_ref[...])
    xn = x + (0.5 * gt) * y
    if final:
        var = jnp.mean(xn * xn, axis=-1, keepdims=True)
        xn = xn * lax.rsqrt(var + EPS) * fg_ref[...]
    o_ref[...] = xn


def _ffn(x, mod, norm_g, w1g, w1u, w2, *, mod_base, seq_tokens, pos=None, final_g=None):
    n = x.shape[0]
    tm = FFN_TM
    per_mod = seq_tokens // tm
    n_mod = mod.shape[0]
    in_specs = [pl.BlockSpec((tm, D_MODEL), lambda i: (i, 0))]
    args = [x]
    if pos is not None:
        pos_blocks = pos.shape[0] // tm
        in_specs.append(pl.BlockSpec((tm, D_MODEL), lambda i: (i % pos_blocks, 0)))
        args.append(pos)
    in_specs += [
        pl.BlockSpec((None, N_MOD, D_MODEL), lambda i: ((i // per_mod) % n_mod, 0, 0)),
        _const_spec((1, D_MODEL)),
        _const_spec((D_MODEL, D_FF)),
        _const_spec((D_MODEL, D_FF)),
        _const_spec((D_FF, D_MODEL)),
    ]
    args += [mod, norm_g.reshape(1, D_MODEL), w1g, w1u, w2]
    if final_g is not None:
        in_specs.append(_const_spec((1, D_MODEL)))
        args.append(final_g.reshape(1, D_MODEL))
    kern = functools.partial(_ffn_kernel, mod_base=mod_base, add_pos=pos is not None,
                             final=final_g is not None)
    return pl.pallas_call(
        kern,
        grid=(n // tm,),
        in_specs=in_specs,
        out_specs=pl.BlockSpec((tm, D_MODEL), lambda i: (i, 0)),
        out_shape=jax.ShapeDtypeStruct((n, D_MODEL), F32),
        compiler_params=_params(1),
        name="ffn",
    )(*args)


def _mix_in_kernel(x_ref, mod_ref, g_ref, win_ref, cw_ref, cb_ref, clg_ref, clb_ref,
                   sglg_ref, sglb_ref, sgw_ref, sgb_ref,
                   za_ref, yb_ref, yc_ref, pad_ref, *, seq_len):
    n_seq = TOK_BLOCK // seq_len
    x = x_ref[...]
    h = _mod_norm(x, g_ref[...], mod_ref[4:5, :], mod_ref[3:4, :]).astype(BF)
    z = _dot(h, win_ref[...])
    za_ref[...] = z[:, :S5_WIDTH]

    o = S5_WIDTH
    gl = z[:, o:o + CONV_WIDTH] * jax.nn.sigmoid(z[:, o + CONV_WIDTH:o + 2 * CONV_WIDTH])
    zeros = jnp.zeros((CONV_PAD, CONV_WIDTH), F32)
    shifted_rows = seq_len + CONV_SHIFT_ROWS
    for s in range(n_seq):
        padded = jnp.concatenate([zeros, gl[s * seq_len:(s + 1) * seq_len, :], zeros], axis=0)
        for r in range(SUBLANES):
            pad_ref[r, s] = padded[r:r + shifted_rows, :]
    first = CONV_PAD - CONV_K // 2
    cb, clg, clb = cb_ref[...], clg_ref[...], clb_ref[...]
    for s in range(n_seq):
        def conv_rows(r, carry, s=s):
            r0 = pl.multiple_of(r * CONV_RC, CONV_RC)
            acc = jnp.zeros((CONV_RC, CONV_WIDTH), F32)
            for k in range(CONV_K):
                o = first + k
                start = pl.multiple_of(r0 + SUBLANES * (o // SUBLANES), SUBLANES)
                acc = acc + cw_ref[k:k + 1, :] * pad_ref[o % SUBLANES, s, pl.ds(start, CONV_RC), :]
            y = _layernorm(acc + cb, clg, clb)
            y = y * jax.nn.sigmoid(y)
            yb_ref[pl.ds(pl.multiple_of(s * seq_len + r0, CONV_RC), CONV_RC), :] = y.astype(BF)
            return carry
        lax.fori_loop(0, seq_len // CONV_RC, conv_rows, 0)

    o = S5_WIDTH + 2 * CONV_WIDTH
    zc = jax.nn.gelu(z[:, o:o + 2 * SG_WIDTH])
    u = zc[:, :SG_WIDTH]
    v = _layernorm(zc[:, SG_WIDTH:], sglg_ref[...], sglb_ref[...]).astype(BF)
    head = lax.broadcasted_iota(jnp.int32, (SG_CHUNK, SG_WIDTH), 1) // SG_HEAD_DIM
    zero = jnp.zeros((SG_CHUNK, SG_WIDTH), BF)
    for n in range(TOK_BLOCK // SG_CHUNK):
        vn = v[n * SG_CHUNK:(n + 1) * SG_CHUNK, :]
        vcat = jnp.concatenate([jnp.where(head == hh, vn, zero) for hh in range(SG_HEADS)], axis=0)
        sgate = _dot(sgw_ref[...], vcat) + sgb_ref[...]
        yc_ref[n * SG_CHUNK:(n + 1) * SG_CHUNK, :] = (u[n * SG_CHUNK:(n + 1) * SG_CHUNK, :] * sgate).astype(BF)


def _mix_in(x, mod, norm_g, w_in, conv_w, conv_b, conv_ln_g, conv_ln_b, sg_ln_g, sg_ln_b, sgw_cat, sgb_full,
            *, seq_len):
    n = x.shape[0]
    n_mod = mod.shape[0]
    per_mod = max(seq_len // TOK_BLOCK, 1)
    n_seq = TOK_BLOCK // seq_len
    row = lambda a: a.reshape(1, -1)
    kern = functools.partial(_mix_in_kernel, seq_len=seq_len)
    return pl.pallas_call(
        kern,
        grid=(n // TOK_BLOCK,),
        in_specs=[
            pl.BlockSpec((TOK_BLOCK, D_MODEL), lambda i: (i, 0)),
            pl.BlockSpec((None, N_MOD, D_MODEL), lambda i: ((i // per_mod) % n_mod, 0, 0)),
            _const_spec((1, D_MODEL)),
            _const_spec((D_MODEL, IN_COLS)),
            _const_spec((CONV_K, CONV_WIDTH)),
            _const_spec((1, CONV_WIDTH)),
            _const_spec((1, CONV_WIDTH)),
            _const_spec((1, CONV_WIDTH)),
            _const_spec((1, SG_WIDTH)),
            _const_spec((1, SG_WIDTH)),
            _const_spec((SG_CHUNK, SG_HEADS * SG_CHUNK)),
            _const_spec((SG_CHUNK, SG_WIDTH)),
        ],
        out_specs=[
            pl.BlockSpec((TOK_BLOCK, S5_WIDTH), lambda i: (i, 0)),
            pl.BlockSpec((TOK_BLOCK, CONV_WIDTH), lambda i: (i, 0)),
            pl.BlockSpec((TOK_BLOCK, SG_WIDTH), lambda i: (i, 0)),
        ],
        out_shape=[
            jax.ShapeDtypeStruct((n, S5_WIDTH), F32),
            jax.ShapeDtypeStruct((n, CONV_WIDTH), BF),
            jax.ShapeDtypeStruct((n, SG_WIDTH), BF),
        ],
        scratch_shapes=[pltpu.VMEM((SUBLANES, n_seq, seq_len + CONV_SHIFT_ROWS, CONV_WIDTH), F32)],
        compiler_params=_params(1),
        name="mix_in",
    )(x, mod, row(norm_g), w_in, conv_w, row(conv_b), row(conv_ln_g), row(conv_ln_b),
      row(sg_ln_g), row(sg_ln_b), sgw_cat, sgb_full)


def _s5_prep_kernel(ac_re_ref, ac_im_ref, dt_ref, bt_re_ref, bt_im_ref, ct_re_ref, ct_im_ref,
                    c_re_ref, c_im_ref, powb_ref, powc_ref, kern_ref, at_ref):
    a_re = ac_re_ref[...]
    a_im = ac_im_ref[...]
    dt = jnp.exp(dt_ref[...])
    mag = jnp.exp(a_re * dt)
    ab_re = mag * jnp.cos(a_im * dt)
    ab_im = mag * jnp.sin(a_im * dt)
    den = a_re * a_re + a_im * a_im
    nr, ni = ab_re - 1.0, ab_im
    q_re = (nr * a_re + ni * a_im) / den
    q_im = (ni * a_re - nr * a_im) / den
    bb_re = q_re * bt_re_ref[...] - q_im * bt_im_ref[...]
    bb_im = q_re * bt_im_ref[...] + q_im * bt_re_ref[...]

    lag = (lax.broadcasted_iota(jnp.int32, (S5_STATE, S5_CL), 1) // S5_GROUP).astype(F32)

    def power(m):
        pm = jnp.exp(a_re * dt * m)
        return pm * jnp.cos(a_im * dt * m), pm * jnp.sin(a_im * dt * m)

    p_re, p_im = power(lag)
    pb_re = p_re * bb_re - p_im * bb_im
    pb_im = p_re * bb_im + p_im * bb_re
    powb_ref[0] = pb_re
    powb_ref[1] = pb_im
    kern_ref[...] = (jnp.dot(c_re_ref[...], pb_re, preferred_element_type=F32, precision=lax.Precision.HIGHEST)
                     - jnp.dot(c_im_ref[...], pb_im, preferred_element_type=F32, precision=lax.Precision.HIGHEST))
    p1_re, p1_im = power(lag + 1.0)
    ct_re, ct_im = ct_re_ref[...], ct_im_ref[...]
    powc_ref[0] = ct_re * p1_re - ct_im * p1_im
    powc_ref[1] = -(ct_re * p1_im + ct_im * p1_re)
    t_re, t_im = power(jnp.full((S5_STATE, S5_CL), float(S5_T), F32))
    at_ref[0] = t_re
    at_ref[1] = t_im


def _s5_operators(a_re, a_im, log_dt, b_re, b_im, c_re, c_im):
    G, P, H, T = S5_GROUPS, S5_STATE, S5_GROUP, S5_T
    dg = 2 * G
    col = lambda a: a.reshape(dg, P, 1)
    tile_b = lambda b: jnp.tile(b.reshape(dg, P, H), (1, 1, T))
    tile_c = lambda c: jnp.tile(c.reshape(dg, H, P).transpose(0, 2, 1), (1, 1, T))
    blk = lambda r, c: pl.BlockSpec((None, r, c), lambda i: (i, 0, 0))
    blk4 = lambda r, c: pl.BlockSpec((None, 2, r, c), lambda i: (i, 0, 0, 0))
    powb, powc, kern, at = pl.pallas_call(
        _s5_prep_kernel,
        grid=(dg,),
        in_specs=[blk(P, 1), blk(P, 1), blk(1, 1), blk(P, S5_CL), blk(P, S5_CL), blk(P, S5_CL), blk(P, S5_CL),
                  blk(H, P), blk(H, P)],
        out_specs=[blk4(P, S5_CL), blk4(P, S5_CL), blk(H, S5_CL), blk4(P, S5_CL)],
        out_shape=[jax.ShapeDtypeStruct((dg, 2, P, S5_CL), F32), jax.ShapeDtypeStruct((dg, 2, P, S5_CL), F32),
                   jax.ShapeDtypeStruct((dg, H, S5_CL), F32), jax.ShapeDtypeStruct((dg, 2, P, S5_CL), F32)],
        compiler_params=_params(1),
        name="s5_prep",
    )(col(a_re), col(a_im), log_dt.reshape(dg, 1, 1), tile_b(b_re), tile_b(b_im), tile_c(c_re), tile_c(c_im),
      c_re.reshape(dg, H, P), c_im.reshape(dg, H, P))

    powb = powb.reshape(2, G, 2, P, T, H)
    powc = powc.reshape(2, G, 2, P, T, H)
    kern = kern.reshape(2, G, H, T, H)
    win = jnp.stack([powb[0][:, :, :, ::-1, :], powb[1]], axis=1)
    win = win.transpose(0, 4, 5, 1, 2, 3).reshape(G, S5_CL, 4 * P)
    wout = jnp.stack([powc[0], powc[1][:, :, :, ::-1, :]], axis=1)
    wout = wout.reshape(G, 4 * P, S5_CL)
    j = jnp.arange(T)[:, None]
    i = jnp.arange(T)[None, :]
    kf = jnp.take(kern[0], jnp.clip(i - j, 0, T - 1), axis=2)
    kb = jnp.take(kern[1], jnp.clip(j - i, 0, T - 1), axis=2)
    mask = lambda c: c[None, None, :, :, None]
    m = jnp.where(mask(i >= j), kf, 0.0) + jnp.where(mask(j >= i), kb, 0.0)
    m = m.transpose(0, 2, 4, 3, 1).reshape(G, S5_CL, S5_CL)
    at = at[:, :, :, 0].reshape(2, G, 2, P)
    ar = jnp.concatenate([at[0, :, 0], at[0, :, 0], at[1, :, 0], at[1, :, 0]], axis=-1)
    ai = jnp.concatenate([-at[0, :, 1], at[0, :, 1], -at[1, :, 1], at[1, :, 1]], axis=-1)
    at_rows = jnp.stack([ar, ai], axis=1)
    return m.astype(BF), win.astype(BF), wout.astype(BF), at_rows


def _s5_kernel(x_ref, m_ref, win_ref, wout_ref, at_ref, h0_ref, y_ref, hf_ref, v_ref, *, n_seq, n_chunks):
    half = 2 * S5_STATE
    for g in range(S5_GB):
        v_ref[g] = _dot(x_ref[g], win_ref[g])
    ar = at_ref[:, 0:1, :]
    ai = at_ref[:, 1:2, :]
    ar_f, ar_b = ar[:, :, :half], ar[:, :, half:]
    ai_f, ai_b = ai[:, :, :half], ai[:, :, half:]

    def step(t, carry):
        sf, sb = carry
        rf = pl.multiple_of(t * n_seq, n_seq)
        rb = pl.multiple_of((n_chunks - 1 - t) * n_seq, n_seq)
        vf = v_ref[:, pl.ds(rf, n_seq), 0:half]
        vb = v_ref[:, pl.ds(rb, n_seq), half:2 * half]
        v_ref[:, pl.ds(rf, n_seq), 0:half] = sf
        v_ref[:, pl.ds(rb, n_seq), half:2 * half] = sb
        sf = ar_f * sf + ai_f * pltpu.roll(sf, S5_STATE, 2) + vf
        sb = ar_b * sb + ai_b * pltpu.roll(sb, S5_STATE, 2) + vb
        return sf, sb

    h0 = h0_ref[...]
    sf, sb = lax.fori_loop(0, n_chunks, step, (h0[:, :, :half], h0[:, :, half:]))
    hf_ref[:, :, 0:half] = sf
    hf_ref[:, :, half:2 * half] = sb
    for g in range(S5_GB):
        y_ref[g] = _dot(x_ref[g], m_ref[g]) + _dot(v_ref[g].astype(BF), wout_ref[g])


def _s5_scan(xc, m, win, wout, at_rows, h0, *, n_seq, n_chunks):
    rows = n_chunks * n_seq
    ns = 4 * S5_STATE
    gspec = lambda r, c: pl.BlockSpec((S5_GB, r, c), lambda i: (i, 0, 0))
    kern = functools.partial(_s5_kernel, n_seq=n_seq, n_chunks=n_chunks)
    return pl.pallas_call(
        kern,
        grid=(S5_GROUPS // S5_GB,),
        in_specs=[gspec(rows, S5_CL), gspec(S5_CL, S5_CL), gspec(S5_CL, ns), gspec(ns, S5_CL),
                  gspec(2, ns), gspec(n_seq, ns)],
        out_specs=[gspec(rows, S5_CL), gspec(n_seq, ns)],
        out_shape=[jax.ShapeDtypeStruct((S5_GROUPS, rows, S5_CL), F32),
                   jax.ShapeDtypeStruct((S5_GROUPS, n_seq, ns), F32)],
        scratch_shapes=[pltpu.VMEM((S5_GB, rows, ns), F32)],
        compiler_params=_params(1),
        name="s5_scan",
    )(xc, m, win, wout, at_rows, h0)


def _to_chunks(za, n_seq, n_chunks):
    z = za.reshape(n_seq, n_chunks, S5_T, S5_GROUPS, S5_GROUP)
    return z.transpose(3, 1, 0, 2, 4).reshape(S5_GROUPS, n_chunks * n_seq, S5_CL).astype(BF)


def _from_chunks(y, n_seq, n_chunks):
    y = y.reshape(S5_GROUPS, n_chunks, n_seq, S5_T, S5_GROUP)
    return y.transpose(2, 1, 3, 0, 4).reshape(n_seq * n_chunks * S5_T, S5_WIDTH)


def _mix_out_kernel(x_ref, mod_ref, g_ref, za_ref, ys_ref, yb_ref, yc_ref, d_ref, wglu_ref,
                    wa_ref, wb_ref, wc_ref, wgate_ref, bgate_ref, wout_ref, o_ref):
    x = x_ref[...]
    h = _mod_norm(x, g_ref[...], mod_ref[4:5, :], mod_ref[3:4, :]).astype(BF)
    ya = jax.nn.gelu(ys_ref[...] + d_ref[...] * za_ref[...])
    ya = ya * jax.nn.sigmoid(_dot(ya.astype(BF), wglu_ref[...]))

    def gate(k):
        lo = k * D_MODEL
        return jax.nn.sigmoid(_dot(h, wgate_ref[:, lo:lo + D_MODEL]) + bgate_ref[:, lo:lo + D_MODEL])

    merged = gate(0) * _dot(ya.astype(BF), wa_ref[...])
    merged = merged + gate(1) * _dot(yb_ref[...], wb_ref[...])
    merged = merged + gate(2) * _dot(yc_ref[...], wc_ref[...])
    y = _dot(merged.astype(BF), wout_ref[...])
    o_ref[...] = x + mod_ref[5:6, :] * y


def _mix_out(x, mod, norm_g, za, ys, yb, yc, s5_d, w_glu, w_a, w_b, w_c, w_gate, b_gate, w_out, *, seq_tokens):
    n = x.shape[0]
    tm = FFN_TM
    per_mod = seq_tokens // tm
    n_mod = mod.shape[0]
    tok = lambda c: pl.BlockSpec((tm, c), lambda i: (i, 0))
    return pl.pallas_call(
        _mix_out_kernel,
        grid=(n // tm,),
        in_specs=[
            tok(D_MODEL),
            pl.BlockSpec((None, N_MOD, D_MODEL), lambda i: ((i // per_mod) % n_mod, 0, 0)),
            _const_spec((1, D_MODEL)),
            tok(S5_WIDTH), tok(S5_WIDTH), tok(CONV_WIDTH), tok(SG_WIDTH),
            _const_spec((1, S5_WIDTH)),
            _const_spec((S5_WIDTH, S5_WIDTH)),
            _const_spec((S5_WIDTH, D_MODEL)),
            _const_spec((CONV_WIDTH, D_MODEL)),
            _const_spec((SG_WIDTH, D_MODEL)),
            _const_spec((D_MODEL, 3 * D_MODEL)),
            _const_spec((1, 3 * D_MODEL)),
            _const_spec((D_MODEL, D_MODEL)),
        ],
        out_specs=tok(D_MODEL),
        out_shape=jax.ShapeDtypeStruct((n, D_MODEL), F32),
        compiler_params=_params(1),
        name="mix_out",
    )(x, mod, norm_g.reshape(1, D_MODEL), za, ys, yb, yc, s5_d.reshape(1, S5_WIDTH), w_glu, w_a, w_b, w_c,
      w_gate, b_gate.reshape(1, 3 * D_MODEL), w_out)


def _grid_pos_embed(n_tokens, dim):
    rows = n_tokens // GRID_W
    rr, cc = jnp.meshgrid(jnp.arange(rows, dtype=F32), jnp.arange(GRID_W, dtype=F32), indexing='ij')
    quarter = dim // 4
    omega = 1.0 / (10000.0 ** (jnp.arange(quarter, dtype=F32) / quarter))

    def emb(p):
        ang = p.reshape(-1)[:, None] * omega[None, :]
        return jnp.concatenate([jnp.sin(ang), jnp.cos(ang)], axis=-1)

    return jnp.concatenate([emb(rr), emb(cc)], axis=-1)


def kernel(x_prompt, x_sample, state_ssm, c, c_ctx, w_mod, b_mod, norm_g, ffn_w1, ffn_w2, w_in, w_gate, b_gate,
           s5_a_re, s5_a_im, s5_log_dt, s5_b_re, s5_b_im, s5_c_re, s5_c_im, s5_d, s5_w_glu, w_br_a, conv_w,
           conv_b, conv_ln_g, conv_ln_b, w_br_b, sg_ln_g, sg_ln_b, sg_w, sg_b, w_br_c, w_out, final_g):
    batch, seq, _ = x_prompt.shape
    dec_batch, dec_seq, _ = x_sample.shape
    assert (batch * seq) % TOK_BLOCK == 0 and TOK_BLOCK % seq == 0 and dec_seq % TOK_BLOCK == 0
    assert seq % SG_CHUNK == 0 and seq % S5_T == 0 and dec_seq % S5_T == 0
    assert 1 + dec_batch <= MOD_ROWS

    cond = jnp.zeros((MOD_ROWS, D_MODEL), F32).at[0].set(c_ctx).at[1:1 + dec_batch].set(c)
    mod_all = _modulation(cond, w_mod, b_mod).reshape(DEPTH, MOD_ROWS, N_MOD, D_MODEL)
    pos = _grid_pos_embed(dec_seq, D_MODEL)

    groups = {
        "ctx": dict(x=x_prompt.reshape(batch * seq, D_MODEL), n_seq=batch, seq_len=seq, mod_rows=slice(0, 1),
                    mod_tokens=batch * seq, pos=None),
        "smp": dict(x=x_sample.reshape(dec_batch * dec_seq, D_MODEL), n_seq=dec_batch, seq_len=dec_seq,
                    mod_rows=slice(1, 1 + dec_batch), mod_tokens=dec_seq, pos=pos),
    }
    ctx_states = []
    for l in range(DEPTH):
        w1 = ffn_w1[l].astype(BF)
        w2 = ffn_w2[l].astype(BF)
        w_in_l = w_in[l].astype(BF)
        s5_ops = _s5_operators(s5_a_re[l], s5_a_im[l], s5_log_dt[l], s5_b_re[l], s5_b_im[l], s5_c_re[l], s5_c_im[l])
        sgw_cat = sg_w[l].transpose(1, 0, 2).reshape(SG_CHUNK, SG_HEADS * SG_CHUNK).astype(BF)
        sgb_full = jnp.repeat(sg_b[l].T, SG_HEAD_DIM, axis=1)
        for name, gr in groups.items():
            mod = mod_all[l, gr["mod_rows"]]
            n_seq, seq_len = gr["n_seq"], gr["seq_len"]
            n_chunks = seq_len // S5_T
            x = _ffn(gr["x"], mod, norm_g[l, 0], w1[0, :, :D_FF], w1[0, :, D_FF:], w2[0], mod_base=0,
                     seq_tokens=gr["mod_tokens"], pos=gr["pos"] if l == 0 else None)
            za, yb, yc = _mix_in(x, mod, norm_g[l, 1], w_in_l, conv_w[l], conv_b[l], conv_ln_g[l], conv_ln_b[l],
                                 sg_ln_g[l], sg_ln_b[l], sgw_cat, sgb_full, seq_len=seq_len)
            if name == "ctx":
                h0 = jnp.zeros((S5_GROUPS, n_seq, 4 * S5_STATE), F32)
            else:
                h0 = state_ssm[:, l].transpose(2, 0, 1, 4, 3).reshape(S5_GROUPS, n_seq, 4 * S5_STATE)
            ys, hf = _s5_scan(_to_chunks(za, n_seq, n_chunks), *s5_ops, h0, n_seq=n_seq, n_chunks=n_chunks)
            if name == "ctx":
                hf = hf.reshape(S5_GROUPS, n_seq, 2, 2, S5_STATE).transpose(1, 2, 0, 4, 3)
                ctx_states.append(hf)
            x = _mix_out(x, mod, norm_g[l, 1], za, _from_chunks(ys, n_seq, n_chunks), yb, yc, s5_d[l],
                         s5_w_glu[l].astype(BF), w_br_a[l].astype(BF), w_br_b[l].astype(BF), w_br_c[l].astype(BF),
                         w_gate[l].astype(BF), b_gate[l], w_out[l].astype(BF), seq_tokens=gr["mod_tokens"])
            x = _ffn(x, mod, norm_g[l, 2], w1[1, :, :D_FF], w1[1, :, D_FF:], w2[1], mod_base=6,
                     seq_tokens=gr["mod_tokens"], final_g=final_g if l == DEPTH - 1 else None)
            gr["x"] = x
    y_prompt = groups["ctx"]["x"].reshape(batch, seq, D_MODEL)
    y_sample = groups["smp"]["x"].reshape(dec_batch, dec_seq, D_MODEL)
    new_state_ssm = jnp.stack(ctx_states, axis=1)
    return (y_prompt, y_sample, new_state_ssm)
```

```python
import functools

import jax
import jax.numpy as jnp
from jax import lax
from jax.experimental import pallas as pl
from jax.experimental.pallas import tpu as pltpu

D_MODEL = 1024
DEPTH = 2
GRID_W = 64
D_FF = 2816
S5_WIDTH = 512
S5_GROUP = 16
S5_GROUPS = 32
S5_STATE = 64
CONV_WIDTH = 256
CONV_K = 31
SG_WIDTH = 256
SG_CHUNK = 128
SG_HEADS = 4
SG_HEAD_DIM = SG_WIDTH // SG_HEADS
BRANCH_COLS = 2 * CONV_WIDTH
IN_COLS = S5_WIDTH + 2 * BRANCH_COLS
N_MOD = 9
EPS = 1e-6

LANES = 128
SUBLANES = 8
S5_T = 16
S5_CL = S5_T * S5_GROUP
S5_NS = 4 * S5_STATE
S5_GB = 8
S5_CB = 128
MOD_ROWS = 16
MOD_TN = 1152
TOK_BLOCK = 1024
FFN_TM = 512
CONV_RC = 64
CONV_PAD = 16
CONV_SHIFT_ROWS = SUBLANES * ((CONV_PAD - CONV_K // 2 + CONV_K - 1) // SUBLANES)
VMEM_LIMIT = 56 * 1024 * 1024

BF = jnp.bfloat16
F32 = jnp.float32
HIGHEST = lax.Precision.HIGHEST


def _dot(a, b):
    return jnp.dot(a, b, preferred_element_type=F32)


def _dot_exact(a, b):
    return jnp.dot(a, b, preferred_element_type=F32, precision=HIGHEST)


def _const_spec(block, index):
    return pl.BlockSpec(block, lambda *_: index, pipeline_mode=pl.Buffered(1))


def _params(n_axes=1):
    return pltpu.CompilerParams(dimension_semantics=("arbitrary",) * n_axes,
                                vmem_limit_bytes=VMEM_LIMIT)


def _mod_norm(x, g, sc, sh):
    var = jnp.mean(x * x, axis=-1, keepdims=True)
    return (x * lax.rsqrt(var + EPS) * g) * (1.0 + sc) + sh


def _layernorm(x, g, b):
    mu = jnp.mean(x, axis=-1, keepdims=True)
    xc = x - mu
    var = jnp.mean(xc * xc, axis=-1, keepdims=True)
    return xc * lax.rsqrt(var + EPS) * g + b


def _mod_spec(n_mod, blocks_per_mod, rows=1):
    if n_mod == 1:
        return pl.BlockSpec((1, N_MOD, D_MODEL), lambda i: (0, 0, 0))
    return pl.BlockSpec((rows, N_MOD, D_MODEL), lambda i: (i // blocks_per_mod, 0, 0))


def _mod_kernel(c_ref, w_ref, b_ref, o_ref):
    c = c_ref[...]
    a = (c * jax.nn.sigmoid(c)).astype(BF)
    o_ref[...] = _dot(a, w_ref[...].astype(BF)) + b_ref[...]


def _modulation(cond, w_mod, b_mod):
    n = N_MOD * D_MODEL
    return pl.pallas_call(
        _mod_kernel,
        grid=(DEPTH, n // MOD_TN),
        in_specs=[
            pl.BlockSpec((MOD_ROWS, D_MODEL), lambda l, j: (0, 0)),
            pl.BlockSpec((None, D_MODEL, MOD_TN), lambda l, j: (l, 0, j)),
            pl.BlockSpec((None, 1, MOD_TN), lambda l, j: (l, 0, j)),
        ],
        out_specs=pl.BlockSpec((None, MOD_ROWS, MOD_TN), lambda l, j: (l, 0, j)),
        out_shape=jax.ShapeDtypeStruct((DEPTH, MOD_ROWS, n), F32),
        compiler_params=_params(2),
        name="modulation",
    )(cond, w_mod, b_mod.reshape(DEPTH, 1, n))


def _ffn_kernel(*refs, mod_base, add_pos, final):
    refs = list(refs)
    x_ref = refs.pop(0)
    pos_ref = refs.pop(0) if add_pos else None
    mod_ref, g_ref, w1g_ref, w1u_ref, w2_ref = refs[:5]
    refs = refs[5:]
    fg_ref = refs.pop(0) if final else None
    o_ref = refs.pop(0)

    x = x_ref[...]
    if add_pos:
        x = x + pos_ref[...]
    sh = mod_ref[0, mod_base:mod_base + 1, :]
    sc = mod_ref[0, mod_base + 1:mod_base + 2, :]
    gt = mod_ref[0, mod_base + 2:mod_base + 3, :]
    h = _mod_norm(x, g_ref[...], sc, sh).astype(BF)
    g = _dot(h, w1g_ref[...])
    u = _dot(h, w1u_ref[...])
    a = (g * jax.nn.sigmoid(g) * u).astype(BF)
    y = _dot(a, w2_ref[...])
    xn = x + (0.5 * gt) * y
    if final:
        var = jnp.mean(xn * xn, axis=-1, keepdims=True)
        xn = xn * lax.rsqrt(var + EPS) * fg_ref[...]
    o_ref[...] = xn


def _ffn(x, mod, norm_g, w1, w2, *, layer, which, seq_tokens, pos=None, final_g=None):
    n = x.shape[0]
    tm = FFN_TM
    in_specs = [pl.BlockSpec((tm, D_MODEL), lambda i: (i, 0))]
    args = [x]
    if pos is not None:
        pos_blocks = pos.shape[0] // tm
        in_specs.append(pl.BlockSpec((tm, D_MODEL), lambda i: (i % pos_blocks, 0)))
        args.append(pos)
    in_specs += [
        _mod_spec(mod.shape[0], seq_tokens // tm),
        _const_spec((None, 1, D_MODEL), (3 * layer + 2 * which, 0, 0)),
        _const_spec((None, None, D_MODEL, D_FF), (layer, which, 0, 0)),
        _const_spec((None, None, D_MODEL, D_FF), (layer, which, 0, 1)),
        _const_spec((None, None, D_FF, D_MODEL), (layer, which, 0, 0)),
    ]
    args += [mod, norm_g, w1, w1, w2]
    if final_g is not None:
        in_specs.append(_const_spec((1, D_MODEL), (0, 0)))
        args.append(final_g.reshape(1, D_MODEL))
    kern = functools.partial(_ffn_kernel, mod_base=6 * which, add_pos=pos is not None,
                             final=final_g is not None)
    return pl.pallas_call(
        kern,
        grid=(n // tm,),
        in_specs=in_specs,
        out_specs=pl.BlockSpec((tm, D_MODEL), lambda i: (i, 0)),
        out_shape=jax.ShapeDtypeStruct((n, D_MODEL), F32),
        compiler_params=_params(1),
        name="ffn",
    )(*args)


def _mix_in_kernel(x_ref, mod_ref, g_ref, winb_ref, winc_ref, cw_ref, cb_ref, clg_ref, clb_ref,
                   sglg_ref, sglb_ref, sgw_ref, sgb_ref, yb_ref, yc_ref, pad_ref, *, seq_len):
    n_seq = TOK_BLOCK // seq_len
    x = x_ref[...]
    h = _mod_norm(x, g_ref[...], mod_ref[0, 4:5, :], mod_ref[0, 3:4, :]).astype(BF)

    zb = _dot(h, winb_ref[...])
    gl = zb[:, :CONV_WIDTH] * jax.nn.sigmoid(zb[:, CONV_WIDTH:])
    zeros = jnp.zeros((CONV_PAD, CONV_WIDTH), F32)
    shifted_rows = seq_len + CONV_SHIFT_ROWS
    for s in range(n_seq):
        padded = jnp.concatenate([zeros, gl[s * seq_len:(s + 1) * seq_len, :], zeros], axis=0)
        for r in range(SUBLANES):
            pad_ref[r, s] = padded[r:r + shifted_rows, :]
    first = CONV_PAD - CONV_K // 2
    cb, clg, clb = cb_ref[...], clg_ref[...], clb_ref[...]
    for s in range(n_seq):
        def conv_rows(r, carry, s=s):
            r0 = pl.multiple_of(r * CONV_RC, CONV_RC)
            acc = jnp.zeros((CONV_RC, CONV_WIDTH), F32)
            for k in range(CONV_K):
                o = first + k
                start = pl.multiple_of(r0 + SUBLANES * (o // SUBLANES), SUBLANES)
                acc = acc + cw_ref[k:k + 1, :] * pad_ref[o % SUBLANES, s, pl.ds(start, CONV_RC), :]
            y = _layernorm(acc + cb, clg, clb)
            y = y * jax.nn.sigmoid(y)
            yb_ref[pl.ds(pl.multiple_of(s * seq_len + r0, CONV_RC), CONV_RC), :] = y.astype(BF)
            return carry
        lax.fori_loop(0, seq_len // CONV_RC, conv_rows, 0)

    zc = jax.nn.gelu(_dot(h, winc_ref[...]))
    u = zc[:, :SG_WIDTH]
    v = _layernorm(zc[:, SG_WIDTH:], sglg_ref[...], sglb_ref[...]).astype(BF)
    head = lax.broadcasted_iota(jnp.int32, (SG_CHUNK, SG_WIDTH), 1) // SG_HEAD_DIM
    zero = jnp.zeros((SG_CHUNK, SG_WIDTH), BF)
    for n in range(TOK_BLOCK // SG_CHUNK):
        vn = v[n * SG_CHUNK:(n + 1) * SG_CHUNK, :]
        vcat = jnp.concatenate([jnp.where(head == hh, vn, zero) for hh in range(SG_HEADS)], axis=0)
        sgate = _dot(sgw_ref[...], vcat) + sgb_ref[...]
        yc_ref[n * SG_CHUNK:(n + 1) * SG_CHUNK, :] = (u[n * SG_CHUNK:(n + 1) * SG_CHUNK, :] * sgate).astype(BF)


def _mix_in(x, mod, norm_g, w_in, conv_w, conv_b, conv_ln_g, conv_ln_b, sg_ln_g, sg_ln_b, sgw_cat, sgb_full,
            *, layer, seq_len):
    n = x.shape[0]
    n_seq = TOK_BLOCK // seq_len
    vec = lambda width: _const_spec((None, 1, width), (layer, 0, 0))
    kern = functools.partial(_mix_in_kernel, seq_len=seq_len)
    return pl.pallas_call(
        kern,
        grid=(n // TOK_BLOCK,),
        in_specs=[
            pl.BlockSpec((TOK_BLOCK, D_MODEL), lambda i: (i, 0)),
            _mod_spec(mod.shape[0], max(seq_len // TOK_BLOCK, 1)),
            _const_spec((None, 1, D_MODEL), (3 * layer + 1, 0, 0)),
            _const_spec((None, D_MODEL, BRANCH_COLS), (layer, 0, S5_WIDTH // BRANCH_COLS)),
            _const_spec((None, D_MODEL, BRANCH_COLS), (layer, 0, S5_WIDTH // BRANCH_COLS + 1)),
            _const_spec((None, CONV_K, CONV_WIDTH), (layer, 0, 0)),
            vec(CONV_WIDTH), vec(CONV_WIDTH), vec(CONV_WIDTH), vec(SG_WIDTH), vec(SG_WIDTH),
            _const_spec((None, SG_CHUNK, SG_HEADS * SG_CHUNK), (layer, 0, 0)),
            _const_spec((None, SG_CHUNK, SG_WIDTH), (layer, 0, 0)),
        ],
        out_specs=[
            pl.BlockSpec((TOK_BLOCK, CONV_WIDTH), lambda i: (i, 0)),
            pl.BlockSpec((TOK_BLOCK, SG_WIDTH), lambda i: (i, 0)),
        ],
        out_shape=[
            jax.ShapeDtypeStruct((n, CONV_WIDTH), BF),
            jax.ShapeDtypeStruct((n, SG_WIDTH), BF),
        ],
        scratch_shapes=[pltpu.VMEM((SUBLANES, n_seq, seq_len + CONV_SHIFT_ROWS, CONV_WIDTH), F32)],
        compiler_params=_params(1),
        name="mix_in",
    )(x, mod, norm_g, w_in, w_in, conv_w, conv_b, conv_ln_g, conv_ln_b, sg_ln_g, sg_ln_b, sgw_cat, sgb_full)


def _s5_prep_kernel(acol_re_ref, acol_im_ref, dtcol_ref, arow_re_ref, arow_im_ref, dtrow_ref,
                    b_re_ref, b_im_ref, c4_re_ref, c4_im_ref, mt_ref, wint_ref, woutt_ref):
    P, T, H = S5_STATE, S5_T, S5_GROUP

    def cpow(a_re, a_im, dt, e):
        mag = jnp.exp(a_re * dt * e)
        ang = a_im * dt * e
        return mag * jnp.cos(ang), mag * jnp.sin(ang)

    a_re, a_im = acol_re_ref[...], acol_im_ref[...]
    dt = jnp.exp(dtcol_ref[...])
    ab_re, ab_im = cpow(a_re, a_im, dt, 1.0)
    den = a_re * a_re + a_im * a_im
    nr, ni = ab_re - 1.0, ab_im
    q_re = (nr * a_re + ni * a_im) / den
    q_im = (ni * a_re - nr * a_im) / den
    lanes = 2 * S5_CL
    expand = (lax.broadcasted_iota(jnp.int32, (H, lanes), 1) % H
              == lax.broadcasted_iota(jnp.int32, (H, lanes), 0)).astype(F32)

    def tile_b(ref):
        fwd = _dot_exact(ref[0], expand)
        bwd = _dot_exact(ref[1], expand)
        return jnp.concatenate([fwd, fwd, bwd, bwd], axis=0)

    bt_re, bt_im = tile_b(b_re_ref), tile_b(b_im_ref)
    bb_re = q_re * bt_re - q_im * bt_im
    bb_im = q_re * bt_im + q_im * bt_re

    def weight_b(exponent, width):
        row = lax.broadcasted_iota(jnp.int32, (S5_NS, width), 0)
        blk = lax.broadcasted_iota(jnp.int32, (S5_NS, width), 1) // H
        e, live = exponent(row >= 2 * P, blk)
        p_re, p_im = cpow(a_re, a_im, dt, e.astype(F32))
        b_re, b_im = bb_re[:, :width], bb_im[:, :width]
        w = jnp.where((row // P) % 2 == 1, p_re * b_im + p_im * b_re, p_re * b_re - p_im * b_im)
        return w if live is None else jnp.where(live, w, 0.0)

    wint_ref[...] = weight_b(lambda bwd, blk: (jnp.where(bwd, blk, T - 1 - blk), None), S5_CL).astype(BF)

    def lag_exponent(bwd, blk):
        m = T - 1 - blk
        live = ((bwd & (m <= 0)) | (jnp.logical_not(bwd) & (m >= 0))) & (blk < 2 * T - 1)
        return jnp.abs(m), live

    ystack = weight_b(lag_exponent, lanes)

    ar_re, ar_im = arow_re_ref[...], arow_im_ref[...]
    dtr = jnp.exp(dtrow_ref[...])
    lane = lax.broadcasted_iota(jnp.int32, (1, S5_NS), 1)
    lane_bwd = lane >= 2 * P
    lane_im = (lane // P) % 2 == 1
    c4_re, c4_im = c4_re_ref[...], c4_im_ref[...]
    c_signed = jnp.where(lane_im, -c4_im, c4_re)
    krev = _dot_exact(c_signed, ystack)
    mt_ref[...] = jnp.concatenate(
        [pltpu.roll(krev, (lanes - H * (T - 1 - i)) % lanes, 1)[:, :S5_CL] for i in range(T)], axis=0).astype(BF)
    tok = lax.broadcasted_iota(jnp.int32, (S5_CL, S5_NS), 0) // H
    e_out = jnp.where(lane_bwd, T - tok, tok + 1).astype(F32)
    p_re, p_im = cpow(ar_re, ar_im, dtr, e_out)
    ct_re = jnp.concatenate([c4_re] * T, axis=0)
    ct_im = jnp.concatenate([c4_im] * T, axis=0)
    woutt_ref[...] = jnp.where(lane_im, -(ct_re * p_im + ct_im * p_re), ct_re * p_re - ct_im * p_im).astype(BF)


def _s5_operators(a_re, a_im, log_dt, b_re, b_im, c_re, c_im):
    G, P, H = S5_GROUPS, S5_STATE, S5_GROUP
    n = DEPTH * G
    rep = lambda a: jnp.broadcast_to(a.transpose(0, 2, 1, 3)[:, :, :, None, :], (DEPTH, G, 2, 2, P)).reshape(n, S5_NS)
    dt4 = rep(jnp.broadcast_to(log_dt[..., None], (DEPTH, 2, G, P)))
    a4_re, a4_im = rep(a_re), rep(a_im)
    col = lambda a: a.reshape(n, S5_NS, 1)
    rowv = lambda a: a.reshape(n, 1, S5_NS)
    bgd = lambda b: b.transpose(0, 2, 1, 3, 4).reshape(n, 2, P, H)
    c4 = lambda c: jnp.broadcast_to(c.transpose(0, 2, 3, 1, 4)[:, :, :, :, None, :],
                                    (DEPTH, G, H, 2, 2, P)).reshape(n, H, S5_NS)
    blk = lambda r, c: pl.BlockSpec((None, r, c), lambda i: (i, 0, 0))
    op = jax.ShapeDtypeStruct((n, S5_CL, S5_CL), BF)
    mt, wint, woutt = pl.pallas_call(
        _s5_prep_kernel,
        grid=(n,),
        in_specs=[blk(S5_NS, 1), blk(S5_NS, 1), blk(S5_NS, 1), blk(1, S5_NS), blk(1, S5_NS), blk(1, S5_NS),
                  pl.BlockSpec((None, 2, P, H), lambda i: (i, 0, 0, 0)),
                  pl.BlockSpec((None, 2, P, H), lambda i: (i, 0, 0, 0)),
                  blk(H, S5_NS), blk(H, S5_NS)],
        out_specs=[blk(S5_CL, S5_CL), blk(S5_NS, S5_CL), blk(S5_CL, S5_NS)],
        out_shape=[op, op, op],
        compiler_params=_params(1),
        name="s5_prep",
    )(col(a4_re), col(a4_im), col(dt4), rowv(a4_re), rowv(a4_im), rowv(dt4), bgd(b_re), bgd(b_im), c4(c_re), c4(c_im))
    return dict(mt=mt, wint=wint, woutt=woutt, acol_re=col(a4_re), acol_im=col(a4_im), dtcol=col(dt4))


def _s5_in_kernel(x_ref, mod_ref, g_ref, wt_ref, zt_ref, *, mod_rows):
    g = g_ref[...]
    rows = S5_CB // mod_rows

    def normed(j):
        x = x_ref[:, j * D_MODEL:(j + 1) * D_MODEL]
        parts = [_mod_norm(x[m * rows:(m + 1) * rows, :], g, mod_ref[m, 4:5, :], mod_ref[m, 3:4, :])
                 for m in range(mod_rows)]
        return parts[0] if mod_rows == 1 else jnp.concatenate(parts, axis=0)

    for j in range(0, S5_T, 2):
        h = jnp.concatenate([normed(j), normed(j + 1)], axis=0).astype(BF)
        zt = lax.dot_general(wt_ref[...], h, (((1,), (1,)), ((), ())), preferred_element_type=F32)
        zt_ref[j] = zt[:, :S5_CB]
        zt_ref[j + 1] = zt[:, S5_CB:]


def _s5_in(x, mod, norm_g, w_in_t, *, layer, seq_chunks):
    nc = x.shape[0] // S5_T
    xv = x.reshape(nc, S5_T * D_MODEL)
    mod_rows = 1 if mod.shape[0] == 1 else S5_CB // seq_chunks
    in_specs = [
        pl.BlockSpec((S5_CB, S5_T * D_MODEL), lambda i: (i, 0)),
        _mod_spec(mod.shape[0], 1, rows=mod_rows),
        _const_spec((None, 1, D_MODEL), (3 * layer + 1, 0, 0)),
        _const_spec((None, S5_WIDTH, D_MODEL), (layer, 0, 0)),
    ]
    return pl.pallas_call(
        functools.partial(_s5_in_kernel, mod_rows=mod_rows),
        grid=(nc // S5_CB,),
        in_specs=in_specs,
        out_specs=pl.BlockSpec((S5_T, S5_WIDTH, S5_CB), lambda i: (0, 0, i)),
        out_shape=jax.ShapeDtypeStruct((S5_T, S5_WIDTH, nc), F32),
        compiler_params=_params(1),
        name="s5_in",
    )(xv, mod, norm_g, w_in_t)


def _s5_core_kernel(*refs, seq_chunks, has_h0, want_final):
    refs = list(refs)
    zt_ref, mt_ref, wint_ref, woutt_ref, acre_ref, acim_ref, dtc_ref = refs[:7]
    refs = refs[7:]
    h0_ref = refs.pop(0) if has_h0 else None
    yt_ref = refs.pop(0)
    hf_ref = refs.pop(0) if want_final else None
    P, T, K = S5_STATE, S5_T, seq_chunks
    nc = zt_ref.shape[2]
    n_seq = nc // K
    lane = lax.broadcasted_iota(jnp.int32, (1, nc), 1)
    kpos = lane % K
    n_steps = K.bit_length() - 1
    if want_final:
        chunk = lax.broadcasted_iota(jnp.int32, (nc, LANES), 0)
        seq = lax.broadcasted_iota(jnp.int32, (nc, LANES), 1)
        pick = [(chunk == seq * K + (K - 1)).astype(F32), (chunk == seq * K).astype(F32)]

    for g in range(S5_GB):
        r0 = g * S5_GROUP
        xt = jnp.concatenate([zt_ref[j, r0:r0 + S5_GROUP, :] for j in range(T)], axis=0).astype(BF)
        v = _dot(wint_ref[g], xt)
        states, finals = [], []
        for d in range(2):
            base = 2 * P * d
            a_re = acre_ref[g, base:base + P, :]
            a_im = acim_ref[g, base:base + P, :]
            dt = jnp.exp(dtc_ref[g, base:base + P, :])

            def coef(e, a_re=a_re, a_im=a_im, dt=dt):
                mag = jnp.exp(a_re * dt * e)
                ang = a_im * dt * e
                return mag * jnp.cos(ang), mag * jnp.sin(ang)

            shift = lambda x, s, d=d: pltpu.roll(x, s if d == 0 else nc - s, 1)
            first_k = 0 if d == 0 else K - 1
            first = kpos == first_k
            vr, vi = v[base:base + P, :], v[base + P:base + 2 * P, :]
            if has_h0:
                hr = jnp.zeros((P, nc), F32)
                hi = jnp.zeros((P, nc), F32)
                for s in range(n_seq):
                    here = lane == s * K + first_k
                    hr = jnp.where(here, h0_ref[g, base:base + P, s:s + 1], hr)
                    hi = jnp.where(here, h0_ref[g, base + P:base + 2 * P, s:s + 1], hi)
            else:
                hr = hi = 0.0
            sr = jnp.where(first, hr, shift(vr, 1))
            si = jnp.where(first, hi, shift(vi, 1))
            for t in range(n_steps):
                s = 1 << t
                cr, ci = coef(float(T * s))
                valid = (kpos >= s) if d == 0 else (kpos < K - s)
                rr = jnp.where(valid, shift(sr, s), 0.0)
                ri = jnp.where(valid, shift(si, s), 0.0)
                sr, si = sr + (cr * rr - ci * ri), si + (cr * ri + ci * rr)
            states += [sr, si]
            if want_final:
                cr, ci = coef(float(T))
                finals += [cr * sr - ci * si + vr, cr * si + ci * sr + vi]
        s_in = jnp.concatenate(states, axis=0).astype(BF)
        y = _dot(mt_ref[g], xt) + _dot(woutt_ref[g], s_in)
        for i in range(T):
            yt_ref[i, r0:r0 + S5_GROUP, :] = y[i * S5_GROUP:(i + 1) * S5_GROUP, :]
        if want_final:
            for d in range(2):
                f = jnp.concatenate(finals[2 * d:2 * d + 2], axis=0)
                hf_ref[g, 2 * P * d:2 * P * (d + 1), :] = _dot_exact(f, pick[d])


def _s5_core(zt, ops, *, layer, seq_chunks, h0=None, want_final=False):
    nc = zt.shape[2]
    rows = S5_GB * S5_GROUP
    steps = S5_GROUPS // S5_GB
    gspec = lambda r, c: pl.BlockSpec((S5_GB, r, c), lambda i: (layer * steps + i, 0, 0))
    slab = pl.BlockSpec((S5_T, rows, nc), lambda i: (0, i, 0))
    in_specs = [slab, gspec(S5_CL, S5_CL), gspec(S5_NS, S5_CL), gspec(S5_CL, S5_NS),
                gspec(S5_NS, 1), gspec(S5_NS, 1), gspec(S5_NS, 1)]
    args = [zt, ops["mt"], ops["wint"], ops["woutt"], ops["acol_re"], ops["acol_im"], ops["dtcol"]]
    if h0 is not None:
        in_specs.append(pl.BlockSpec((S5_GB, S5_NS, h0.shape[2]), lambda i: (i, 0, 0)))
        args.append(h0)
    out_specs = [slab]
    out_shape = [jax.ShapeDtypeStruct(zt.shape, F32)]
    if want_final:
        assert nc // seq_chunks <= LANES
        out_specs.append(pl.BlockSpec((S5_GB, S5_NS, LANES), lambda i: (i, 0, 0)))
        out_shape.append(jax.ShapeDtypeStruct((S5_GROUPS, S5_NS, LANES), F32))
    kern = functools.partial(_s5_core_kernel, seq_chunks=seq_chunks, has_h0=h0 is not None, want_final=want_final)
    out = pl.pallas_call(
        kern,
        grid=(steps,),
        in_specs=in_specs,
        out_specs=out_specs,
        out_shape=out_shape,
        compiler_params=_params(1),
        name="s5_core",
    )(*args)
    return out if want_final else (out[0], None)


def _s5_out_kernel(yt_ref, zt_ref, d_ref, wglut_ref, wa_ref, o_ref):
    d = d_ref[...]
    for j in range(0, S5_T, 2):
        pre = jnp.concatenate([yt_ref[j] + d * zt_ref[j], yt_ref[j + 1] + d * zt_ref[j + 1]], axis=1)
        ya = jax.nn.gelu(pre)
        ya = ya * jax.nn.sigmoid(_dot(wglut_ref[...], ya.astype(BF)))
        pa = lax.dot_general(ya.astype(BF), wa_ref[...], (((0,), (0,)), ((), ())), preferred_element_type=F32)
        o_ref[:, j * D_MODEL:(j + 1) * D_MODEL] = pa[:S5_CB, :].astype(BF)
        o_ref[:, (j + 1) * D_MODEL:(j + 2) * D_MODEL] = pa[S5_CB:, :].astype(BF)


def _s5_out(yt, zt, s5_d_col, w_glu_t, w_br_a, *, layer):
    nc = yt.shape[2]
    slab = pl.BlockSpec((S5_T, S5_WIDTH, S5_CB), lambda i: (0, 0, i))
    out = pl.pallas_call(
        _s5_out_kernel,
        grid=(nc // S5_CB,),
        in_specs=[slab, slab,
                  _const_spec((None, S5_WIDTH, 1), (layer, 0, 0)),
                  _const_spec((None, S5_WIDTH, S5_WIDTH), (layer, 0, 0)),
                  _const_spec((None, S5_WIDTH, D_MODEL), (layer, 0, 0))],
        out_specs=pl.BlockSpec((S5_CB, S5_T * D_MODEL), lambda i: (i, 0)),
        out_shape=jax.ShapeDtypeStruct((nc, S5_T * D_MODEL), BF),
        compiler_params=_params(1),
        name="s5_out",
    )(yt, zt, s5_d_col, w_glu_t, w_br_a)
    return out.reshape(nc * S5_T, D_MODEL)


def _mix_out_kernel(x_ref, mod_ref, g_ref, pa_ref, yb_ref, yc_ref, wb_ref, wc_ref, wgate_ref, bgate_ref,
                    wout_ref, o_ref):
    x = x_ref[...]
    h = _mod_norm(x, g_ref[...], mod_ref[0, 4:5, :], mod_ref[0, 3:4, :]).astype(BF)

    def gate(k):
        lo = k * D_MODEL
        return jax.nn.sigmoid(_dot(h, wgate_ref[:, lo:lo + D_MODEL]) + bgate_ref[:, lo:lo + D_MODEL])

    merged = gate(0) * pa_ref[...].astype(F32)
    merged = merged + gate(1) * _dot(yb_ref[...], wb_ref[...])
    merged = merged + gate(2) * _dot(yc_ref[...], wc_ref[...])
    y = _dot(merged.astype(BF), wout_ref[...])
    o_ref[...] = x + mod_ref[0, 5:6, :] * y


def _mix_out(x, mod, norm_g, pa, yb, yc, w_b, w_c, w_gate, b_gate, w_out, *, layer, seq_tokens):
    n = x.shape[0]
    tm = FFN_TM
    tok = lambda c: pl.BlockSpec((tm, c), lambda i: (i, 0))
    return pl.pallas_call(
        _mix_out_kernel,
        grid=(n // tm,),
        in_specs=[
            tok(D_MODEL),
            _mod_spec(mod.shape[0], seq_tokens // tm),
            _const_spec((None, 1, D_MODEL), (3 * layer + 1, 0, 0)),
            tok(D_MODEL), tok(CONV_WIDTH), tok(SG_WIDTH),
            _const_spec((None, CONV_WIDTH, D_MODEL), (layer, 0, 0)),
            _const_spec((None, SG_WIDTH, D_MODEL), (layer, 0, 0)),
            _const_spec((None, D_MODEL, 3 * D_MODEL), (layer, 0, 0)),
            _const_spec((None, 1, 3 * D_MODEL), (layer, 0, 0)),
            _const_spec((None, D_MODEL, D_MODEL), (layer, 0, 0)),
        ],
        out_specs=tok(D_MODEL),
        out_shape=jax.ShapeDtypeStruct((n, D_MODEL), F32),
        compiler_params=_params(1),
        name="mix_out",
    )(x, mod, norm_g, pa, yb, yc, w_b, w_c, w_gate, b_gate, w_out)


def _grid_pos_embed(n_tokens, dim):
    rows = n_tokens // GRID_W
    rr, cc = jnp.meshgrid(jnp.arange(rows, dtype=F32), jnp.arange(GRID_W, dtype=F32), indexing='ij')
    quarter = dim // 4
    omega = 1.0 / (10000.0 ** (jnp.arange(quarter, dtype=F32) / quarter))

    def emb(p):
        ang = p.reshape(-1)[:, None] * omega[None, :]
        return jnp.concatenate([jnp.sin(ang), jnp.cos(ang)], axis=-1)

    return jnp.concatenate([emb(rr), emb(cc)], axis=-1)


def kernel(x_prompt, x_sample, state_ssm, c, c_ctx, w_mod, b_mod, norm_g, ffn_w1, ffn_w2, w_in, w_gate, b_gate,
           s5_a_re, s5_a_im, s5_log_dt, s5_b_re, s5_b_im, s5_c_re, s5_c_im, s5_d, s5_w_glu, w_br_a, conv_w,
           conv_b, conv_ln_g, conv_ln_b, w_br_b, sg_ln_g, sg_ln_b, sg_w, sg_b, w_br_c, w_out, final_g):
    batch, seq, _ = x_prompt.shape
    dec_batch, dec_seq, _ = x_sample.shape
    assert (batch * seq) % TOK_BLOCK == 0 and TOK_BLOCK % seq == 0 and dec_seq % TOK_BLOCK == 0
    assert seq % SG_CHUNK == 0 and seq % S5_T == 0 and dec_seq % S5_T == 0
    ctx_chunks, smp_chunks = seq // S5_T, dec_seq // S5_T
    assert S5_CB % ctx_chunks == 0 and S5_CB % smp_chunks == 0
    assert ctx_chunks & (ctx_chunks - 1) == 0 and smp_chunks & (smp_chunks - 1) == 0
    assert 1 + dec_batch <= MOD_ROWS

    cond = jnp.zeros((MOD_ROWS, D_MODEL), F32).at[0].set(c_ctx).at[1:1 + dec_batch].set(c)
    mod_all = _modulation(cond, w_mod, b_mod).reshape(DEPTH, MOD_ROWS, N_MOD, D_MODEL)
    pos = _grid_pos_embed(dec_seq, D_MODEL)

    w1 = ffn_w1.astype(BF)
    w2 = ffn_w2.astype(BF)
    w_in_b = w_in.astype(BF)
    w_in_t = w_in[:, :, :S5_WIDTH].transpose(0, 2, 1).astype(BF)
    w_glu_t = s5_w_glu.transpose(0, 2, 1).astype(BF)
    w_a, w_b, w_c = w_br_a.astype(BF), w_br_b.astype(BF), w_br_c.astype(BF)
    w_gate_b, w_out_b = w_gate.astype(BF), w_out.astype(BF)
    norm_rows = norm_g.reshape(DEPTH * 3, 1, D_MODEL)
    rows = lambda a: a.reshape(DEPTH, 1, -1)
    sgw_cat = sg_w.transpose(0, 2, 1, 3).reshape(DEPTH, SG_CHUNK, SG_HEADS * SG_CHUNK).astype(BF)
    sgb_full = jnp.repeat(sg_b.transpose(0, 2, 1), SG_HEAD_DIM, axis=2)
    s5_ops = _s5_operators(s5_a_re, s5_a_im, s5_log_dt, s5_b_re, s5_b_im, s5_c_re, s5_c_im)
    s5_d_col = s5_d.reshape(DEPTH, S5_WIDTH, 1)

    groups = {
        "ctx": dict(x=x_prompt.reshape(batch * seq, D_MODEL), n_seq=batch, seq_len=seq, mod_rows=slice(0, 1),
                    mod_tokens=batch * seq, pos=None),
        "smp": dict(x=x_sample.reshape(dec_batch * dec_seq, D_MODEL), n_seq=dec_batch, seq_len=dec_seq,
                    mod_rows=slice(1, 1 + dec_batch), mod_tokens=dec_seq, pos=pos),
    }
    ctx_states = []
    for l in range(DEPTH):
        for name, gr in groups.items():
            mod = mod_all[l, gr["mod_rows"]]
            n_seq, seq_len = gr["n_seq"], gr["seq_len"]
            x = _ffn(gr["x"], mod, norm_rows, w1, w2, layer=l, which=0, seq_tokens=gr["mod_tokens"],
                     pos=gr["pos"] if l == 0 else None)
            yb, yc = _mix_in(x, mod, norm_rows, w_in_b, conv_w, rows(conv_b), rows(conv_ln_g), rows(conv_ln_b),
                             rows(sg_ln_g), rows(sg_ln_b), sgw_cat, sgb_full, layer=l, seq_len=seq_len)
            zt = _s5_in(x, mod, norm_rows, w_in_t, layer=l, seq_chunks=seq_len // S5_T)
            if name == "ctx":
                yt, hf = _s5_core(zt, s5_ops, layer=l, seq_chunks=seq_len // S5_T, want_final=True)
                hf = hf[:, :, :n_seq].reshape(S5_GROUPS, 2, 2, S5_STATE, n_seq)
                ctx_states.append(hf.transpose(4, 1, 0, 3, 2))
            else:
                h0 = state_ssm[:, l].transpose(2, 1, 4, 3, 0).reshape(S5_GROUPS, S5_NS, n_seq)
                yt, _ = _s5_core(zt, s5_ops, layer=l, seq_chunks=seq_len // S5_T, h0=h0)
            pa = _s5_out(yt, zt, s5_d_col, w_glu_t, w_a, layer=l)
            x = _mix_out(x, mod, norm_rows, pa, yb, yc, w_b, w_c, w_gate_b, rows(b_gate), w_out_b,
                         layer=l, seq_tokens=gr["mod_tokens"])
            x = _ffn(x, mod, norm_rows, w1, w2, layer=l, which=1, seq_tokens=gr["mod_tokens"],
                     final_g=final_g if l == DEPTH - 1 else None)
            gr["x"] = x
    y_prompt = groups["ctx"]["x"].reshape(batch, seq, D_MODEL)
    y_sample = groups["smp"]["x"].reshape(dec_batch, dec_seq, D_MODEL)
    new_state_ssm = jnp.stack(ctx_states, axis=1)
    return (y_prompt, y_sample, new_state_ssm)
```

```python
import functools

import jax
import jax.numpy as jnp
from jax import lax
from jax.experimental import pallas as pl
from jax.experimental.pallas import tpu as pltpu

D_MODEL = 1024
DEPTH = 2
GRID_W = 64
D_FF = 2816
S5_WIDTH = 512
S5_GROUP = 16
S5_GROUPS = 32
S5_STATE = 64
CONV_WIDTH = 256
CONV_K = 31
SG_WIDTH = 256
SG_CHUNK = 128
SG_HEADS = 4
SG_HEAD_DIM = SG_WIDTH // SG_HEADS
BRANCH_COLS = 2 * CONV_WIDTH
IN_COLS = S5_WIDTH + 2 * BRANCH_COLS
N_MOD = 9
EPS = 1e-6

LANES = 128
SUBLANES = 8
S5_T = 16
S5_CL = S5_T * S5_GROUP
S5_NS = 4 * S5_STATE
S5_GB = 8
S5_PREP_GB = 4
S5_POW_ROWS = SUBLANES * (S5_T // SUBLANES + 1)
S5_CB = 128
MOD_ROWS = 16
MOD_TN = 1152
TOK_BLOCK = 1024
FFN_TM = 512
CONV_RC = 64
CONV_PAD = 16
CONV_SHIFT_ROWS = SUBLANES * ((CONV_PAD - CONV_K // 2 + CONV_K - 1) // SUBLANES)
VMEM_LIMIT = 56 * 1024 * 1024

BF = jnp.bfloat16
F32 = jnp.float32
HIGHEST = lax.Precision.HIGHEST


def _dot(a, b):
    return jnp.dot(a, b, preferred_element_type=F32)


def _dot_exact(a, b):
    return jnp.dot(a, b, preferred_element_type=F32, precision=HIGHEST)


def _const_spec(block, index):
    return pl.BlockSpec(block, lambda *_: index, pipeline_mode=pl.Buffered(1))


def _params(n_axes=1):
    return pltpu.CompilerParams(dimension_semantics=("arbitrary",) * n_axes,
                                vmem_limit_bytes=VMEM_LIMIT)


def _mod_norm(x, g, sc, sh):
    var = jnp.mean(x * x, axis=-1, keepdims=True)
    return (x * lax.rsqrt(var + EPS) * g) * (1.0 + sc) + sh


def _layernorm(x, g, b):
    mu = jnp.mean(x, axis=-1, keepdims=True)
    xc = x - mu
    var = jnp.mean(xc * xc, axis=-1, keepdims=True)
    return xc * lax.rsqrt(var + EPS) * g + b


def _mod_spec(n_mod, blocks_per_mod, rows=1):
    if n_mod == 1:
        return pl.BlockSpec((1, N_MOD, D_MODEL), lambda i: (0, 0, 0))
    return pl.BlockSpec((rows, N_MOD, D_MODEL), lambda i: (i // blocks_per_mod, 0, 0))


def _mod_kernel(c_ref, w_ref, b_ref, o_ref):
    c = c_ref[...]
    a = (c * jax.nn.sigmoid(c)).astype(BF)
    o_ref[...] = _dot(a, w_ref[...].astype(BF)) + b_ref[...]


def _modulation(cond, w_mod, b_mod):
    n = N_MOD * D_MODEL
    return pl.pallas_call(
        _mod_kernel,
        grid=(DEPTH, n // MOD_TN),
        in_specs=[
            pl.BlockSpec((MOD_ROWS, D_MODEL), lambda l, j: (0, 0)),
            pl.BlockSpec((None, D_MODEL, MOD_TN), lambda l, j: (l, 0, j)),
            pl.BlockSpec((None, 1, MOD_TN), lambda l, j: (l, 0, j)),
        ],
        out_specs=pl.BlockSpec((None, MOD_ROWS, MOD_TN), lambda l, j: (l, 0, j)),
        out_shape=jax.ShapeDtypeStruct((DEPTH, MOD_ROWS, n), F32),
        compiler_params=_params(2),
        name="modulation",
    )(cond, w_mod, b_mod.reshape(DEPTH, 1, n))


def _ffn_kernel(*refs, mod_base, add_pos, final):
    refs = list(refs)
    x_ref = refs.pop(0)
    pos_ref = refs.pop(0) if add_pos else None
    mod_ref, g_ref, w1g_ref, w1u_ref, w2_ref = refs[:5]
    refs = refs[5:]
    fg_ref = refs.pop(0) if final else None
    o_ref = refs.pop(0)

    x = x_ref[...]
    if add_pos:
        x = x + pos_ref[...]
    sh = mod_ref[0, mod_base:mod_base + 1, :]
    sc = mod_ref[0, mod_base + 1:mod_base + 2, :]
    gt = mod_ref[0, mod_base + 2:mod_base + 3, :]
    h = _mod_norm(x, g_ref[...], sc, sh).astype(BF)
    g = _dot(h, w1g_ref[...])
    u = _dot(h, w1u_ref[...])
    a = (g * jax.nn.sigmoid(g) * u).astype(BF)
    y = _dot(a, w2_ref[...])
    xn = x + (0.5 * gt) * y
    if final:
        var = jnp.mean(xn * xn, axis=-1, keepdims=True)
        xn = xn * lax.rsqrt(var + EPS) * fg_ref[...]
    o_ref[...] = xn


def _ffn(x, mod, norm_g, w1, w2, *, layer, which, seq_tokens, pos=None, final_g=None):
    n = x.shape[0]
    tm = FFN_TM
    in_specs = [pl.BlockSpec((tm, D_MODEL), lambda i: (i, 0))]
    args = [x]
    if pos is not None:
        pos_blocks = pos.shape[0] // tm
        in_specs.append(pl.BlockSpec((tm, D_MODEL), lambda i: (i % pos_blocks, 0)))
        args.append(pos)
    in_specs += [
        _mod_spec(mod.shape[0], seq_tokens // tm),
        _const_spec((None, 1, D_MODEL), (3 * layer + 2 * which, 0, 0)),
        _const_spec((None, None, D_MODEL, D_FF), (layer, which, 0, 0)),
        _const_spec((None, None, D_MODEL, D_FF), (layer, which, 0, 1)),
        _const_spec((None, None, D_FF, D_MODEL), (layer, which, 0, 0)),
    ]
    args += [mod, norm_g, w1, w1, w2]
    if final_g is not None:
        in_specs.append(_const_spec((1, D_MODEL), (0, 0)))
        args.append(final_g.reshape(1, D_MODEL))
    kern = functools.partial(_ffn_kernel, mod_base=6 * which, add_pos=pos is not None,
                             final=final_g is not None)
    return pl.pallas_call(
        kern,
        grid=(n // tm,),
        in_specs=in_specs,
        out_specs=pl.BlockSpec((tm, D_MODEL), lambda i: (i, 0)),
        out_shape=jax.ShapeDtypeStruct((n, D_MODEL), F32),
        compiler_params=_params(1),
        name="ffn",
    )(*args)


def _mix_in_kernel(x_ref, mod_ref, g_ref, winb_ref, winc_ref, cw_ref, cb_ref, clg_ref, clb_ref,
                   sglg_ref, sglb_ref, sgw_ref, sgb_ref, yb_ref, yc_ref, pad_ref, *, seq_len):
    n_seq = TOK_BLOCK // seq_len
    x = x_ref[...]
    h = _mod_norm(x, g_ref[...], mod_ref[0, 4:5, :], mod_ref[0, 3:4, :]).astype(BF)

    zb = _dot(h, winb_ref[...])
    gl = zb[:, :CONV_WIDTH] * jax.nn.sigmoid(zb[:, CONV_WIDTH:])
    zeros = jnp.zeros((CONV_PAD, CONV_WIDTH), F32)
    shifted_rows = seq_len + CONV_SHIFT_ROWS
    for s in range(n_seq):
        padded = jnp.concatenate([zeros, gl[s * seq_len:(s + 1) * seq_len, :], zeros], axis=0)
        for r in range(SUBLANES):
            pad_ref[r, s] = padded[r:r + shifted_rows, :]
    first = CONV_PAD - CONV_K // 2
    cb, clg, clb = cb_ref[...], clg_ref[...], clb_ref[...]
    for s in range(n_seq):
        def conv_rows(r, carry, s=s):
            r0 = pl.multiple_of(r * CONV_RC, CONV_RC)
            acc = jnp.zeros((CONV_RC, CONV_WIDTH), F32)
            for k in range(CONV_K):
                o = first + k
                start = pl.multiple_of(r0 + SUBLANES * (o // SUBLANES), SUBLANES)
                acc = acc + cw_ref[k:k + 1, :] * pad_ref[o % SUBLANES, s, pl.ds(start, CONV_RC), :]
            y = _layernorm(acc + cb, clg, clb)
            y = y * jax.nn.sigmoid(y)
            yb_ref[pl.ds(pl.multiple_of(s * seq_len + r0, CONV_RC), CONV_RC), :] = y.astype(BF)
            return carry
        lax.fori_loop(0, seq_len // CONV_RC, conv_rows, 0)

    zc = jax.nn.gelu(_dot(h, winc_ref[...]))
    u = zc[:, :SG_WIDTH]
    v = _layernorm(zc[:, SG_WIDTH:], sglg_ref[...], sglb_ref[...]).astype(BF)
    head = lax.broadcasted_iota(jnp.int32, (SG_CHUNK, SG_WIDTH), 1) // SG_HEAD_DIM
    zero = jnp.zeros((SG_CHUNK, SG_WIDTH), BF)
    for n in range(TOK_BLOCK // SG_CHUNK):
        vn = v[n * SG_CHUNK:(n + 1) * SG_CHUNK, :]
        vcat = jnp.concatenate([jnp.where(head == hh, vn, zero) for hh in range(SG_HEADS)], axis=0)
        sgate = _dot(sgw_ref[...], vcat) + sgb_ref[...]
        yc_ref[n * SG_CHUNK:(n + 1) * SG_CHUNK, :] = (u[n * SG_CHUNK:(n + 1) * SG_CHUNK, :] * sgate).astype(BF)


def _mix_in(x, mod, norm_g, w_in, conv_w, conv_b, conv_ln_g, conv_ln_b, sg_ln_g, sg_ln_b, sgw_cat, sgb_full,
            *, layer, seq_len):
    n = x.shape[0]
    n_seq = TOK_BLOCK // seq_len
    vec = lambda width: _const_spec((None, 1, width), (layer, 0, 0))
    kern = functools.partial(_mix_in_kernel, seq_len=seq_len)
    return pl.pallas_call(
        kern,
        grid=(n // TOK_BLOCK,),
        in_specs=[
            pl.BlockSpec((TOK_BLOCK, D_MODEL), lambda i: (i, 0)),
            _mod_spec(mod.shape[0], max(seq_len // TOK_BLOCK, 1)),
            _const_spec((None, 1, D_MODEL), (3 * layer + 1, 0, 0)),
            _const_spec((None, D_MODEL, BRANCH_COLS), (layer, 0, S5_WIDTH // BRANCH_COLS)),
            _const_spec((None, D_MODEL, BRANCH_COLS), (layer, 0, S5_WIDTH // BRANCH_COLS + 1)),
            _const_spec((None, CONV_K, CONV_WIDTH), (layer, 0, 0)),
            vec(CONV_WIDTH), vec(CONV_WIDTH), vec(CONV_WIDTH), vec(SG_WIDTH), vec(SG_WIDTH),
            _const_spec((None, SG_CHUNK, SG_HEADS * SG_CHUNK), (layer, 0, 0)),
            _const_spec((None, SG_CHUNK, SG_WIDTH), (layer, 0, 0)),
        ],
        out_specs=[
            pl.BlockSpec((TOK_BLOCK, CONV_WIDTH), lambda i: (i, 0)),
            pl.BlockSpec((TOK_BLOCK, SG_WIDTH), lambda i: (i, 0)),
        ],
        out_shape=[
            jax.ShapeDtypeStruct((n, CONV_WIDTH), BF),
            jax.ShapeDtypeStruct((n, SG_WIDTH), BF),
        ],
        scratch_shapes=[pltpu.VMEM((SUBLANES, n_seq, seq_len + CONV_SHIFT_ROWS, CONV_WIDTH), F32)],
        compiler_params=_params(1),
        name="mix_in",
    )(x, mod, norm_g, w_in, w_in, conv_w, conv_b, conv_ln_g, conv_ln_b, sg_ln_g, sg_ln_b, sgw_cat, sgb_full)


def _s5_prep_kernel(arow_re_ref, arow_im_ref, dtrow_ref, b4_re_ref, b4_im_ref, c4_re_ref, c4_im_ref,
                    mt_ref, wint_ref, woutt_ref, atc_re_ref, atc_im_ref):
    T, H = S5_T, S5_GROUP
    lanes = 2 * S5_CL
    lane = lax.broadcasted_iota(jnp.int32, (1, S5_NS), 1)
    lane_bwd = lane >= 2 * S5_STATE
    lane_im = (lane // S5_STATE) % 2 == 1
    e_col = lax.broadcasted_iota(jnp.int32, (S5_POW_ROWS, 1), 0).astype(F32)
    zeros = jnp.zeros((H, S5_NS), F32)
    for g in range(S5_PREP_GB):
        a_re, a_im = arow_re_ref[g], arow_im_ref[g]
        dt = jnp.exp(dtrow_ref[g])
        mag = jnp.exp(a_re * dt * e_col)
        ang = a_im * dt * e_col
        pw_re, pw_im = mag * jnp.cos(ang), mag * jnp.sin(ang)
        den = a_re * a_re + a_im * a_im
        nr, ni = pw_re[1:2, :] - 1.0, pw_im[1:2, :]
        q_re = (nr * a_re + ni * a_im) / den
        q_im = (ni * a_re - nr * a_im) / den
        b_re, b_im = b4_re_ref[g], b4_im_ref[g]
        bb_re = q_re * b_re - q_im * b_im
        bb_im = q_re * b_im + q_im * b_re
        c_re, c_im = c4_re_ref[g], c4_im_ref[g]

        def power(e_f, e_b, pw_re=pw_re, pw_im=pw_im):
            return (jnp.where(lane_bwd, pw_re[e_b:e_b + 1, :], pw_re[e_f:e_f + 1, :]),
                    jnp.where(lane_bwd, pw_im[e_b:e_b + 1, :], pw_im[e_f:e_f + 1, :]))

        def times_b(e_f, e_b, bb_re=bb_re, bb_im=bb_im, power=power):
            p_re, p_im = power(e_f, e_b)
            return jnp.where(lane_im, p_re * bb_im + p_im * bb_re, p_re * bb_re - p_im * bb_im)

        def times_c(e_f, e_b, c_re=c_re, c_im=c_im, power=power):
            p_re, p_im = power(e_f, e_b)
            return jnp.where(lane_im, -(c_re * p_im + c_im * p_re), c_re * p_re - c_im * p_im)

        win = jnp.concatenate([times_b(T - 1 - j, j) for j in range(T)], axis=0)
        wint_ref[g] = win.T.astype(BF)
        woutt_ref[g] = jnp.concatenate([times_c(i + 1, T - i) for i in range(T)], axis=0).astype(BF)
        blocks = []
        for b in range(2 * T - 1):
            m = T - 1 - b
            w = times_b(abs(m), abs(m))
            if m > 0:
                w = jnp.where(lane_bwd, 0.0, w)
            elif m < 0:
                w = jnp.where(lane_bwd, w, 0.0)
            blocks.append(w)
        ystack = jnp.concatenate(blocks + [zeros], axis=0)
        c_signed = jnp.where(lane_im, -c_im, c_re)
        krev = lax.dot_general(c_signed, ystack, (((1,), (1,)), ((), ())), preferred_element_type=F32,
                               precision=HIGHEST)
        mt_ref[g] = jnp.concatenate(
            [pltpu.roll(krev, (lanes - H * (T - 1 - i)) % lanes, 1)[:, :S5_CL] for i in range(T)],
            axis=0).astype(BF)
        atc_re_ref[g] = jnp.broadcast_to(pw_re[T:T + 1, :], (S5_NS, S5_NS)).T[:, :LANES]
        atc_im_ref[g] = jnp.broadcast_to(pw_im[T:T + 1, :], (S5_NS, S5_NS)).T[:, :LANES]


def _s5_operators(a_re, a_im, log_dt, b_re, b_im, c_re, c_im):
    G, P, H = S5_GROUPS, S5_STATE, S5_GROUP
    n = DEPTH * G
    rep = lambda a: jnp.broadcast_to(a.transpose(0, 2, 1, 3)[:, :, :, None, :],
                                     (DEPTH, G, 2, 2, P)).reshape(n, 1, S5_NS)
    dt4 = rep(jnp.broadcast_to(log_dt[..., None], (DEPTH, 2, G, P)))
    c4 = lambda c: jnp.broadcast_to(c.transpose(0, 2, 3, 1, 4)[:, :, :, :, None, :],
                                    (DEPTH, G, H, 2, 2, P)).reshape(n, H, S5_NS)
    b4 = lambda b: c4(b.transpose(0, 1, 2, 4, 3))
    blk = lambda r, c: pl.BlockSpec((S5_PREP_GB, r, c), lambda i: (i, 0, 0))
    op = jax.ShapeDtypeStruct((n, S5_CL, S5_CL), BF)
    col = jax.ShapeDtypeStruct((n, S5_NS, LANES), F32)
    mt, wint, woutt, atc_re, atc_im = pl.pallas_call(
        _s5_prep_kernel,
        grid=(n // S5_PREP_GB,),
        in_specs=[blk(1, S5_NS), blk(1, S5_NS), blk(1, S5_NS), blk(H, S5_NS), blk(H, S5_NS), blk(H, S5_NS),
                  blk(H, S5_NS)],
        out_specs=[blk(S5_CL, S5_CL), blk(S5_NS, S5_CL), blk(S5_CL, S5_NS), blk(S5_NS, LANES), blk(S5_NS, LANES)],
        out_shape=[op, op, op, col, col],
        compiler_params=_params(1),
        name="s5_prep",
    )(rep(a_re), rep(a_im), dt4, b4(b_re), b4(b_im), c4(c_re), c4(c_im))
    return dict(mt=mt, wint=wint, woutt=woutt, atc_re=atc_re, atc_im=atc_im)


def _s5_in_kernel(x_ref, mod_ref, g_ref, wt_ref, zt_ref, *, mod_rows):
    g = g_ref[...]
    rows = S5_CB // mod_rows

    def normed(j):
        x = x_ref[:, j, :]
        parts = [_mod_norm(x[m * rows:(m + 1) * rows, :], g, mod_ref[m, 4:5, :], mod_ref[m, 3:4, :])
                 for m in range(mod_rows)]
        return parts[0] if mod_rows == 1 else jnp.concatenate(parts, axis=0)

    for j in range(0, S5_T, 2):
        h = jnp.concatenate([normed(j), normed(j + 1)], axis=0).astype(BF)
        zt = lax.dot_general(wt_ref[...], h, (((1,), (1,)), ((), ())), preferred_element_type=F32)
        zt_ref[j] = zt[:, :S5_CB]
        zt_ref[j + 1] = zt[:, S5_CB:]


def _s5_in(x, mod, norm_g, w_in_t, *, layer, seq_chunks):
    nc = x.shape[0] // S5_T
    xv = x.reshape(nc, S5_T, D_MODEL)
    mod_rows = 1 if mod.shape[0] == 1 else S5_CB // seq_chunks
    in_specs = [
        pl.BlockSpec((S5_CB, S5_T, D_MODEL), lambda i: (i, 0, 0)),
        _mod_spec(mod.shape[0], 1, rows=mod_rows),
        _const_spec((None, 1, D_MODEL), (3 * layer + 1, 0, 0)),
        _const_spec((None, S5_WIDTH, D_MODEL), (layer, 0, 0)),
    ]
    return pl.pallas_call(
        functools.partial(_s5_in_kernel, mod_rows=mod_rows),
        grid=(nc // S5_CB,),
        in_specs=in_specs,
        out_specs=pl.BlockSpec((S5_T, S5_WIDTH, S5_CB), lambda i: (0, 0, i)),
        out_shape=jax.ShapeDtypeStruct((S5_T, S5_WIDTH, nc), F32),
        compiler_params=_params(1),
        name="s5_in",
    )(xv, mod, norm_g, w_in_t)


def _s5_core_kernel(*refs, seq_chunks, has_h0, want_final):
    refs = list(refs)
    zt_ref, mt_ref, wint_ref, woutt_ref, atc_re_ref, atc_im_ref = refs[:6]
    refs = refs[6:]
    h0_ref = refs.pop(0) if has_h0 else None
    yt_ref = refs.pop(0)
    hf_ref = refs.pop(0) if want_final else None
    P, T, K = S5_STATE, S5_T, seq_chunks
    nc = zt_ref.shape[2]
    n_seq = nc // K
    lane = lax.broadcasted_iota(jnp.int32, (1, nc), 1)
    kpos = lane % K
    n_steps = K.bit_length() - 1
    across = lambda c: jnp.concatenate([c] * (nc // LANES), axis=1)
    if has_h0:
        seq = lax.broadcasted_iota(jnp.int32, (n_seq, nc), 0)
        chunk = lax.broadcasted_iota(jnp.int32, (n_seq, nc), 1)
        place = [(chunk == seq * K).astype(F32), (chunk == seq * K + (K - 1)).astype(F32)]
    if want_final:
        chunk = lax.broadcasted_iota(jnp.int32, (nc, LANES), 0)
        seq = lax.broadcasted_iota(jnp.int32, (nc, LANES), 1)
        pick = [(chunk == seq * K + (K - 1)).astype(F32), (chunk == seq * K).astype(F32)]

    for g in range(S5_GB):
        r0 = g * S5_GROUP
        xt = jnp.concatenate([zt_ref[j, r0:r0 + S5_GROUP, :] for j in range(T)], axis=0).astype(BF)
        v = _dot(wint_ref[g], xt)
        states, finals = [], []
        for d in range(2):
            base = 2 * P * d
            cr = atc_re_ref[g, base:base + P, :]
            ci = atc_im_ref[g, base:base + P, :]
            shift = lambda x, s, d=d: pltpu.roll(x, s if d == 0 else nc - s, 1)
            first = kpos == (0 if d == 0 else K - 1)
            vr, vi = v[base:base + P, :], v[base + P:base + 2 * P, :]
            if has_h0:
                h0 = _dot_exact(h0_ref[g, base:base + 2 * P, :], place[d])
                hr, hi = h0[:P, :], h0[P:, :]
            else:
                hr = hi = 0.0
            sr = jnp.where(first, hr, shift(vr, 1))
            si = jnp.where(first, hi, shift(vi, 1))
            if want_final:
                ar, ai = across(cr), across(ci)
            for t in range(n_steps):
                s = 1 << t
                valid = (kpos >= s) if d == 0 else (kpos < K - s)
                rr = jnp.where(valid, shift(sr, s), 0.0)
                ri = jnp.where(valid, shift(si, s), 0.0)
                cr_n, ci_n = across(cr), across(ci)
                sr, si = sr + (cr_n * rr - ci_n * ri), si + (cr_n * ri + ci_n * rr)
                cr, ci = cr * cr - ci * ci, 2.0 * (cr * ci)
            states += [sr, si]
            if want_final:
                finals += [ar * sr - ai * si + vr, ar * si + ai * sr + vi]
        s_in = jnp.concatenate(states, axis=0).astype(BF)
        y = _dot(mt_ref[g], xt) + _dot(woutt_ref[g], s_in)
        for i in range(T):
            yt_ref[i, r0:r0 + S5_GROUP, :] = y[i * S5_GROUP:(i + 1) * S5_GROUP, :]
        if want_final:
            for d in range(2):
                f = jnp.concatenate(finals[2 * d:2 * d + 2], axis=0)
                hf_ref[g, 2 * P * d:2 * P * (d + 1), :] = _dot_exact(f, pick[d])


def _s5_core(zt, ops, *, layer, seq_chunks, h0=None, want_final=False):
    nc = zt.shape[2]
    rows = S5_GB * S5_GROUP
    steps = S5_GROUPS // S5_GB
    gspec = lambda r, c: pl.BlockSpec((S5_GB, r, c), lambda i: (layer * steps + i, 0, 0))
    slab = pl.BlockSpec((S5_T, rows, nc), lambda i: (0, i, 0))
    in_specs = [slab, gspec(S5_CL, S5_CL), gspec(S5_NS, S5_CL), gspec(S5_CL, S5_NS),
                gspec(S5_NS, LANES), gspec(S5_NS, LANES)]
    args = [zt, ops["mt"], ops["wint"], ops["woutt"], ops["atc_re"], ops["atc_im"]]
    if h0 is not None:
        in_specs.append(pl.BlockSpec((S5_GB, S5_NS, h0.shape[2]), lambda i: (i, 0, 0)))
        args.append(h0)
    out_specs = [slab]
    out_shape = [jax.ShapeDtypeStruct(zt.shape, F32)]
    if want_final:
        assert nc // seq_chunks <= LANES
        out_specs.append(pl.BlockSpec((S5_GB, S5_NS, LANES), lambda i: (i, 0, 0)))
        out_shape.append(jax.ShapeDtypeStruct((S5_GROUPS, S5_NS, LANES), F32))
    kern = functools.partial(_s5_core_kernel, seq_chunks=seq_chunks, has_h0=h0 is not None, want_final=want_final)
    out = pl.pallas_call(
        kern,
        grid=(steps,),
        in_specs=in_specs,
        out_specs=out_specs,
        out_shape=out_shape,
        compiler_params=_params(1),
        name="s5_core",
    )(*args)
    return out if want_final else (out[0], None)


def _s5_out_kernel(yt_ref, zt_ref, d_ref, wglut_ref, wa_ref, o_ref):
    d = d_ref[...]
    for j in range(0, S5_T, 2):
        pre = jnp.concatenate([yt_ref[j] + d * zt_ref[j], yt_ref[j + 1] + d * zt_ref[j + 1]], axis=1)
        ya = jax.nn.gelu(pre)
        ya = ya * jax.nn.sigmoid(_dot(wglut_ref[...], ya.astype(BF)))
        pa = lax.dot_general(ya.astype(BF), wa_ref[...], (((0,), (0,)), ((), ())), preferred_element_type=F32)
        o_ref[:, j, :] = pa[:S5_CB, :]
        o_ref[:, j + 1, :] = pa[S5_CB:, :]


def _s5_out(yt, zt, s5_d_col, w_glu_t, w_br_a, *, layer):
    nc = yt.shape[2]
    slab = pl.BlockSpec((S5_T, S5_WIDTH, S5_CB), lambda i: (0, 0, i))
    out = pl.pallas_call(
        _s5_out_kernel,
        grid=(nc // S5_CB,),
        in_specs=[slab, slab,
                  _const_spec((None, S5_WIDTH, 1), (layer, 0, 0)),
                  _const_spec((None, S5_WIDTH, S5_WIDTH), (layer, 0, 0)),
                  _const_spec((None, S5_WIDTH, D_MODEL), (layer, 0, 0))],
        out_specs=pl.BlockSpec((S5_CB, S5_T, D_MODEL), lambda i: (i, 0, 0)),
        out_shape=jax.ShapeDtypeStruct((nc, S5_T, D_MODEL), F32),
        compiler_params=_params(1),
        name="s5_out",
    )(yt, zt, s5_d_col, w_glu_t, w_br_a)
    return out.reshape(nc * S5_T, D_MODEL)


def _mix_out_kernel(x_ref, mod_ref, g_ref, pa_ref, yb_ref, yc_ref, wb_ref, wc_ref, wgate_ref, bgate_ref,
                    wout_ref, o_ref):
    x = x_ref[...]
    h = _mod_norm(x, g_ref[...], mod_ref[0, 4:5, :], mod_ref[0, 3:4, :]).astype(BF)

    def gate(k):
        lo = k * D_MODEL
        return jax.nn.sigmoid(_dot(h, wgate_ref[:, lo:lo + D_MODEL]) + bgate_ref[:, lo:lo + D_MODEL])

    merged = gate(0) * pa_ref[...]
    merged = merged + gate(1) * _dot(yb_ref[...], wb_ref[...])
    merged = merged + gate(2) * _dot(yc_ref[...], wc_ref[...])
    y = _dot(merged.astype(BF), wout_ref[...])
    o_ref[...] = x + mod_ref[0, 5:6, :] * y


def _mix_out(x, mod, norm_g, pa, yb, yc, w_b, w_c, w_gate, b_gate, w_out, *, layer, seq_tokens):
    n = x.shape[0]
    tm = FFN_TM
    tok = lambda c: pl.BlockSpec((tm, c), lambda i: (i, 0))
    return pl.pallas_call(
        _mix_out_kernel,
        grid=(n // tm,),
        in_specs=[
            tok(D_MODEL),
            _mod_spec(mod.shape[0], seq_tokens // tm),
            _const_spec((None, 1, D_MODEL), (3 * layer + 1, 0, 0)),
            tok(D_MODEL), tok(CONV_WIDTH), tok(SG_WIDTH),
            _const_spec((None, CONV_WIDTH, D_MODEL), (layer, 0, 0)),
            _const_spec((None, SG_WIDTH, D_MODEL), (layer, 0, 0)),
            _const_spec((None, D_MODEL, 3 * D_MODEL), (layer, 0, 0)),
            _const_spec((None, 1, 3 * D_MODEL), (layer, 0, 0)),
            _const_spec((None, D_MODEL, D_MODEL), (layer, 0, 0)),
        ],
        out_specs=tok(D_MODEL),
        out_shape=jax.ShapeDtypeStruct((n, D_MODEL), F32),
        compiler_params=_params(1),
        name="mix_out",
    )(x, mod, norm_g, pa, yb, yc, w_b, w_c, w_gate, b_gate, w_out)


def _grid_pos_embed(n_tokens, dim):
    rows = n_tokens // GRID_W
    rr, cc = jnp.meshgrid(jnp.arange(rows, dtype=F32), jnp.arange(GRID_W, dtype=F32), indexing='ij')
    quarter = dim // 4
    omega = 1.0 / (10000.0 ** (jnp.arange(quarter, dtype=F32) / quarter))

    def emb(p):
        ang = p.reshape(-1)[:, None] * omega[None, :]
        return jnp.concatenate([jnp.sin(ang), jnp.cos(ang)], axis=-1)

    return jnp.concatenate([emb(rr), emb(cc)], axis=-1)


def kernel(x_prompt, x_sample, state_ssm, c, c_ctx, w_mod, b_mod, norm_g, ffn_w1, ffn_w2, w_in, w_gate, b_gate,
           s5_a_re, s5_a_im, s5_log_dt, s5_b_re, s5_b_im, s5_c_re, s5_c_im, s5_d, s5_w_glu, w_br_a, conv_w,
           conv_b, conv_ln_g, conv_ln_b, w_br_b, sg_ln_g, sg_ln_b, sg_w, sg_b, w_br_c, w_out, final_g):
    batch, seq, _ = x_prompt.shape
    dec_batch, dec_seq, _ = x_sample.shape
    assert (batch * seq) % TOK_BLOCK == 0 and TOK_BLOCK % seq == 0 and dec_seq % TOK_BLOCK == 0
    assert seq % SG_CHUNK == 0 and seq % S5_T == 0 and dec_seq % S5_T == 0
    ctx_chunks, smp_chunks = seq // S5_T, dec_seq // S5_T
    assert S5_CB % ctx_chunks == 0 and S5_CB % smp_chunks == 0
    assert ctx_chunks & (ctx_chunks - 1) == 0 and smp_chunks & (smp_chunks - 1) == 0
    assert 1 + dec_batch <= MOD_ROWS

    cond = jnp.zeros((MOD_ROWS, D_MODEL), F32).at[0].set(c_ctx).at[1:1 + dec_batch].set(c)
    mod_all = _modulation(cond, w_mod, b_mod).reshape(DEPTH, MOD_ROWS, N_MOD, D_MODEL)
    pos = _grid_pos_embed(dec_seq, D_MODEL)

    w1 = ffn_w1.astype(BF)
    w2 = ffn_w2.astype(BF)
    w_in_b = w_in.astype(BF)
    w_in_t = w_in[:, :, :S5_WIDTH].transpose(0, 2, 1).astype(BF)
    w_glu_t = s5_w_glu.transpose(0, 2, 1).astype(BF)
    w_a, w_b, w_c = w_br_a.astype(BF), w_br_b.astype(BF), w_br_c.astype(BF)
    w_gate_b, w_out_b = w_gate.astype(BF), w_out.astype(BF)
    norm_rows = norm_g.reshape(DEPTH * 3, 1, D_MODEL)
    rows = lambda a: a.reshape(DEPTH, 1, -1)
    sgw_cat = sg_w.transpose(0, 2, 1, 3).reshape(DEPTH, SG_CHUNK, SG_HEADS * SG_CHUNK).astype(BF)
    sgb_full = jnp.repeat(sg_b.transpose(0, 2, 1), SG_HEAD_DIM, axis=2)
    s5_ops = _s5_operators(s5_a_re, s5_a_im, s5_log_dt, s5_b_re, s5_b_im, s5_c_re, s5_c_im)
    s5_d_col = s5_d.reshape(DEPTH, S5_WIDTH, 1)

    groups = {
        "ctx": dict(x=x_prompt.reshape(batch * seq, D_MODEL), n_seq=batch, seq_len=seq, mod_rows=slice(0, 1),
                    mod_tokens=batch * seq, pos=None),
        "smp": dict(x=x_sample.reshape(dec_batch * dec_seq, D_MODEL), n_seq=dec_batch, seq_len=dec_seq,
                    mod_rows=slice(1, 1 + dec_batch), mod_tokens=dec_seq, pos=pos),
    }
    ctx_states = []
    for l in range(DEPTH):
        for name, gr in groups.items():
            mod = mod_all[l, gr["mod_rows"]]
            n_seq, seq_len = gr["n_seq"], gr["seq_len"]
            x = _ffn(gr["x"], mod, norm_rows, w1, w2, layer=l, which=0, seq_tokens=gr["mod_tokens"],
                     pos=gr["pos"] if l == 0 else None)
            yb, yc = _mix_in(x, mod, norm_rows, w_in_b, conv_w, rows(conv_b), rows(conv_ln_g), rows(conv_ln_b),
                             rows(sg_ln_g), rows(sg_ln_b), sgw_cat, sgb_full, layer=l, seq_len=seq_len)
            zt = _s5_in(x, mod, norm_rows, w_in_t, layer=l, seq_chunks=seq_len // S5_T)
            if name == "ctx":
                yt, hf = _s5_core(zt, s5_ops, layer=l, seq_chunks=seq_len // S5_T, want_final=True)
                hf = hf[:, :, :n_seq].reshape(S5_GROUPS, 2, 2, S5_STATE, n_seq)
                ctx_states.append(hf.transpose(4, 1, 0, 3, 2))
            else:
                h0 = state_ssm[:, l].transpose(2, 1, 4, 3, 0).reshape(S5_GROUPS, S5_NS, n_seq)
                yt, _ = _s5_core(zt, s5_ops, layer=l, seq_chunks=seq_len // S5_T, h0=h0)
            pa = _s5_out(yt, zt, s5_d_col, w_glu_t, w_a, layer=l)
            x = _mix_out(x, mod, norm_rows, pa, yb, yc, w_b, w_c, w_gate_b, rows(b_gate), w_out_b,
                         layer=l, seq_tokens=gr["mod_tokens"])
            x = _ffn(x, mod, norm_rows, w1, w2, layer=l, which=1, seq_tokens=gr["mod_tokens"],
                     final_g=final_g if l == DEPTH - 1 else None)
            gr["x"] = x
    y_prompt = groups["ctx"]["x"].reshape(batch, seq, D_MODEL)
    y_sample = groups["smp"]["x"].reshape(dec_batch, dec_seq, D_MODEL)
    new_state_ssm = jnp.stack(ctx_states, axis=1)
    return (y_prompt, y_sample, new_state_ssm)
```

```python
import functools

import jax
import jax.numpy as jnp
from jax import lax
from jax.experimental import pallas as pl
from jax.experimental.pallas import tpu as pltpu

D_MODEL = 1024
DEPTH = 2
GRID_W = 64
D_FF = 2816
S5_WIDTH = 512
S5_GROUP = 16
S5_GROUPS = 32
S5_STATE = 64
CONV_WIDTH = 256
CONV_K = 31
SG_WIDTH = 256
SG_CHUNK = 128
SG_HEADS = 4
SG_HEAD_DIM = SG_WIDTH // SG_HEADS
BRANCH_COLS = 2 * CONV_WIDTH
IN_COLS = S5_WIDTH + 2 * BRANCH_COLS
N_MOD = 9
EPS = 1e-6

LANES = 128
SUBLANES = 8
S5_T = 16
S5_CL = S5_T * S5_GROUP
S5_NS = 4 * S5_STATE
S5_PREP_GB = 4
S5_POW_ROWS = SUBLANES * (S5_T // SUBLANES + 1)
S5_CB = 128
MOD_ROWS = 16
MOD_TN = 1152
TOK_BLOCK = 1024
FFN_TM = 512
CONV_RC = 32
CONV_PAD = 16
CONV_SHIFT_ROWS = SUBLANES * ((CONV_PAD - CONV_K // 2 + CONV_K - 1) // SUBLANES)
VMEM_LIMIT = 56 * 1024 * 1024

BF = jnp.bfloat16
F32 = jnp.float32
HIGHEST = lax.Precision.HIGHEST


def _dot(a, b):
    return jnp.dot(a, b, preferred_element_type=F32)


def _dot_exact(a, b):
    return jnp.dot(a, b, preferred_element_type=F32, precision=HIGHEST)


def _const_spec(block, index):
    return pl.BlockSpec(block, lambda *_: index, pipeline_mode=pl.Buffered(1))


def _params(n_axes=1):
    return pltpu.CompilerParams(dimension_semantics=("arbitrary",) * n_axes,
                                vmem_limit_bytes=VMEM_LIMIT)


def _call(stages, name):
    grid = stages[0]["grid"]
    assert all(s["grid"] == grid for s in stages)
    n_in = [len(s["in_specs"]) for s in stages]
    n_out = [len(s["out_specs"]) for s in stages]
    total_in, total_out = sum(n_in), sum(n_out)

    def body(*refs):
        i = o = 0
        for s, ni, no in zip(stages, n_in, n_out):
            s["kernel"](*refs[i:i + ni], *refs[total_in + o:total_in + o + no])
            i, o = i + ni, o + no

    outs = pl.pallas_call(
        body,
        grid=grid,
        in_specs=[spec for s in stages for spec in s["in_specs"]],
        out_specs=[spec for s in stages for spec in s["out_specs"]],
        out_shape=[shape for s in stages for shape in s["out_shape"]],
        compiler_params=_params(len(grid)),
        name=name,
    )(*[a for s in stages for a in s["args"]])
    split, o = [], 0
    for no in n_out:
        split.append(list(outs[o:o + no]))
        o += no
    return split


def _mod_norm(x, g, sc, sh):
    var = jnp.mean(x * x, axis=-1, keepdims=True)
    return (x * lax.rsqrt(var + EPS) * g) * (1.0 + sc) + sh


def _layernorm(x, g, b):
    mu = jnp.mean(x, axis=-1, keepdims=True)
    xc = x - mu
    var = jnp.mean(xc * xc, axis=-1, keepdims=True)
    return xc * lax.rsqrt(var + EPS) * g + b


def _mod_spec(n_mod, blocks_per_mod, rows=1):
    if n_mod == 1:
        return pl.BlockSpec((1, N_MOD, D_MODEL), lambda i: (0, 0, 0))
    return pl.BlockSpec((rows, N_MOD, D_MODEL), lambda i: (i // blocks_per_mod, 0, 0))


def _mod_kernel(c_ref, w_ref, b_ref, o_ref):
    c = c_ref[...]
    a = (c * jax.nn.sigmoid(c)).astype(BF)
    o_ref[...] = _dot(a, w_ref[...].astype(BF)) + b_ref[...]


def _modulation(cond, w_mod, b_mod):
    n = N_MOD * D_MODEL
    return pl.pallas_call(
        _mod_kernel,
        grid=(DEPTH, n // MOD_TN),
        in_specs=[
            pl.BlockSpec((MOD_ROWS, D_MODEL), lambda l, j: (0, 0)),
            pl.BlockSpec((None, D_MODEL, MOD_TN), lambda l, j: (l, 0, j)),
            pl.BlockSpec((None, 1, MOD_TN), lambda l, j: (l, 0, j)),
        ],
        out_specs=pl.BlockSpec((None, MOD_ROWS, MOD_TN), lambda l, j: (l, 0, j)),
        out_shape=jax.ShapeDtypeStruct((DEPTH, MOD_ROWS, n), F32),
        compiler_params=_params(2),
        name="modulation",
    )(cond, w_mod, b_mod.reshape(DEPTH, 1, n))


def _ffn_kernel(*refs, mod_base, add_pos, final):
    refs = list(refs)
    x_ref = refs.pop(0)
    pos_ref = refs.pop(0) if add_pos else None
    mod_ref, g_ref, w1g_ref, w1u_ref, w2_ref = refs[:5]
    refs = refs[5:]
    fg_ref = refs.pop(0) if final else None
    o_ref = refs.pop(0)

    x = x_ref[...]
    if add_pos:
        x = x + pos_ref[...]
    sh = mod_ref[0, mod_base:mod_base + 1, :]
    sc = mod_ref[0, mod_base + 1:mod_base + 2, :]
    gt = mod_ref[0, mod_base + 2:mod_base + 3, :]
    h = _mod_norm(x, g_ref[...], sc, sh).astype(BF)
    g = _dot(h, w1g_ref[...])
    u = _dot(h, w1u_ref[...])
    a = (g * jax.nn.sigmoid(g) * u).astype(BF)
    y = _dot(a, w2_ref[...])
    xn = x + (0.5 * gt) * y
    if final:
        var = jnp.mean(xn * xn, axis=-1, keepdims=True)
        xn = xn * lax.rsqrt(var + EPS) * fg_ref[...]
    o_ref[...] = xn


def _ffn_stage(x, mod, norm_g, w1, w2, *, layer, which, seq_tokens, pos=None, final_g=None):
    n = x.shape[0]
    tm = FFN_TM
    in_specs = [pl.BlockSpec((tm, D_MODEL), lambda i: (i, 0))]
    args = [x]
    if pos is not None:
        pos_blocks = pos.shape[0] // tm
        in_specs.append(pl.BlockSpec((tm, D_MODEL), lambda i: (i % pos_blocks, 0)))
        args.append(pos)
    in_specs += [
        _mod_spec(mod.shape[0], seq_tokens // tm),
        _const_spec((None, 1, D_MODEL), (3 * layer + 2 * which, 0, 0)),
        _const_spec((None, None, D_MODEL, D_FF), (layer, which, 0, 0)),
        _const_spec((None, None, D_MODEL, D_FF), (layer, which, 0, 1)),
        _const_spec((None, None, D_FF, D_MODEL), (layer, which, 0, 0)),
    ]
    args += [mod, norm_g, w1, w1, w2]
    if final_g is not None:
        in_specs.append(_const_spec((1, D_MODEL), (0, 0)))
        args.append(final_g.reshape(1, D_MODEL))
    kern = functools.partial(_ffn_kernel, mod_base=6 * which, add_pos=pos is not None,
                             final=final_g is not None)
    return dict(kernel=kern, grid=(n // tm,), in_specs=in_specs, args=args,
                out_specs=[pl.BlockSpec((tm, D_MODEL), lambda i: (i, 0))],
                out_shape=[jax.ShapeDtypeStruct((n, D_MODEL), F32)])


def _mix_in_kernel(x_ref, mod_ref, g_ref, winb_ref, winc_ref, cw_ref, cb_ref, clg_ref, clb_ref,
                   sglg_ref, sglb_ref, sgw_ref, sgb_ref, yb_ref, yc_ref, pad_ref, conv_ref, *, seq_len):
    n_seq = TOK_BLOCK // seq_len
    x = x_ref[...]
    h = _mod_norm(x, g_ref[...], mod_ref[0, 4:5, :], mod_ref[0, 3:4, :]).astype(BF)

    zb = _dot(h, winb_ref[...])
    gl = zb[:, :CONV_WIDTH] * jax.nn.sigmoid(zb[:, CONV_WIDTH:])
    zeros = jnp.zeros((CONV_PAD, CONV_WIDTH), F32)
    shifted_rows = seq_len + CONV_SHIFT_ROWS
    padded_rows = seq_len + 2 * CONV_PAD
    for s in range(n_seq):
        padded = jnp.concatenate([zeros, gl[s * seq_len:(s + 1) * seq_len, :], zeros], axis=0)
        pad_ref[0, s] = padded[:shifted_rows, :]
        for r in range(1, SUBLANES):
            pad_ref[r, s] = pltpu.roll(padded, padded_rows - r, 0)[:shifted_rows, :]
    first = CONV_PAD - CONV_K // 2
    tiles = CONV_RC // SUBLANES
    n_q = CONV_SHIFT_ROWS // SUBLANES + 1
    for s in range(n_seq):
        def conv_rows(r, carry, s=s):
            r0 = pl.multiple_of(r * CONV_RC, CONV_RC)
            accs = [None] * n_q
            for shift in range(SUBLANES):
                span = pad_ref[shift, s, pl.ds(r0, CONV_RC + CONV_SHIFT_ROWS), :]
                span = span.reshape(tiles + n_q - 1, SUBLANES, CONV_WIDTH)
                for q in range(n_q):
                    k = SUBLANES * q + shift - first
                    if 0 <= k < CONV_K:
                        term = cw_ref[k] * span[q:q + tiles]
                        accs[q] = term if accs[q] is None else accs[q] + term
            acc = (accs[0] + accs[1]) + (accs[2] + accs[3])
            conv_ref[pl.ds(pl.multiple_of(s * seq_len + r0, CONV_RC), CONV_RC), :] = acc.reshape(CONV_RC, CONV_WIDTH)
            return carry
        lax.fori_loop(0, seq_len // CONV_RC, conv_rows, 0)
    y = _layernorm(conv_ref[...] + cb_ref[...], clg_ref[...], clb_ref[...])
    yb_ref[...] = (y * jax.nn.sigmoid(y)).astype(BF)

    zc = jax.nn.gelu(_dot(h, winc_ref[...]))
    u = zc[:, :SG_WIDTH]
    v = _layernorm(zc[:, SG_WIDTH:], sglg_ref[...], sglb_ref[...]).astype(BF)
    head = lax.broadcasted_iota(jnp.int32, (SG_CHUNK, SG_WIDTH), 1) // SG_HEAD_DIM
    zero = jnp.zeros((SG_CHUNK, SG_WIDTH), BF)
    for n in range(TOK_BLOCK // SG_CHUNK):
        vn = v[n * SG_CHUNK:(n + 1) * SG_CHUNK, :]
        vcat = jnp.concatenate([jnp.where(head == hh, vn, zero) for hh in range(SG_HEADS)], axis=0)
        sgate = _dot(sgw_ref[...], vcat) + sgb_ref[...]
        yc_ref[n * SG_CHUNK:(n + 1) * SG_CHUNK, :] = (u[n * SG_CHUNK:(n + 1) * SG_CHUNK, :] * sgate).astype(BF)


def _mix_in(x, mod, norm_g, w_in, conv_w, conv_b, conv_ln_g, conv_ln_b, sg_ln_g, sg_ln_b, sgw_cat, sgb_full,
            *, layer, seq_len):
    n = x.shape[0]
    n_seq = TOK_BLOCK // seq_len
    vec = lambda width: _const_spec((None, 1, width), (layer, 0, 0))
    kern = functools.partial(_mix_in_kernel, seq_len=seq_len)
    return pl.pallas_call(
        kern,
        grid=(n // TOK_BLOCK,),
        in_specs=[
            pl.BlockSpec((TOK_BLOCK, D_MODEL), lambda i: (i, 0)),
            _mod_spec(mod.shape[0], max(seq_len // TOK_BLOCK, 1)),
            _const_spec((None, 1, D_MODEL), (3 * layer + 1, 0, 0)),
            _const_spec((None, D_MODEL, BRANCH_COLS), (layer, 0, S5_WIDTH // BRANCH_COLS)),
            _const_spec((None, D_MODEL, BRANCH_COLS), (layer, 0, S5_WIDTH // BRANCH_COLS + 1)),
            _const_spec((None, CONV_K, SUBLANES, CONV_WIDTH), (layer, 0, 0, 0)),
            vec(CONV_WIDTH), vec(CONV_WIDTH), vec(CONV_WIDTH), vec(SG_WIDTH), vec(SG_WIDTH),
            _const_spec((None, SG_CHUNK, SG_HEADS * SG_CHUNK), (layer, 0, 0)),
            _const_spec((None, SG_CHUNK, SG_WIDTH), (layer, 0, 0)),
        ],
        out_specs=[
            pl.BlockSpec((TOK_BLOCK, CONV_WIDTH), lambda i: (i, 0)),
            pl.BlockSpec((TOK_BLOCK, SG_WIDTH), lambda i: (i, 0)),
        ],
        out_shape=[
            jax.ShapeDtypeStruct((n, CONV_WIDTH), BF),
            jax.ShapeDtypeStruct((n, SG_WIDTH), BF),
        ],
        scratch_shapes=[pltpu.VMEM((SUBLANES, n_seq, seq_len + CONV_SHIFT_ROWS, CONV_WIDTH), F32),
                        pltpu.VMEM((TOK_BLOCK, CONV_WIDTH), F32)],
        compiler_params=_params(1),
        name="mix_in",
    )(x, mod, norm_g, w_in, w_in, conv_w, conv_b, conv_ln_g, conv_ln_b, sg_ln_g, sg_ln_b, sgw_cat, sgb_full)


def _s5_prep_kernel(arow_re_ref, arow_im_ref, dtrow_ref, b4_re_ref, b4_im_ref, c4_re_ref, c4_im_ref,
                    mt_ref, wint_ref, woutt_ref, atc_re_ref, atc_im_ref):
    T, H = S5_T, S5_GROUP
    lanes = 2 * S5_CL
    lane = lax.broadcasted_iota(jnp.int32, (1, S5_NS), 1)
    lane_bwd = lane >= 2 * S5_STATE
    lane_im = (lane // S5_STATE) % 2 == 1
    e_col = lax.broadcasted_iota(jnp.int32, (S5_POW_ROWS, 1), 0).astype(F32)
    zeros = jnp.zeros((H, S5_NS), F32)
    for g in range(S5_PREP_GB):
        a_re, a_im = arow_re_ref[g], arow_im_ref[g]
        dt = jnp.exp(dtrow_ref[g])
        mag = jnp.exp(a_re * dt * e_col)
        ang = a_im * dt * e_col
        pw_re, pw_im = mag * jnp.cos(ang), mag * jnp.sin(ang)
        den = a_re * a_re + a_im * a_im
        nr, ni = pw_re[1:2, :] - 1.0, pw_im[1:2, :]
        q_re = (nr * a_re + ni * a_im) / den
        q_im = (ni * a_re - nr * a_im) / den
        b_re, b_im = b4_re_ref[g], b4_im_ref[g]
        bb_re = q_re * b_re - q_im * b_im
        bb_im = q_re * b_im + q_im * b_re
        c_re, c_im = c4_re_ref[g], c4_im_ref[g]

        def power(e_f, e_b, pw_re=pw_re, pw_im=pw_im):
            return (jnp.where(lane_bwd, pw_re[e_b:e_b + 1, :], pw_re[e_f:e_f + 1, :]),
                    jnp.where(lane_bwd, pw_im[e_b:e_b + 1, :], pw_im[e_f:e_f + 1, :]))

        def times_b(e_f, e_b, bb_re=bb_re, bb_im=bb_im, power=power):
            p_re, p_im = power(e_f, e_b)
            return jnp.where(lane_im, p_re * bb_im + p_im * bb_re, p_re * bb_re - p_im * bb_im)

        def times_c(e_f, e_b, c_re=c_re, c_im=c_im, power=power):
            p_re, p_im = power(e_f, e_b)
            return jnp.where(lane_im, -(c_re * p_im + c_im * p_re), c_re * p_re - c_im * p_im)

        win = jnp.concatenate([times_b(T - 1 - j, j) for j in range(T)], axis=0)
        wint_ref[g] = win.T.astype(BF)
        woutt_ref[g] = jnp.concatenate([times_c(i + 1, T - i) for i in range(T)], axis=0).astype(BF)
        blocks = []
        for b in range(2 * T - 1):
            m = T - 1 - b
            w = times_b(abs(m), abs(m))
            if m > 0:
                w = jnp.where(lane_bwd, 0.0, w)
            elif m < 0:
                w = jnp.where(lane_bwd, w, 0.0)
            blocks.append(w)
        ystack = jnp.concatenate(blocks + [zeros], axis=0)
        c_signed = jnp.where(lane_im, -c_im, c_re)
        krev = lax.dot_general(c_signed, ystack, (((1,), (1,)), ((), ())), preferred_element_type=F32,
                               precision=HIGHEST)
        mt_ref[g] = jnp.concatenate(
            [pltpu.roll(krev, (lanes - H * (T - 1 - i)) % lanes, 1)[:, :S5_CL] for i in range(T)],
            axis=0).astype(BF)
        atc_re_ref[g] = jnp.broadcast_to(pw_re[T:T + 1, :], (S5_NS, S5_NS)).T[:, :LANES]
        atc_im_ref[g] = jnp.broadcast_to(pw_im[T:T + 1, :], (S5_NS, S5_NS)).T[:, :LANES]


def _s5_operators(a_re, a_im, log_dt, b_re, b_im, c_re, c_im):
    G, P, H = S5_GROUPS, S5_STATE, S5_GROUP
    n = DEPTH * G
    rep = lambda a: jnp.broadcast_to(a.transpose(0, 2, 1, 3)[:, :, :, None, :],
                                     (DEPTH, G, 2, 2, P)).reshape(n, 1, S5_NS)
    dt4 = rep(jnp.broadcast_to(log_dt[..., None], (DEPTH, 2, G, P)))
    c4 = lambda c: jnp.broadcast_to(c.transpose(0, 2, 3, 1, 4)[:, :, :, :, None, :],
                                    (DEPTH, G, H, 2, 2, P)).reshape(n, H, S5_NS)
    b4 = lambda b: c4(b.transpose(0, 1, 2, 4, 3))
    blk = lambda r, c: pl.BlockSpec((S5_PREP_GB, r, c), lambda i: (i, 0, 0))
    op = jax.ShapeDtypeStruct((n, S5_CL, S5_CL), BF)
    col = jax.ShapeDtypeStruct((n, S5_NS, LANES), F32)
    mt, wint, woutt, atc_re, atc_im = pl.pallas_call(
        _s5_prep_kernel,
        grid=(n // S5_PREP_GB,),
        in_specs=[blk(1, S5_NS), blk(1, S5_NS), blk(1, S5_NS), blk(H, S5_NS), blk(H, S5_NS), blk(H, S5_NS),
                  blk(H, S5_NS)],
        out_specs=[blk(S5_CL, S5_CL), blk(S5_NS, S5_CL), blk(S5_CL, S5_NS), blk(S5_NS, LANES), blk(S5_NS, LANES)],
        out_shape=[op, op, op, col, col],
        compiler_params=_params(1),
        name="s5_prep",
    )(rep(a_re), rep(a_im), dt4, b4(b_re), b4(b_im), c4(c_re), c4(c_im))
    return dict(mt=mt, wint=wint, woutt=woutt, atc_re=atc_re, atc_im=atc_im)


def _s5_in_kernel(x_ref, mod_ref, g_ref, wt_ref, zt_ref, *, mod_rows):
    g = g_ref[...]
    rows = S5_CB // mod_rows

    def normed(j):
        x = x_ref[:, j, :]
        parts = [_mod_norm(x[m * rows:(m + 1) * rows, :], g, mod_ref[m, 4:5, :], mod_ref[m, 3:4, :])
                 for m in range(mod_rows)]
        return parts[0] if mod_rows == 1 else jnp.concatenate(parts, axis=0)

    for j in range(0, S5_T, 2):
        h = jnp.concatenate([normed(j), normed(j + 1)], axis=0).astype(BF)
        zt = lax.dot_general(wt_ref[...], h, (((1,), (1,)), ((), ())), preferred_element_type=F32)
        zt_ref[j] = zt[:, :S5_CB]
        zt_ref[j + 1] = zt[:, S5_CB:]


def _s5_in(x, mod, norm_g, w_in_t, *, layer, seq_chunks):
    nc = x.shape[0] // S5_T
    xv = x.reshape(nc, S5_T, D_MODEL)
    mod_rows = 1 if mod.shape[0] == 1 else S5_CB // seq_chunks
    in_specs = [
        pl.BlockSpec((S5_CB, S5_T, D_MODEL), lambda i: (i, 0, 0)),
        _mod_spec(mod.shape[0], 1, rows=mod_rows),
        _const_spec((None, 1, D_MODEL), (3 * layer + 1, 0, 0)),
        _const_spec((None, S5_WIDTH, D_MODEL), (layer, 0, 0)),
    ]
    return pl.pallas_call(
        functools.partial(_s5_in_kernel, mod_rows=mod_rows),
        grid=(nc // S5_CB,),
        in_specs=in_specs,
        out_specs=pl.BlockSpec((S5_T, S5_WIDTH, S5_CB), lambda i: (0, 0, i)),
        out_shape=jax.ShapeDtypeStruct((S5_T, S5_WIDTH, nc), F32),
        compiler_params=_params(1),
        name="s5_in",
    )(xv, mod, norm_g, w_in_t)


def _s5_core_kernel(*refs, groups, seq_chunks, has_h0, want_final):
    refs = list(refs)
    zt_ref, mt_ref, wint_ref, woutt_ref, atc_re_ref, atc_im_ref = refs[:6]
    refs = refs[6:]
    h0_ref = refs.pop(0) if has_h0 else None
    yt_ref = refs.pop(0)
    hf_ref = refs.pop(0) if want_final else None
    P, T, K = S5_STATE, S5_T, seq_chunks
    nc = zt_ref.shape[2]
    n_seq = nc // K
    lane = lax.broadcasted_iota(jnp.int32, (1, nc), 1)
    kpos = lane % K
    n_steps = K.bit_length() - 1
    across = lambda c: jnp.concatenate([c] * (nc // LANES), axis=1)
    if has_h0:
        seq = lax.broadcasted_iota(jnp.int32, (n_seq, nc), 0)
        chunk = lax.broadcasted_iota(jnp.int32, (n_seq, nc), 1)
        place = [(chunk == seq * K).astype(F32), (chunk == seq * K + (K - 1)).astype(F32)]
    if want_final:
        chunk = lax.broadcasted_iota(jnp.int32, (nc, LANES), 0)
        seq = lax.broadcasted_iota(jnp.int32, (nc, LANES), 1)
        pick = [(chunk == seq * K + (K - 1)).astype(F32), (chunk == seq * K).astype(F32)]

    for g in range(groups):
        r0 = g * S5_GROUP
        xt = jnp.concatenate([zt_ref[j, r0:r0 + S5_GROUP, :] for j in range(T)], axis=0).astype(BF)
        v = _dot(wint_ref[g], xt)
        states, finals = [], []
        for d in range(2):
            base = 2 * P * d
            cr = atc_re_ref[g, base:base + P, :]
            ci = atc_im_ref[g, base:base + P, :]
            shift = lambda x, s, d=d: pltpu.roll(x, s if d == 0 else nc - s, 1)
            first = kpos == (0 if d == 0 else K - 1)
            vr, vi = v[base:base + P, :], v[base + P:base + 2 * P, :]
            if has_h0:
                h0 = _dot_exact(h0_ref[g, base:base + 2 * P, :], place[d])
                hr, hi = h0[:P, :], h0[P:, :]
            else:
                hr = hi = 0.0
            sr = jnp.where(first, hr, shift(vr, 1))
            si = jnp.where(first, hi, shift(vi, 1))
            if want_final:
                ar, ai = across(cr), across(ci)
            for t in range(n_steps):
                s = 1 << t
                valid = (kpos >= s) if d == 0 else (kpos < K - s)
                rr = jnp.where(valid, shift(sr, s), 0.0)
                ri = jnp.where(valid, shift(si, s), 0.0)
                cr_n, ci_n = across(cr), across(ci)
                sr, si = sr + (cr_n * rr - ci_n * ri), si + (cr_n * ri + ci_n * rr)
                cr, ci = cr * cr - ci * ci, 2.0 * (cr * ci)
            states += [sr, si]
            if want_final:
                finals += [ar * sr - ai * si + vr, ar * si + ai * sr + vi]
        s_in = jnp.concatenate(states, axis=0).astype(BF)
        y = _dot(mt_ref[g], xt) + _dot(woutt_ref[g], s_in)
        for i in range(T):
            yt_ref[i, r0:r0 + S5_GROUP, :] = y[i * S5_GROUP:(i + 1) * S5_GROUP, :]
        if want_final:
            for d in range(2):
                f = jnp.concatenate(finals[2 * d:2 * d + 2], axis=0)
                hf_ref[g, 2 * P * d:2 * P * (d + 1), :] = _dot_exact(f, pick[d])


def _s5_core_stage(zt, ops, *, layer, seq_chunks, steps, h0=None, want_final=False):
    nc = zt.shape[2]
    gb = S5_GROUPS // steps
    rows = gb * S5_GROUP
    gspec = lambda r, c: pl.BlockSpec((gb, r, c), lambda i: (layer * steps + i, 0, 0))
    slab = pl.BlockSpec((S5_T, rows, nc), lambda i: (0, i, 0))
    in_specs = [slab, gspec(S5_CL, S5_CL), gspec(S5_NS, S5_CL), gspec(S5_CL, S5_NS),
                gspec(S5_NS, LANES), gspec(S5_NS, LANES)]
    args = [zt, ops["mt"], ops["wint"], ops["woutt"], ops["atc_re"], ops["atc_im"]]
    if h0 is not None:
        in_specs.append(pl.BlockSpec((gb, S5_NS, h0.shape[2]), lambda i: (i, 0, 0)))
        args.append(h0)
    out_specs = [slab]
    out_shape = [jax.ShapeDtypeStruct(zt.shape, F32)]
    if want_final:
        assert nc // seq_chunks <= LANES
        out_specs.append(pl.BlockSpec((gb, S5_NS, LANES), lambda i: (i, 0, 0)))
        out_shape.append(jax.ShapeDtypeStruct((S5_GROUPS, S5_NS, LANES), F32))
    kern = functools.partial(_s5_core_kernel, groups=gb, seq_chunks=seq_chunks, has_h0=h0 is not None,
                             want_final=want_final)
    return dict(kernel=kern, grid=(steps,), in_specs=in_specs, args=args, out_specs=out_specs, out_shape=out_shape)


def _s5_out_kernel(yt_ref, zt_ref, d_ref, wglut_ref, wa_ref, o_ref):
    d = d_ref[...]
    for j in range(0, S5_T, 2):
        pre = jnp.concatenate([yt_ref[j] + d * zt_ref[j], yt_ref[j + 1] + d * zt_ref[j + 1]], axis=1)
        ya = jax.nn.gelu(pre)
        ya = ya * jax.nn.sigmoid(_dot(wglut_ref[...], ya.astype(BF)))
        pa = lax.dot_general(ya.astype(BF), wa_ref[...], (((0,), (0,)), ((), ())), preferred_element_type=F32)
        o_ref[:, j, :] = pa[:S5_CB, :]
        o_ref[:, j + 1, :] = pa[S5_CB:, :]


def _s5_out(yt, zt, s5_d_col, w_glu_t, w_br_a, *, layer):
    nc = yt.shape[2]
    slab = pl.BlockSpec((S5_T, S5_WIDTH, S5_CB), lambda i: (0, 0, i))
    out = pl.pallas_call(
        _s5_out_kernel,
        grid=(nc // S5_CB,),
        in_specs=[slab, slab,
                  _const_spec((None, S5_WIDTH, 1), (layer, 0, 0)),
                  _const_spec((None, S5_WIDTH, S5_WIDTH), (layer, 0, 0)),
                  _const_spec((None, S5_WIDTH, D_MODEL), (layer, 0, 0))],
        out_specs=pl.BlockSpec((S5_CB, S5_T, D_MODEL), lambda i: (i, 0, 0)),
        out_shape=jax.ShapeDtypeStruct((nc, S5_T, D_MODEL), F32),
        compiler_params=_params(1),
        name="s5_out",
    )(yt, zt, s5_d_col, w_glu_t, w_br_a)
    return out.reshape(nc * S5_T, D_MODEL)


def _mix_out_kernel(x_ref, mod_ref, g_ref, pa_ref, yb_ref, yc_ref, wb_ref, wc_ref, wgate_ref, bgate_ref,
                    wout_ref, o_ref):
    x = x_ref[...]
    h = _mod_norm(x, g_ref[...], mod_ref[0, 4:5, :], mod_ref[0, 3:4, :]).astype(BF)

    def gate(k):
        lo = k * D_MODEL
        return jax.nn.sigmoid(_dot(h, wgate_ref[:, lo:lo + D_MODEL]) + bgate_ref[:, lo:lo + D_MODEL])

    merged = gate(0) * pa_ref[...]
    merged = merged + gate(1) * _dot(yb_ref[...], wb_ref[...])
    merged = merged + gate(2) * _dot(yc_ref[...], wc_ref[...])
    y = _dot(merged.astype(BF), wout_ref[...])
    o_ref[...] = x + mod_ref[0, 5:6, :] * y


def _mix_out(x, mod, norm_g, pa, yb, yc, w_b, w_c, w_gate, b_gate, w_out, *, layer, seq_tokens):
    n = x.shape[0]
    tm = FFN_TM
    tok = lambda c: pl.BlockSpec((tm, c), lambda i: (i, 0))
    return pl.pallas_call(
        _mix_out_kernel,
        grid=(n // tm,),
        in_specs=[
            tok(D_MODEL),
            _mod_spec(mod.shape[0], seq_tokens // tm),
            _const_spec((None, 1, D_MODEL), (3 * layer + 1, 0, 0)),
            tok(D_MODEL), tok(CONV_WIDTH), tok(SG_WIDTH),
            _const_spec((None, CONV_WIDTH, D_MODEL), (layer, 0, 0)),
            _const_spec((None, SG_WIDTH, D_MODEL), (layer, 0, 0)),
            _const_spec((None, D_MODEL, 3 * D_MODEL), (layer, 0, 0)),
            _const_spec((None, 1, 3 * D_MODEL), (layer, 0, 0)),
            _const_spec((None, D_MODEL, D_MODEL), (layer, 0, 0)),
        ],
        out_specs=tok(D_MODEL),
        out_shape=jax.ShapeDtypeStruct((n, D_MODEL), F32),
        compiler_params=_params(1),
        name="mix_out",
    )(x, mod, norm_g, pa, yb, yc, w_b, w_c, w_gate, b_gate, w_out)


def _grid_pos_embed(n_tokens, dim):
    rows = n_tokens // GRID_W
    rr, cc = jnp.meshgrid(jnp.arange(rows, dtype=F32), jnp.arange(GRID_W, dtype=F32), indexing='ij')
    quarter = dim // 4
    omega = 1.0 / (10000.0 ** (jnp.arange(quarter, dtype=F32) / quarter))

    def emb(p):
        ang = p.reshape(-1)[:, None] * omega[None, :]
        return jnp.concatenate([jnp.sin(ang), jnp.cos(ang)], axis=-1)

    return jnp.concatenate([emb(rr), emb(cc)], axis=-1)


def kernel(x_prompt, x_sample, state_ssm, c, c_ctx, w_mod, b_mod, norm_g, ffn_w1, ffn_w2, w_in, w_gate, b_gate,
           s5_a_re, s5_a_im, s5_log_dt, s5_b_re, s5_b_im, s5_c_re, s5_c_im, s5_d, s5_w_glu, w_br_a, conv_w,
           conv_b, conv_ln_g, conv_ln_b, w_br_b, sg_ln_g, sg_ln_b, sg_w, sg_b, w_br_c, w_out, final_g):
    batch, seq, _ = x_prompt.shape
    dec_batch, dec_seq, _ = x_sample.shape
    assert (batch * seq) % TOK_BLOCK == 0 and TOK_BLOCK % seq == 0 and dec_seq % TOK_BLOCK == 0
    assert seq % SG_CHUNK == 0 and seq % S5_T == 0 and dec_seq % S5_T == 0
    ctx_chunks, smp_chunks = seq // S5_T, dec_seq // S5_T
    assert S5_CB % ctx_chunks == 0 and S5_CB % smp_chunks == 0
    assert ctx_chunks & (ctx_chunks - 1) == 0 and smp_chunks & (smp_chunks - 1) == 0
    assert 1 + dec_batch <= MOD_ROWS

    cond = jnp.zeros((MOD_ROWS, D_MODEL), F32).at[0].set(c_ctx).at[1:1 + dec_batch].set(c)
    mod_all = _modulation(cond, w_mod, b_mod).reshape(DEPTH, MOD_ROWS, N_MOD, D_MODEL)
    pos = _grid_pos_embed(dec_seq, D_MODEL)

    w1 = ffn_w1.astype(BF)
    w2 = ffn_w2.astype(BF)
    w_in_b = w_in.astype(BF)
    w_in_t = w_in[:, :, :S5_WIDTH].transpose(0, 2, 1).astype(BF)
    w_glu_t = s5_w_glu.transpose(0, 2, 1).astype(BF)
    w_a, w_b, w_c = w_br_a.astype(BF), w_br_b.astype(BF), w_br_c.astype(BF)
    w_gate_b, w_out_b = w_gate.astype(BF), w_out.astype(BF)
    norm_rows = norm_g.reshape(DEPTH * 3, 1, D_MODEL)
    rows = lambda a: a.reshape(DEPTH, 1, -1)
    sgw_cat = sg_w.transpose(0, 2, 1, 3).reshape(DEPTH, SG_CHUNK, SG_HEADS * SG_CHUNK).astype(BF)
    sgb_full = jnp.repeat(sg_b.transpose(0, 2, 1), SG_HEAD_DIM, axis=2)
    s5_ops = _s5_operators(s5_a_re, s5_a_im, s5_log_dt, s5_b_re, s5_b_im, s5_c_re, s5_c_im)
    s5_d_col = s5_d.reshape(DEPTH, S5_WIDTH, 1)

    conv_w8 = jnp.broadcast_to(conv_w[:, :, None, :], (DEPTH, CONV_K, SUBLANES, CONV_WIDTH))

    ctx = dict(n_seq=batch, seq_len=seq, mod_rows=slice(0, 1), mod_tokens=batch * seq)
    smp = dict(n_seq=dec_batch, seq_len=dec_seq, mod_rows=slice(1, 1 + dec_batch), mod_tokens=dec_seq)
    x_ctx = x_prompt.reshape(batch * seq, D_MODEL)
    x_smp = x_sample.reshape(dec_batch * dec_seq, D_MODEL)
    ffn_steps = x_ctx.shape[0] // FFN_TM
    assert x_smp.shape[0] // FFN_TM == ffn_steps and S5_GROUPS % ffn_steps == 0

    def ffn_stage(gr, x, l, which, **kw):
        return _ffn_stage(x, mod_all[l, gr["mod_rows"]], norm_rows, w1, w2, layer=l, which=which,
                          seq_tokens=gr["mod_tokens"], **kw)

    def branches_in(gr, x, l):
        mod = mod_all[l, gr["mod_rows"]]
        yb, yc = _mix_in(x, mod, norm_rows, w_in_b, conv_w8, rows(conv_b), rows(conv_ln_g), rows(conv_ln_b),
                         rows(sg_ln_g), rows(sg_ln_b), sgw_cat, sgb_full, layer=l, seq_len=gr["seq_len"])
        zt = _s5_in(x, mod, norm_rows, w_in_t, layer=l, seq_chunks=gr["seq_len"] // S5_T)
        return yb, yc, zt

    def s5_stage(gr, zt, l, **kw):
        return _s5_core_stage(zt, s5_ops, layer=l, seq_chunks=gr["seq_len"] // S5_T, steps=ffn_steps, **kw)

    def branches_out(gr, x, l, yb, yc, zt, yt):
        pa = _s5_out(yt, zt, s5_d_col, w_glu_t, w_a, layer=l)
        return _mix_out(x, mod_all[l, gr["mod_rows"]], norm_rows, pa, yb, yc, w_b, w_c, w_gate_b, rows(b_gate),
                        w_out_b, layer=l, seq_tokens=gr["mod_tokens"])

    ctx_states = []
    for l in range(DEPTH):
        [x_ctx], = _call([ffn_stage(ctx, x_ctx, l, 0)], "ffn")
        yb_c, yc_c, zt_c = branches_in(ctx, x_ctx, l)
        [x_smp], [yt_c, hf] = _call([ffn_stage(smp, x_smp, l, 0, pos=pos if l == 0 else None),
                                      s5_stage(ctx, zt_c, l, want_final=True)], "ffn_s5")
        hf = hf[:, :, :batch].reshape(S5_GROUPS, 2, 2, S5_STATE, batch)
        ctx_states.append(hf.transpose(4, 1, 0, 3, 2))
        x_ctx = branches_out(ctx, x_ctx, l, yb_c, yc_c, zt_c, yt_c)
        yb_s, yc_s, zt_s = branches_in(smp, x_smp, l)
        h0 = state_ssm[:, l].transpose(2, 1, 4, 3, 0).reshape(S5_GROUPS, S5_NS, dec_batch)
        final = dict(final_g=final_g) if l == DEPTH - 1 else {}
        [x_ctx], [yt_s] = _call([ffn_stage(ctx, x_ctx, l, 1, **final), s5_stage(smp, zt_s, l, h0=h0)], "ffn_s5")
        x_smp = branches_out(smp, x_smp, l, yb_s, yc_s, zt_s, yt_s)
        [x_smp], = _call([ffn_stage(smp, x_smp, l, 1, **final)], "ffn")
    y_prompt = x_ctx.reshape(batch, seq, D_MODEL)
    y_sample = x_smp.reshape(dec_batch, dec_seq, D_MODEL)
    new_state_ssm = jnp.stack(ctx_states, axis=1)
    return (y_prompt, y_sample, new_state_ssm)
```

```python
import functools

import jax
import jax.numpy as jnp
from jax import lax
from jax.experimental import pallas as pl
from jax.experimental.pallas import tpu as pltpu

D_MODEL = 1024
DEPTH = 2
GRID_W = 64
D_FF = 2816
S5_WIDTH = 512
S5_GROUP = 16
S5_GROUPS = 32
S5_STATE = 64
CONV_WIDTH = 256
CONV_K = 31
SG_WIDTH = 256
SG_CHUNK = 128
SG_HEADS = 4
SG_HEAD_DIM = SG_WIDTH // SG_HEADS
BRANCH_COLS = 2 * CONV_WIDTH
IN_COLS = S5_WIDTH + 2 * BRANCH_COLS
N_MOD = 9
EPS = 1e-6

LANES = 128
SUBLANES = 8
S5_T = 16
S5_CL = S5_T * S5_GROUP
S5_NS = 4 * S5_STATE
S5_PREP_GB = 4
S5_POW_ROWS = SUBLANES * (S5_T // SUBLANES + 1)
S5_CB = 128
MOD_ROWS = 16
MOD_TN = 1152
TOK_BLOCK = 1024
FFN_TM = 512
CONV_RC = 32
CONV_PAD = 16
CONV_SHIFT_ROWS = SUBLANES * ((CONV_PAD - CONV_K // 2 + CONV_K - 1) // SUBLANES)
VMEM_LIMIT = 56 * 1024 * 1024

BF = jnp.bfloat16
F32 = jnp.float32
HIGHEST = lax.Precision.HIGHEST


def _dot(a, b):
    return jnp.dot(a, b, preferred_element_type=F32)


def _dot_exact(a, b):
    return jnp.dot(a, b, preferred_element_type=F32, precision=HIGHEST)


def _const_spec(block, index):
    return pl.BlockSpec(block, lambda *_: index, pipeline_mode=pl.Buffered(1))


def _params(n_axes=1):
    return pltpu.CompilerParams(dimension_semantics=("arbitrary",) * n_axes,
                                vmem_limit_bytes=VMEM_LIMIT)


def _call(stages, name):
    grid = stages[0]["grid"]
    assert all(s["grid"] == grid for s in stages)
    n_in = [len(s["in_specs"]) for s in stages]
    n_out = [len(s["out_specs"]) for s in stages]
    total_in, total_out = sum(n_in), sum(n_out)

    def body(*refs):
        i = o = 0
        for s, ni, no in zip(stages, n_in, n_out):
            s["kernel"](*refs[i:i + ni], *refs[total_in + o:total_in + o + no])
            i, o = i + ni, o + no

    outs = pl.pallas_call(
        body,
        grid=grid,
        in_specs=[spec for s in stages for spec in s["in_specs"]],
        out_specs=[spec for s in stages for spec in s["out_specs"]],
        out_shape=[shape for s in stages for shape in s["out_shape"]],
        compiler_params=_params(len(grid)),
        name=name,
    )(*[a for s in stages for a in s["args"]])
    split, o = [], 0
    for no in n_out:
        split.append(list(outs[o:o + no]))
        o += no
    return split


def _mod_norm(x, g, sc, sh):
    var = jnp.mean(x * x, axis=-1, keepdims=True)
    return (x * lax.rsqrt(var + EPS) * g) * (1.0 + sc) + sh


def _layernorm(x, g, b):
    mu = jnp.mean(x, axis=-1, keepdims=True)
    xc = x - mu
    var = jnp.mean(xc * xc, axis=-1, keepdims=True)
    return xc * lax.rsqrt(var + EPS) * g + b


def _mod_spec(n_mod, blocks_per_mod, rows=1):
    if n_mod == 1:
        return pl.BlockSpec((1, N_MOD, D_MODEL), lambda i: (0, 0, 0))
    return pl.BlockSpec((rows, N_MOD, D_MODEL), lambda i: (i // blocks_per_mod, 0, 0))


def _mod_kernel(c_ref, w_ref, b_ref, o_ref):
    c = c_ref[...]
    a = (c * jax.nn.sigmoid(c)).astype(BF)
    o_ref[...] = _dot(a, w_ref[...].astype(BF)) + b_ref[...]


def _modulation(cond, w_mod, b_mod):
    n = N_MOD * D_MODEL
    return pl.pallas_call(
        _mod_kernel,
        grid=(DEPTH, n // MOD_TN),
        in_specs=[
            pl.BlockSpec((MOD_ROWS, D_MODEL), lambda l, j: (0, 0)),
            pl.BlockSpec((None, D_MODEL, MOD_TN), lambda l, j: (l, 0, j)),
            pl.BlockSpec((None, 1, MOD_TN), lambda l, j: (l, 0, j)),
        ],
        out_specs=pl.BlockSpec((None, MOD_ROWS, MOD_TN), lambda l, j: (l, 0, j)),
        out_shape=jax.ShapeDtypeStruct((DEPTH, MOD_ROWS, n), F32),
        compiler_params=_params(2),
        name="modulation",
    )(cond, w_mod, b_mod.reshape(DEPTH, 1, n))


def _ffn_kernel(*refs, mod_base, add_pos, final):
    refs = list(refs)
    x_ref = refs.pop(0)
    pos_ref = refs.pop(0) if add_pos else None
    mod_ref, g_ref, w1g_ref, w1u_ref, w2_ref = refs[:5]
    refs = refs[5:]
    fg_ref = refs.pop(0) if final else None
    o_ref = refs.pop(0)

    x = x_ref[...]
    if add_pos:
        x = x + pos_ref[...]
    sh = mod_ref[0, mod_base:mod_base + 1, :]
    sc = mod_ref[0, mod_base + 1:mod_base + 2, :]
    gt = mod_ref[0, mod_base + 2:mod_base + 3, :]
    h = _mod_norm(x, g_ref[...], sc, sh).astype(BF)
    g = _dot(h, w1g_ref[...])
    u = _dot(h, w1u_ref[...])
    a = (g * jax.nn.sigmoid(g) * u).astype(BF)
    y = _dot(a, w2_ref[...])
    xn = x + (0.5 * gt) * y
    if final:
        var = jnp.mean(xn * xn, axis=-1, keepdims=True)
        xn = xn * lax.rsqrt(var + EPS) * fg_ref[...]
    o_ref[...] = xn


def _ffn_stage(x, mod, norm_g, w1, w2, *, layer, which, seq_tokens, pos=None, final_g=None):
    n = x.shape[0]
    tm = FFN_TM
    in_specs = [pl.BlockSpec((tm, D_MODEL), lambda i: (i, 0))]
    args = [x]
    if pos is not None:
        pos_blocks = pos.shape[0] // tm
        in_specs.append(pl.BlockSpec((tm, D_MODEL), lambda i: (i % pos_blocks, 0)))
        args.append(pos)
    in_specs += [
        _mod_spec(mod.shape[0], seq_tokens // tm),
        _const_spec((None, 1, D_MODEL), (3 * layer + 2 * which, 0, 0)),
        _const_spec((None, None, D_MODEL, D_FF), (layer, which, 0, 0)),
        _const_spec((None, None, D_MODEL, D_FF), (layer, which, 0, 1)),
        _const_spec((None, None, D_FF, D_MODEL), (layer, which, 0, 0)),
    ]
    args += [mod, norm_g, w1, w1, w2]
    if final_g is not None:
        in_specs.append(_const_spec((1, D_MODEL), (0, 0)))
        args.append(final_g.reshape(1, D_MODEL))
    kern = functools.partial(_ffn_kernel, mod_base=6 * which, add_pos=pos is not None,
                             final=final_g is not None)
    return dict(kernel=kern, grid=(n // tm,), in_specs=in_specs, args=args,
                out_specs=[pl.BlockSpec((tm, D_MODEL), lambda i: (i, 0))],
                out_shape=[jax.ShapeDtypeStruct((n, D_MODEL), F32)])


def _mix_in_kernel(x_ref, mod_ref, g_ref, winb_ref, winc_ref, cw_ref, cb_ref, clg_ref, clb_ref,
                   sglg_ref, sglb_ref, sgw_ref, sgb_ref, yb_ref, yc_ref, pad_ref, conv_ref, *, seq_len):
    n_seq = TOK_BLOCK // seq_len
    x = x_ref[...]
    h = _mod_norm(x, g_ref[...], mod_ref[0, 4:5, :], mod_ref[0, 3:4, :]).astype(BF)

    zb = _dot(h, winb_ref[...])
    gl = zb[:, :CONV_WIDTH] * jax.nn.sigmoid(zb[:, CONV_WIDTH:])
    zeros = jnp.zeros((CONV_PAD, CONV_WIDTH), F32)
    shifted_rows = seq_len + CONV_SHIFT_ROWS
    padded_rows = seq_len + 2 * CONV_PAD
    for s in range(n_seq):
        padded = jnp.concatenate([zeros, gl[s * seq_len:(s + 1) * seq_len, :], zeros], axis=0)
        pad_ref[0, s] = padded[:shifted_rows, :]
        for r in range(1, SUBLANES):
            pad_ref[r, s] = pltpu.roll(padded, padded_rows - r, 0)[:shifted_rows, :]
    first = CONV_PAD - CONV_K // 2
    tiles = CONV_RC // SUBLANES
    n_q = CONV_SHIFT_ROWS // SUBLANES + 1
    for s in range(n_seq):
        def conv_rows(r, carry, s=s):
            r0 = pl.multiple_of(r * CONV_RC, CONV_RC)
            accs = [None] * n_q
            for shift in range(SUBLANES):
                span = pad_ref[shift, s, pl.ds(r0, CONV_RC + CONV_SHIFT_ROWS), :]
                span = span.reshape(tiles + n_q - 1, SUBLANES, CONV_WIDTH)
                for q in range(n_q):
                    k = SUBLANES * q + shift - first
                    if 0 <= k < CONV_K:
                        term = cw_ref[k] * span[q:q + tiles]
                        accs[q] = term if accs[q] is None else accs[q] + term
            acc = (accs[0] + accs[1]) + (accs[2] + accs[3])
            conv_ref[pl.ds(pl.multiple_of(s * seq_len + r0, CONV_RC), CONV_RC), :] = acc.reshape(CONV_RC, CONV_WIDTH)
            return carry
        lax.fori_loop(0, seq_len // CONV_RC, conv_rows, 0)
    y = _layernorm(conv_ref[...] + cb_ref[...], clg_ref[...], clb_ref[...])
    yb_ref[...] = (y * jax.nn.sigmoid(y)).astype(BF)

    zc = jax.nn.gelu(_dot(h, winc_ref[...]))
    u = zc[:, :SG_WIDTH]
    v = _layernorm(zc[:, SG_WIDTH:], sglg_ref[...], sglb_ref[...]).astype(BF)
    head = lax.broadcasted_iota(jnp.int32, (SG_CHUNK, SG_WIDTH), 1) // SG_HEAD_DIM
    zero = jnp.zeros((SG_CHUNK, SG_WIDTH), BF)
    for n in range(TOK_BLOCK // SG_CHUNK):
        vn = v[n * SG_CHUNK:(n + 1) * SG_CHUNK, :]
        vcat = jnp.concatenate([jnp.where(head == hh, vn, zero) for hh in range(SG_HEADS)], axis=0)
        sgate = _dot(sgw_ref[...], vcat) + sgb_ref[...]
        yc_ref[n * SG_CHUNK:(n + 1) * SG_CHUNK, :] = (u[n * SG_CHUNK:(n + 1) * SG_CHUNK, :] * sgate).astype(BF)


def _mix_in(x, mod, norm_g, w_in, conv_w, conv_b, conv_ln_g, conv_ln_b, sg_ln_g, sg_ln_b, sgw_cat, sgb_full,
            *, layer, seq_len):
    n = x.shape[0]
    n_seq = TOK_BLOCK // seq_len
    vec = lambda width: _const_spec((None, 1, width), (layer, 0, 0))
    kern = functools.partial(_mix_in_kernel, seq_len=seq_len)
    return pl.pallas_call(
        kern,
        grid=(n // TOK_BLOCK,),
        in_specs=[
            pl.BlockSpec((TOK_BLOCK, D_MODEL), lambda i: (i, 0)),
            _mod_spec(mod.shape[0], max(seq_len // TOK_BLOCK, 1)),
            _const_spec((None, 1, D_MODEL), (3 * layer + 1, 0, 0)),
            _const_spec((None, D_MODEL, BRANCH_COLS), (layer, 0, S5_WIDTH // BRANCH_COLS)),
            _const_spec((None, D_MODEL, BRANCH_COLS), (layer, 0, S5_WIDTH // BRANCH_COLS + 1)),
            _const_spec((None, CONV_K, SUBLANES, CONV_WIDTH), (layer, 0, 0, 0)),
            vec(CONV_WIDTH), vec(CONV_WIDTH), vec(CONV_WIDTH), vec(SG_WIDTH), vec(SG_WIDTH),
            _const_spec((None, SG_CHUNK, SG_HEADS * SG_CHUNK), (layer, 0, 0)),
            _const_spec((None, SG_CHUNK, SG_WIDTH), (layer, 0, 0)),
        ],
        out_specs=[
            pl.BlockSpec((TOK_BLOCK, CONV_WIDTH), lambda i: (i, 0)),
            pl.BlockSpec((TOK_BLOCK, SG_WIDTH), lambda i: (i, 0)),
        ],
        out_shape=[
            jax.ShapeDtypeStruct((n, CONV_WIDTH), BF),
            jax.ShapeDtypeStruct((n, SG_WIDTH), BF),
        ],
        scratch_shapes=[pltpu.VMEM((SUBLANES, n_seq, seq_len + CONV_SHIFT_ROWS, CONV_WIDTH), F32),
                        pltpu.VMEM((TOK_BLOCK, CONV_WIDTH), F32)],
        compiler_params=_params(1),
        name="mix_in",
    )(x, mod, norm_g, w_in, w_in, conv_w, conv_b, conv_ln_g, conv_ln_b, sg_ln_g, sg_ln_b, sgw_cat, sgb_full)


def _s5_prep_kernel(arow_re_ref, arow_im_ref, dtrow_ref, b4_re_ref, b4_im_ref, c4_re_ref, c4_im_ref,
                    mt_ref, wint_ref, woutt_ref, atc_re_ref, atc_im_ref):
    T, H = S5_T, S5_GROUP
    lanes = 2 * S5_CL
    lane = lax.broadcasted_iota(jnp.int32, (1, S5_NS), 1)
    lane_bwd = lane >= 2 * S5_STATE
    lane_im = (lane // S5_STATE) % 2 == 1
    e_col = lax.broadcasted_iota(jnp.int32, (S5_POW_ROWS, 1), 0).astype(F32)
    zeros = jnp.zeros((H, S5_NS), F32)
    for g in range(S5_PREP_GB):
        a_re, a_im = arow_re_ref[g], arow_im_ref[g]
        dt = jnp.exp(dtrow_ref[g])
        mag = jnp.exp(a_re * dt * e_col)
        ang = a_im * dt * e_col
        pw_re, pw_im = mag * jnp.cos(ang), mag * jnp.sin(ang)
        den = a_re * a_re + a_im * a_im
        nr, ni = pw_re[1:2, :] - 1.0, pw_im[1:2, :]
        q_re = (nr * a_re + ni * a_im) / den
        q_im = (ni * a_re - nr * a_im) / den
        b_re, b_im = b4_re_ref[g], b4_im_ref[g]
        bb_re = q_re * b_re - q_im * b_im
        bb_im = q_re * b_im + q_im * b_re
        c_re, c_im = c4_re_ref[g], c4_im_ref[g]

        def power(e_f, e_b, pw_re=pw_re, pw_im=pw_im):
            return (jnp.where(lane_bwd, pw_re[e_b:e_b + 1, :], pw_re[e_f:e_f + 1, :]),
                    jnp.where(lane_bwd, pw_im[e_b:e_b + 1, :], pw_im[e_f:e_f + 1, :]))

        def times_b(e_f, e_b, bb_re=bb_re, bb_im=bb_im, power=power):
            p_re, p_im = power(e_f, e_b)
            return jnp.where(lane_im, p_re * bb_im + p_im * bb_re, p_re * bb_re - p_im * bb_im)

        def times_c(e_f, e_b, c_re=c_re, c_im=c_im, power=power):
            p_re, p_im = power(e_f, e_b)
            return jnp.where(lane_im, -(c_re * p_im + c_im * p_re), c_re * p_re - c_im * p_im)

        win = jnp.concatenate([times_b(T - 1 - j, j) for j in range(T)], axis=0)
        wint_ref[g] = win.T.astype(BF)
        woutt_ref[g] = jnp.concatenate([times_c(i + 1, T - i) for i in range(T)], axis=0).astype(BF)
        blocks = []
        for b in range(2 * T - 1):
            m = T - 1 - b
            w = times_b(abs(m), abs(m))
            if m > 0:
                w = jnp.where(lane_bwd, 0.0, w)
            elif m < 0:
                w = jnp.where(lane_bwd, w, 0.0)
            blocks.append(w)
        ystack = jnp.concatenate(blocks + [zeros], axis=0)
        c_signed = jnp.where(lane_im, -c_im, c_re)
        krev = lax.dot_general(c_signed, ystack, (((1,), (1,)), ((), ())), preferred_element_type=F32,
                               precision=HIGHEST)
        mt_ref[g] = jnp.concatenate(
            [pltpu.roll(krev, (lanes - H * (T - 1 - i)) % lanes, 1)[:, :S5_CL] for i in range(T)],
            axis=0).astype(BF)
        atc_re_ref[g] = jnp.broadcast_to(pw_re[T:T + 1, :], (S5_NS, S5_NS)).T[:, :LANES]
        atc_im_ref[g] = jnp.broadcast_to(pw_im[T:T + 1, :], (S5_NS, S5_NS)).T[:, :LANES]


def _s5_operators(a_re, a_im, log_dt, b_re, b_im, c_re, c_im):
    G, P, H = S5_GROUPS, S5_STATE, S5_GROUP
    n = DEPTH * G
    rep = lambda a: jnp.broadcast_to(a.transpose(0, 2, 1, 3)[:, :, :, None, :],
                                     (DEPTH, G, 2, 2, P)).reshape(n, 1, S5_NS)
    dt4 = rep(jnp.broadcast_to(log_dt[..., None], (DEPTH, 2, G, P)))
    c4 = lambda c: jnp.broadcast_to(c.transpose(0, 2, 3, 1, 4)[:, :, :, :, None, :],
                                    (DEPTH, G, H, 2, 2, P)).reshape(n, H, S5_NS)
    b4 = lambda b: c4(b.transpose(0, 1, 2, 4, 3))
    blk = lambda r, c: pl.BlockSpec((S5_PREP_GB, r, c), lambda i: (i, 0, 0))
    op = jax.ShapeDtypeStruct((n, S5_CL, S5_CL), BF)
    col = jax.ShapeDtypeStruct((n, S5_NS, LANES), F32)
    mt, wint, woutt, atc_re, atc_im = pl.pallas_call(
        _s5_prep_kernel,
        grid=(n // S5_PREP_GB,),
        in_specs=[blk(1, S5_NS), blk(1, S5_NS), blk(1, S5_NS), blk(H, S5_NS), blk(H, S5_NS), blk(H, S5_NS),
                  blk(H, S5_NS)],
        out_specs=[blk(S5_CL, S5_CL), blk(S5_NS, S5_CL), blk(S5_CL, S5_NS), blk(S5_NS, LANES), blk(S5_NS, LANES)],
        out_shape=[op, op, op, col, col],
        compiler_params=_params(1),
        name="s5_prep",
    )(rep(a_re), rep(a_im), dt4, b4(b_re), b4(b_im), c4(c_re), c4(c_im))
    return dict(mt=mt, wint=wint, woutt=woutt, atc_re=atc_re, atc_im=atc_im)


def _token_slab_copies(hbm_ref, buf_ref, sem_ref, block, slot, to_hbm):
    copies = []
    for j in range(S5_T):
        hbm = hbm_ref.at[pl.ds(block * S5_CB, S5_CB), j, :]
        vmem = buf_ref.at[slot, j]
        src, dst = (vmem, hbm) if to_hbm else (hbm, vmem)
        copies.append(pltpu.make_async_copy(src, dst, sem_ref.at[slot, j]))
    return copies


def _s5_in_kernel(x_hbm, mod_ref, g_ref, wt_ref, zt_ref, xbuf, sem, *, mod_rows):
    g = g_ref[...]
    rows = S5_CB // mod_rows
    i = pl.program_id(0)
    slot = i % 2
    fetch = functools.partial(_token_slab_copies, x_hbm, xbuf, sem, to_hbm=False)

    @pl.when(i == 0)
    def _():
        for c in fetch(0, 0):
            c.start()

    @pl.when(i + 1 < pl.num_programs(0))
    def _():
        for c in fetch(i + 1, 1 - slot):
            c.start()

    for c in fetch(i, slot):
        c.wait()

    def normed(j):
        x = xbuf[slot, j]
        parts = [_mod_norm(x[m * rows:(m + 1) * rows, :], g, mod_ref[m, 4:5, :], mod_ref[m, 3:4, :])
                 for m in range(mod_rows)]
        return parts[0] if mod_rows == 1 else jnp.concatenate(parts, axis=0)

    for j in range(0, S5_T, 2):
        h = jnp.concatenate([normed(j), normed(j + 1)], axis=0).astype(BF)
        zt = lax.dot_general(wt_ref[...], h, (((1,), (1,)), ((), ())), preferred_element_type=F32)
        zt_ref[j] = zt[:, :S5_CB]
        zt_ref[j + 1] = zt[:, S5_CB:]


def _s5_in(x, mod, norm_g, w_in_t, *, layer, seq_chunks):
    nc = x.shape[0] // S5_T
    xv = x.reshape(nc, S5_T, D_MODEL)
    mod_rows = 1 if mod.shape[0] == 1 else S5_CB // seq_chunks
    in_specs = [
        pl.BlockSpec(memory_space=pl.ANY),
        _mod_spec(mod.shape[0], 1, rows=mod_rows),
        _const_spec((None, 1, D_MODEL), (3 * layer + 1, 0, 0)),
        _const_spec((None, S5_WIDTH, D_MODEL), (layer, 0, 0)),
    ]
    return pl.pallas_call(
        functools.partial(_s5_in_kernel, mod_rows=mod_rows),
        grid=(nc // S5_CB,),
        in_specs=in_specs,
        out_specs=pl.BlockSpec((S5_T, S5_WIDTH, S5_CB), lambda i: (0, 0, i)),
        out_shape=jax.ShapeDtypeStruct((S5_T, S5_WIDTH, nc), F32),
        scratch_shapes=[pltpu.VMEM((2, S5_T, S5_CB, D_MODEL), F32), pltpu.SemaphoreType.DMA((2, S5_T))],
        compiler_params=_params(1),
        name="s5_in",
    )(xv, mod, norm_g, w_in_t)


def _s5_core_kernel(*refs, groups, seq_chunks, has_h0, want_final):
    refs = list(refs)
    zt_ref, mt_ref, wint_ref, woutt_ref, atc_re_ref, atc_im_ref = refs[:6]
    refs = refs[6:]
    h0_ref = refs.pop(0) if has_h0 else None
    yt_ref = refs.pop(0)
    hf_ref = refs.pop(0) if want_final else None
    P, T, K = S5_STATE, S5_T, seq_chunks
    nc = zt_ref.shape[2]
    n_seq = nc // K
    lane = lax.broadcasted_iota(jnp.int32, (1, nc), 1)
    kpos = lane % K
    n_steps = K.bit_length() - 1
    across = lambda c: jnp.concatenate([c] * (nc // LANES), axis=1)
    if has_h0:
        seq = lax.broadcasted_iota(jnp.int32, (n_seq, nc), 0)
        chunk = lax.broadcasted_iota(jnp.int32, (n_seq, nc), 1)
        place = [(chunk == seq * K).astype(F32), (chunk == seq * K + (K - 1)).astype(F32)]
    if want_final:
        chunk = lax.broadcasted_iota(jnp.int32, (nc, LANES), 0)
        seq = lax.broadcasted_iota(jnp.int32, (nc, LANES), 1)
        pick = [(chunk == seq * K + (K - 1)).astype(F32), (chunk == seq * K).astype(F32)]

    for g in range(groups):
        r0 = g * S5_GROUP
        xt = jnp.concatenate([zt_ref[j, r0:r0 + S5_GROUP, :] for j in range(T)], axis=0).astype(BF)
        v = _dot(wint_ref[g], xt)
        states, finals = [], []
        for d in range(2):
            base = 2 * P * d
            cr = atc_re_ref[g, base:base + P, :]
            ci = atc_im_ref[g, base:base + P, :]
            shift = lambda x, s, d=d: pltpu.roll(x, s if d == 0 else nc - s, 1)
            first = kpos == (0 if d == 0 else K - 1)
            vr, vi = v[base:base + P, :], v[base + P:base + 2 * P, :]
            if has_h0:
                h0 = _dot_exact(h0_ref[g, base:base + 2 * P, :], place[d])
                hr, hi = h0[:P, :], h0[P:, :]
            else:
                hr = hi = 0.0
            sr = jnp.where(first, hr, shift(vr, 1))
            si = jnp.where(first, hi, shift(vi, 1))
            if want_final:
                ar, ai = across(cr), across(ci)
            for t in range(n_steps):
                s = 1 << t
                valid = (kpos >= s) if d == 0 else (kpos < K - s)
                rr = jnp.where(valid, shift(sr, s), 0.0)
                ri = jnp.where(valid, shift(si, s), 0.0)
                cr_n, ci_n = across(cr), across(ci)
                sr, si = sr + (cr_n * rr - ci_n * ri), si + (cr_n * ri + ci_n * rr)
                cr, ci = cr * cr - ci * ci, 2.0 * (cr * ci)
            states += [sr, si]
            if want_final:
                finals += [ar * sr - ai * si + vr, ar * si + ai * sr + vi]
        s_in = jnp.concatenate(states, axis=0).astype(BF)
        y = _dot(mt_ref[g], xt) + _dot(woutt_ref[g], s_in)
        for i in range(T):
            yt_ref[i, r0:r0 + S5_GROUP, :] = y[i * S5_GROUP:(i + 1) * S5_GROUP, :]
        if want_final:
            for d in range(2):
                f = jnp.concatenate(finals[2 * d:2 * d + 2], axis=0)
                hf_ref[g, 2 * P * d:2 * P * (d + 1), :] = _dot_exact(f, pick[d])


def _s5_core_stage(zt, ops, *, layer, seq_chunks, steps, h0=None, want_final=False):
    nc = zt.shape[2]
    gb = S5_GROUPS // steps
    rows = gb * S5_GROUP
    gspec = lambda r, c: pl.BlockSpec((gb, r, c), lambda i: (layer * steps + i, 0, 0))
    slab = pl.BlockSpec((S5_T, rows, nc), lambda i: (0, i, 0))
    in_specs = [slab, gspec(S5_CL, S5_CL), gspec(S5_NS, S5_CL), gspec(S5_CL, S5_NS),
                gspec(S5_NS, LANES), gspec(S5_NS, LANES)]
    args = [zt, ops["mt"], ops["wint"], ops["woutt"], ops["atc_re"], ops["atc_im"]]
    if h0 is not None:
        in_specs.append(pl.BlockSpec((gb, S5_NS, h0.shape[2]), lambda i: (i, 0, 0)))
        args.append(h0)
    out_specs = [slab]
    out_shape = [jax.ShapeDtypeStruct(zt.shape, F32)]
    if want_final:
        assert nc // seq_chunks <= LANES
        out_specs.append(pl.BlockSpec((gb, S5_NS, LANES), lambda i: (i, 0, 0)))
        out_shape.append(jax.ShapeDtypeStruct((S5_GROUPS, S5_NS, LANES), F32))
    kern = functools.partial(_s5_core_kernel, groups=gb, seq_chunks=seq_chunks, has_h0=h0 is not None,
                             want_final=want_final)
    return dict(kernel=kern, grid=(steps,), in_specs=in_specs, args=args, out_specs=out_specs, out_shape=out_shape)


def _s5_out_kernel(yt_ref, zt_ref, d_ref, wglut_ref, wa_ref, o_hbm, obuf, sem):
    i = pl.program_id(0)
    last = pl.num_programs(0) - 1
    slot = i % 2
    store = functools.partial(_token_slab_copies, o_hbm, obuf, sem, to_hbm=True)

    @pl.when(i >= 2)
    def _():
        for c in store(i - 2, slot):
            c.wait()

    d = d_ref[...]
    for j in range(0, S5_T, 2):
        pre = jnp.concatenate([yt_ref[j] + d * zt_ref[j], yt_ref[j + 1] + d * zt_ref[j + 1]], axis=1)
        ya = jax.nn.gelu(pre)
        ya = ya * jax.nn.sigmoid(_dot(wglut_ref[...], ya.astype(BF)))
        pa = lax.dot_general(ya.astype(BF), wa_ref[...], (((0,), (0,)), ((), ())), preferred_element_type=F32)
        obuf[slot, j] = pa[:S5_CB, :]
        obuf[slot, j + 1] = pa[S5_CB:, :]
    for c in store(i, slot):
        c.start()

    @pl.when(i == last)
    def _():
        for c in store(i, slot):
            c.wait()

    @pl.when((i == last) & (i >= 1))
    def _():
        for c in store(i - 1, 1 - slot):
            c.wait()


def _s5_out(yt, zt, s5_d_col, w_glu_t, w_br_a, *, layer):
    nc = yt.shape[2]
    slab = pl.BlockSpec((S5_T, S5_WIDTH, S5_CB), lambda i: (0, 0, i))
    out = pl.pallas_call(
        _s5_out_kernel,
        grid=(nc // S5_CB,),
        in_specs=[slab, slab,
                  _const_spec((None, S5_WIDTH, 1), (layer, 0, 0)),
                  _const_spec((None, S5_WIDTH, S5_WIDTH), (layer, 0, 0)),
                  _const_spec((None, S5_WIDTH, D_MODEL), (layer, 0, 0))],
        out_specs=pl.BlockSpec(memory_space=pl.ANY),
        out_shape=jax.ShapeDtypeStruct((nc, S5_T, D_MODEL), F32),
        scratch_shapes=[pltpu.VMEM((2, S5_T, S5_CB, D_MODEL), F32), pltpu.SemaphoreType.DMA((2, S5_T))],
        compiler_params=_params(1),
        name="s5_out",
    )(yt, zt, s5_d_col, w_glu_t, w_br_a)
    return out.reshape(nc * S5_T, D_MODEL)


def _mix_out_kernel(x_ref, mod_ref, g_ref, pa_ref, yb_ref, yc_ref, wb_ref, wc_ref, wgate_ref, bgate_ref,
                    wout_ref, o_ref):
    x = x_ref[...]
    h = _mod_norm(x, g_ref[...], mod_ref[0, 4:5, :], mod_ref[0, 3:4, :]).astype(BF)

    def gate(k):
        lo = k * D_MODEL
        return jax.nn.sigmoid(_dot(h, wgate_ref[:, lo:lo + D_MODEL]) + bgate_ref[:, lo:lo + D_MODEL])

    merged = gate(0) * pa_ref[...]
    merged = merged + gate(1) * _dot(yb_ref[...], wb_ref[...])
    merged = merged + gate(2) * _dot(yc_ref[...], wc_ref[...])
    y = _dot(merged.astype(BF), wout_ref[...])
    o_ref[...] = x + mod_ref[0, 5:6, :] * y


def _mix_out(x, mod, norm_g, pa, yb, yc, w_b, w_c, w_gate, b_gate, w_out, *, layer, seq_tokens):
    n = x.shape[0]
    tm = FFN_TM
    tok = lambda c: pl.BlockSpec((tm, c), lambda i: (i, 0))
    return pl.pallas_call(
        _mix_out_kernel,
        grid=(n // tm,),
        in_specs=[
            tok(D_MODEL),
            _mod_spec(mod.shape[0], seq_tokens // tm),
            _const_spec((None, 1, D_MODEL), (3 * layer + 1, 0, 0)),
            tok(D_MODEL), tok(CONV_WIDTH), tok(SG_WIDTH),
            _const_spec((None, CONV_WIDTH, D_MODEL), (layer, 0, 0)),
            _const_spec((None, SG_WIDTH, D_MODEL), (layer, 0, 0)),
            _const_spec((None, D_MODEL, 3 * D_MODEL), (layer, 0, 0)),
            _const_spec((None, 1, 3 * D_MODEL), (layer, 0, 0)),
            _const_spec((None, D_MODEL, D_MODEL), (layer, 0, 0)),
        ],
        out_specs=tok(D_MODEL),
        out_shape=jax.ShapeDtypeStruct((n, D_MODEL), F32),
        compiler_params=_params(1),
        name="mix_out",
    )(x, mod, norm_g, pa, yb, yc, w_b, w_c, w_gate, b_gate, w_out)


def _grid_pos_embed(n_tokens, dim):
    rows = n_tokens // GRID_W
    rr, cc = jnp.meshgrid(jnp.arange(rows, dtype=F32), jnp.arange(GRID_W, dtype=F32), indexing='ij')
    quarter = dim // 4
    omega = 1.0 / (10000.0 ** (jnp.arange(quarter, dtype=F32) / quarter))

    def emb(p):
        ang = p.reshape(-1)[:, None] * omega[None, :]
        return jnp.concatenate([jnp.sin(ang), jnp.cos(ang)], axis=-1)

    return jnp.concatenate([emb(rr), emb(cc)], axis=-1)


def kernel(x_prompt, x_sample, state_ssm, c, c_ctx, w_mod, b_mod, norm_g, ffn_w1, ffn_w2, w_in, w_gate, b_gate,
           s5_a_re, s5_a_im, s5_log_dt, s5_b_re, s5_b_im, s5_c_re, s5_c_im, s5_d, s5_w_glu, w_br_a, conv_w,
           conv_b, conv_ln_g, conv_ln_b, w_br_b, sg_ln_g, sg_ln_b, sg_w, sg_b, w_br_c, w_out, final_g):
    batch, seq, _ = x_prompt.shape
    dec_batch, dec_seq, _ = x_sample.shape
    assert (batch * seq) % TOK_BLOCK == 0 and TOK_BLOCK % seq == 0 and dec_seq % TOK_BLOCK == 0
    assert seq % SG_CHUNK == 0 and seq % S5_T == 0 and dec_seq % S5_T == 0
    ctx_chunks, smp_chunks = seq // S5_T, dec_seq // S5_T
    assert S5_CB % ctx_chunks == 0 and S5_CB % smp_chunks == 0
    assert ctx_chunks & (ctx_chunks - 1) == 0 and smp_chunks & (smp_chunks - 1) == 0
    assert 1 + dec_batch <= MOD_ROWS

    cond = jnp.zeros((MOD_ROWS, D_MODEL), F32).at[0].set(c_ctx).at[1:1 + dec_batch].set(c)
    mod_all = _modulation(cond, w_mod, b_mod).reshape(DEPTH, MOD_ROWS, N_MOD, D_MODEL)
    pos = _grid_pos_embed(dec_seq, D_MODEL)

    w1 = ffn_w1.astype(BF)
    w2 = ffn_w2.astype(BF)
    w_in_b = w_in.astype(BF)
    w_in_t = w_in[:, :, :S5_WIDTH].transpose(0, 2, 1).astype(BF)
    w_glu_t = s5_w_glu.transpose(0, 2, 1).astype(BF)
    w_a, w_b, w_c = w_br_a.astype(BF), w_br_b.astype(BF), w_br_c.astype(BF)
    w_gate_b, w_out_b = w_gate.astype(BF), w_out.astype(BF)
    norm_rows = norm_g.reshape(DEPTH * 3, 1, D_MODEL)
    rows = lambda a: a.reshape(DEPTH, 1, -1)
    sgw_cat = sg_w.transpose(0, 2, 1, 3).reshape(DEPTH, SG_CHUNK, SG_HEADS * SG_CHUNK).astype(BF)
    sgb_full = jnp.repeat(sg_b.transpose(0, 2, 1), SG_HEAD_DIM, axis=2)
    s5_ops = _s5_operators(s5_a_re, s5_a_im, s5_log_dt, s5_b_re, s5_b_im, s5_c_re, s5_c_im)
    s5_d_col = s5_d.reshape(DEPTH, S5_WIDTH, 1)

    conv_w8 = jnp.broadcast_to(conv_w[:, :, None, :], (DEPTH, CONV_K, SUBLANES, CONV_WIDTH))

    ctx = dict(n_seq=batch, seq_len=seq, mod_rows=slice(0, 1), mod_tokens=batch * seq)
    smp = dict(n_seq=dec_batch, seq_len=dec_seq, mod_rows=slice(1, 1 + dec_batch), mod_tokens=dec_seq)
    x_ctx = x_prompt.reshape(batch * seq, D_MODEL)
    x_smp = x_sample.reshape(dec_batch * dec_seq, D_MODEL)
    ffn_steps = x_ctx.shape[0] // FFN_TM
    assert x_smp.shape[0] // FFN_TM == ffn_steps and S5_GROUPS % ffn_steps == 0

    def ffn_stage(gr, x, l, which, **kw):
        return _ffn_stage(x, mod_all[l, gr["mod_rows"]], norm_rows, w1, w2, layer=l, which=which,
                          seq_tokens=gr["mod_tokens"], **kw)

    def branches_in(gr, x, l):
        mod = mod_all[l, gr["mod_rows"]]
        yb, yc = _mix_in(x, mod, norm_rows, w_in_b, conv_w8, rows(conv_b), rows(conv_ln_g), rows(conv_ln_b),
                         rows(sg_ln_g), rows(sg_ln_b), sgw_cat, sgb_full, layer=l, seq_len=gr["seq_len"])
        zt = _s5_in(x, mod, norm_rows, w_in_t, layer=l, seq_chunks=gr["seq_len"] // S5_T)
        return yb, yc, zt

    def s5_stage(gr, zt, l, **kw):
        return _s5_core_stage(zt, s5_ops, layer=l, seq_chunks=gr["seq_len"] // S5_T, steps=ffn_steps, **kw)

    def branches_out(gr, x, l, yb, yc, zt, yt):
        pa = _s5_out(yt, zt, s5_d_col, w_glu_t, w_a, layer=l)
        return _mix_out(x, mod_all[l, gr["mod_rows"]], norm_rows, pa, yb, yc, w_b, w_c, w_gate_b, rows(b_gate),
                        w_out_b, layer=l, seq_tokens=gr["mod_tokens"])

    ctx_states = []
    for l in range(DEPTH):
        [x_ctx], = _call([ffn_stage(ctx, x_ctx, l, 0)], "ffn")
        yb_c, yc_c, zt_c = branches_in(ctx, x_ctx, l)
        [x_smp], [yt_c, hf] = _call([ffn_stage(smp, x_smp, l, 0, pos=pos if l == 0 else None),
                                      s5_stage(ctx, zt_c, l, want_final=True)], "ffn_s5")
        hf = hf[:, :, :batch].reshape(S5_GROUPS, 2, 2, S5_STATE, batch)
        ctx_states.append(hf.transpose(4, 1, 0, 3, 2))
        x_ctx = branches_out(ctx, x_ctx, l, yb_c, yc_c, zt_c, yt_c)
        yb_s, yc_s, zt_s = branches_in(smp, x_smp, l)
        h0 = state_ssm[:, l].transpose(2, 1, 4, 3, 0).reshape(S5_GROUPS, S5_NS, dec_batch)
        final = dict(final_g=final_g) if l == DEPTH - 1 else {}
        [x_ctx], [yt_s] = _call([ffn_stage(ctx, x_ctx, l, 1, **final), s5_stage(smp, zt_s, l, h0=h0)], "ffn_s5")
        x_smp = branches_out(smp, x_smp, l, yb_s, yc_s, zt_s, yt_s)
        [x_smp], = _call([ffn_stage(smp, x_smp, l, 1, **final)], "ffn")
    y_prompt = x_ctx.reshape(batch, seq, D_MODEL)
    y_sample = x_smp.reshape(dec_batch, dec_seq, D_MODEL)
    new_state_ssm = jnp.stack(ctx_states, axis=1)
    return (y_prompt, y_sample, new_state_ssm)
```

```python
import functools

import jax
import jax.numpy as jnp
from jax import lax
from jax.experimental import pallas as pl
from jax.experimental.pallas import tpu as pltpu

D_MODEL = 1024
DEPTH = 2
GRID_W = 64
D_FF = 2816
S5_WIDTH = 512
S5_GROUP = 16
S5_GROUPS = 32
S5_STATE = 64
CONV_WIDTH = 256
CONV_K = 31
SG_WIDTH = 256
SG_CHUNK = 128
SG_HEADS = 4
SG_HEAD_DIM = SG_WIDTH // SG_HEADS
BRANCH_COLS = 2 * CONV_WIDTH
IN_COLS = S5_WIDTH + 2 * BRANCH_COLS
N_MOD = 9
EPS = 1e-6

LANES = 128
SUBLANES = 8
S5_T = 16
S5_CL = S5_T * S5_GROUP
S5_NS = 4 * S5_STATE
S5_PREP_GB = 4
S5_POW_ROWS = SUBLANES * (S5_T // SUBLANES + 1)
S5_CB = 128
MOD_ROWS = 16
MOD_TN = 1152
TOK_BLOCK = 1024
FFN_TM = 512
CONV_RC = 32
CONV_PAD = 16
CONV_SHIFT_ROWS = SUBLANES * ((CONV_PAD - CONV_K // 2 + CONV_K - 1) // SUBLANES)
VMEM_LIMIT = 56 * 1024 * 1024

BF = jnp.bfloat16
F32 = jnp.float32
HIGHEST = lax.Precision.HIGHEST


def _dot(a, b):
    return jnp.dot(a, b, preferred_element_type=F32)


def _dot_exact(a, b):
    return jnp.dot(a, b, preferred_element_type=F32, precision=HIGHEST)


def _const_spec(block, index):
    return pl.BlockSpec(block, lambda *_: index, pipeline_mode=pl.Buffered(1))


def _params(n_axes=1):
    return pltpu.CompilerParams(dimension_semantics=("arbitrary",) * n_axes,
                                vmem_limit_bytes=VMEM_LIMIT)


def _call(stages, name):
    grid = stages[0]["grid"]
    assert all(s["grid"] == grid for s in stages)
    n_in = [len(s["in_specs"]) for s in stages]
    n_out = [len(s["out_specs"]) for s in stages]
    total_in, total_out = sum(n_in), sum(n_out)

    def body(*refs):
        gens, i, o = [], 0, 0
        for s, ni, no in zip(stages, n_in, n_out):
            gens.append((s["kernel"], refs[i:i + ni] + refs[total_in + o:total_in + o + no]))
            i, o = i + ni, o + no
        live = [kern(*r) for kern, r in reversed(gens)]
        while live:
            for gen in list(live):
                if next(gen, "done") == "done":
                    live.remove(gen)

    outs = pl.pallas_call(
        body,
        grid=grid,
        in_specs=[spec for s in stages for spec in s["in_specs"]],
        out_specs=[spec for s in stages for spec in s["out_specs"]],
        out_shape=[shape for s in stages for shape in s["out_shape"]],
        compiler_params=_params(len(grid)),
        name=name,
    )(*[a for s in stages for a in s["args"]])
    split, o = [], 0
    for no in n_out:
        split.append(list(outs[o:o + no]))
        o += no
    return split


def _mod_norm(x, g, sc, sh):
    var = jnp.mean(x * x, axis=-1, keepdims=True)
    return (x * lax.rsqrt(var + EPS) * g) * (1.0 + sc) + sh


def _layernorm(x, g, b):
    mu = jnp.mean(x, axis=-1, keepdims=True)
    xc = x - mu
    var = jnp.mean(xc * xc, axis=-1, keepdims=True)
    return xc * lax.rsqrt(var + EPS) * g + b


def _mod_spec(n_mod, blocks_per_mod, rows=1):
    if n_mod == 1:
        return pl.BlockSpec((1, N_MOD, D_MODEL), lambda i: (0, 0, 0))
    return pl.BlockSpec((rows, N_MOD, D_MODEL), lambda i: (i // blocks_per_mod, 0, 0))


def _mod_kernel(c_ref, w_ref, b_ref, o_ref):
    c = c_ref[...]
    a = (c * jax.nn.sigmoid(c)).astype(BF)
    o_ref[...] = _dot(a, w_ref[...].astype(BF)) + b_ref[...]


def _modulation(cond, w_mod, b_mod):
    n = N_MOD * D_MODEL
    return pl.pallas_call(
        _mod_kernel,
        grid=(DEPTH, n // MOD_TN),
        in_specs=[
            pl.BlockSpec((MOD_ROWS, D_MODEL), lambda l, j: (0, 0)),
            pl.BlockSpec((None, D_MODEL, MOD_TN), lambda l, j: (l, 0, j)),
            pl.BlockSpec((None, 1, MOD_TN), lambda l, j: (l, 0, j)),
        ],
        out_specs=pl.BlockSpec((None, MOD_ROWS, MOD_TN), lambda l, j: (l, 0, j)),
        out_shape=jax.ShapeDtypeStruct((DEPTH, MOD_ROWS, n), F32),
        compiler_params=_params(2),
        name="modulation",
    )(cond, w_mod, b_mod.reshape(DEPTH, 1, n))


def _ffn_kernel(*refs, mod_base, add_pos, final):
    refs = list(refs)
    x_ref = refs.pop(0)
    pos_ref = refs.pop(0) if add_pos else None
    mod_ref, g_ref, w1g_ref, w1u_ref, w2_ref = refs[:5]
    refs = refs[5:]
    fg_ref = refs.pop(0) if final else None
    o_ref = refs.pop(0)

    x = x_ref[...]
    if add_pos:
        x = x + pos_ref[...]
    sh = mod_ref[0, mod_base:mod_base + 1, :]
    sc = mod_ref[0, mod_base + 1:mod_base + 2, :]
    gt = mod_ref[0, mod_base + 2:mod_base + 3, :]
    h = _mod_norm(x, g_ref[...], sc, sh).astype(BF)
    g = _dot(h, w1g_ref[...])
    yield
    u = _dot(h, w1u_ref[...])
    a = (g * jax.nn.sigmoid(g) * u).astype(BF)
    yield
    y = _dot(a, w2_ref[...])
    xn = x + (0.5 * gt) * y
    if final:
        var = jnp.mean(xn * xn, axis=-1, keepdims=True)
        xn = xn * lax.rsqrt(var + EPS) * fg_ref[...]
    o_ref[...] = xn
    yield


def _ffn_stage(x, mod, norm_g, w1, w2, *, layer, which, seq_tokens, pos=None, final_g=None):
    n = x.shape[0]
    tm = FFN_TM
    in_specs = [pl.BlockSpec((tm, D_MODEL), lambda i: (i, 0))]
    args = [x]
    if pos is not None:
        pos_blocks = pos.shape[0] // tm
        in_specs.append(pl.BlockSpec((tm, D_MODEL), lambda i: (i % pos_blocks, 0)))
        args.append(pos)
    in_specs += [
        _mod_spec(mod.shape[0], seq_tokens // tm),
        _const_spec((None, 1, D_MODEL), (3 * layer + 2 * which, 0, 0)),
        _const_spec((None, None, D_MODEL, D_FF), (layer, which, 0, 0)),
        _const_spec((None, None, D_MODEL, D_FF), (layer, which, 0, 1)),
        _const_spec((None, None, D_FF, D_MODEL), (layer, which, 0, 0)),
    ]
    args += [mod, norm_g, w1, w1, w2]
    if final_g is not None:
        in_specs.append(_const_spec((1, D_MODEL), (0, 0)))
        args.append(final_g.reshape(1, D_MODEL))
    kern = functools.partial(_ffn_kernel, mod_base=6 * which, add_pos=pos is not None,
                             final=final_g is not None)
    return dict(kernel=kern, grid=(n // tm,), in_specs=in_specs, args=args,
                out_specs=[pl.BlockSpec((tm, D_MODEL), lambda i: (i, 0))],
                out_shape=[jax.ShapeDtypeStruct((n, D_MODEL), F32)])


def _mix_in_kernel(x_ref, mod_ref, g_ref, winb_ref, winc_ref, cw_ref, cb_ref, clg_ref, clb_ref,
                   sglg_ref, sglb_ref, sgw_ref, sgb_ref, yb_ref, yc_ref, pad_ref, conv_ref, *, seq_len):
    n_seq = TOK_BLOCK // seq_len
    x = x_ref[...]
    h = _mod_norm(x, g_ref[...], mod_ref[0, 4:5, :], mod_ref[0, 3:4, :]).astype(BF)

    zb = _dot(h, winb_ref[...])
    gl = zb[:, :CONV_WIDTH] * jax.nn.sigmoid(zb[:, CONV_WIDTH:])
    zeros = jnp.zeros((CONV_PAD, CONV_WIDTH), F32)
    shifted_rows = seq_len + CONV_SHIFT_ROWS
    padded_rows = seq_len + 2 * CONV_PAD
    for s in range(n_seq):
        padded = jnp.concatenate([zeros, gl[s * seq_len:(s + 1) * seq_len, :], zeros], axis=0)
        pad_ref[0, s] = padded[:shifted_rows, :]
        for r in range(1, SUBLANES):
            pad_ref[r, s] = pltpu.roll(padded, padded_rows - r, 0)[:shifted_rows, :]
    first = CONV_PAD - CONV_K // 2
    tiles = CONV_RC // SUBLANES
    n_q = CONV_SHIFT_ROWS // SUBLANES + 1
    for s in range(n_seq):
        def conv_rows(r, carry, s=s):
            r0 = pl.multiple_of(r * CONV_RC, CONV_RC)
            accs = [None] * n_q
            for shift in range(SUBLANES):
                span = pad_ref[shift, s, pl.ds(r0, CONV_RC + CONV_SHIFT_ROWS), :]
                span = span.reshape(tiles + n_q - 1, SUBLANES, CONV_WIDTH)
                for q in range(n_q):
                    k = SUBLANES * q + shift - first
                    if 0 <= k < CONV_K:
                        term = cw_ref[k] * span[q:q + tiles]
                        accs[q] = term if accs[q] is None else accs[q] + term
            acc = (accs[0] + accs[1]) + (accs[2] + accs[3])
            conv_ref[pl.ds(pl.multiple_of(s * seq_len + r0, CONV_RC), CONV_RC), :] = acc.reshape(CONV_RC, CONV_WIDTH)
            return carry
        lax.fori_loop(0, seq_len // CONV_RC, conv_rows, 0)
    y = _layernorm(conv_ref[...] + cb_ref[...], clg_ref[...], clb_ref[...])
    yb_ref[...] = (y * jax.nn.sigmoid(y)).astype(BF)

    zc = jax.nn.gelu(_dot(h, winc_ref[...]))
    u = zc[:, :SG_WIDTH]
    v = _layernorm(zc[:, SG_WIDTH:], sglg_ref[...], sglb_ref[...]).astype(BF)
    head = lax.broadcasted_iota(jnp.int32, (SG_CHUNK, SG_WIDTH), 1) // SG_HEAD_DIM
    zero = jnp.zeros((SG_CHUNK, SG_WIDTH), BF)
    for n in range(TOK_BLOCK // SG_CHUNK):
        vn = v[n * SG_CHUNK:(n + 1) * SG_CHUNK, :]
        vcat = jnp.concatenate([jnp.where(head == hh, vn, zero) for hh in range(SG_HEADS)], axis=0)
        sgate = _dot(sgw_ref[...], vcat) + sgb_ref[...]
        yc_ref[n * SG_CHUNK:(n + 1) * SG_CHUNK, :] = (u[n * SG_CHUNK:(n + 1) * SG_CHUNK, :] * sgate).astype(BF)


def _mix_in(x, mod, norm_g, w_in, conv_w, conv_b, conv_ln_g, conv_ln_b, sg_ln_g, sg_ln_b, sgw_cat, sgb_full,
            *, layer, seq_len):
    n = x.shape[0]
    n_seq = TOK_BLOCK // seq_len
    vec = lambda width: _const_spec((None, 1, width), (layer, 0, 0))
    kern = functools.partial(_mix_in_kernel, seq_len=seq_len)
    return pl.pallas_call(
        kern,
        grid=(n // TOK_BLOCK,),
        in_specs=[
            pl.BlockSpec((TOK_BLOCK, D_MODEL), lambda i: (i, 0)),
            _mod_spec(mod.shape[0], max(seq_len // TOK_BLOCK, 1)),
            _const_spec((None, 1, D_MODEL), (3 * layer + 1, 0, 0)),
            _const_spec((None, D_MODEL, BRANCH_COLS), (layer, 0, S5_WIDTH // BRANCH_COLS)),
            _const_spec((None, D_MODEL, BRANCH_COLS), (layer, 0, S5_WIDTH // BRANCH_COLS + 1)),
            _const_spec((None, CONV_K, SUBLANES, CONV_WIDTH), (layer, 0, 0, 0)),
            vec(CONV_WIDTH), vec(CONV_WIDTH), vec(CONV_WIDTH), vec(SG_WIDTH), vec(SG_WIDTH),
            _const_spec((None, SG_CHUNK, SG_HEADS * SG_CHUNK), (layer, 0, 0)),
            _const_spec((None, SG_CHUNK, SG_WIDTH), (layer, 0, 0)),
        ],
        out_specs=[
            pl.BlockSpec((TOK_BLOCK, CONV_WIDTH), lambda i: (i, 0)),
            pl.BlockSpec((TOK_BLOCK, SG_WIDTH), lambda i: (i, 0)),
        ],
        out_shape=[
            jax.ShapeDtypeStruct((n, CONV_WIDTH), BF),
            jax.ShapeDtypeStruct((n, SG_WIDTH), BF),
        ],
        scratch_shapes=[pltpu.VMEM((SUBLANES, n_seq, seq_len + CONV_SHIFT_ROWS, CONV_WIDTH), F32),
                        pltpu.VMEM((TOK_BLOCK, CONV_WIDTH), F32)],
        compiler_params=_params(1),
        name="mix_in",
    )(x, mod, norm_g, w_in, w_in, conv_w, conv_b, conv_ln_g, conv_ln_b, sg_ln_g, sg_ln_b, sgw_cat, sgb_full)


def _s5_prep_kernel(arow_re_ref, arow_im_ref, dtrow_ref, b4_re_ref, b4_im_ref, c4_re_ref, c4_im_ref,
                    mt_ref, win_ref, woutt_ref, at_re_ref, at_im_ref):
    T, H = S5_T, S5_GROUP
    lanes = 2 * S5_CL
    lane = lax.broadcasted_iota(jnp.int32, (1, S5_NS), 1)
    lane_im = lane >= 2 * S5_STATE
    lane_bwd = (lane // S5_STATE) % 2 == 1
    e_col = lax.broadcasted_iota(jnp.int32, (S5_POW_ROWS, 1), 0).astype(F32)
    zeros = jnp.zeros((H, S5_NS), F32)
    for g in range(S5_PREP_GB):
        a_re, a_im = arow_re_ref[g], arow_im_ref[g]
        dt = jnp.exp(dtrow_ref[g])
        mag = jnp.exp(a_re * dt * e_col)
        ang = a_im * dt * e_col
        pw_re, pw_im = mag * jnp.cos(ang), mag * jnp.sin(ang)
        den = a_re * a_re + a_im * a_im
        nr, ni = pw_re[1:2, :] - 1.0, pw_im[1:2, :]
        q_re = (nr * a_re + ni * a_im) / den
        q_im = (ni * a_re - nr * a_im) / den
        b_re, b_im = b4_re_ref[g], b4_im_ref[g]
        bb_re = q_re * b_re - q_im * b_im
        bb_im = q_re * b_im + q_im * b_re
        c_re, c_im = c4_re_ref[g], c4_im_ref[g]

        def power(e_f, e_b, pw_re=pw_re, pw_im=pw_im):
            return (jnp.where(lane_bwd, pw_re[e_b:e_b + 1, :], pw_re[e_f:e_f + 1, :]),
                    jnp.where(lane_bwd, pw_im[e_b:e_b + 1, :], pw_im[e_f:e_f + 1, :]))

        def times_b(e_f, e_b, bb_re=bb_re, bb_im=bb_im, power=power):
            p_re, p_im = power(e_f, e_b)
            return jnp.where(lane_im, p_re * bb_im + p_im * bb_re, p_re * bb_re - p_im * bb_im)

        def times_c(e_f, e_b, c_re=c_re, c_im=c_im, power=power):
            p_re, p_im = power(e_f, e_b)
            return jnp.where(lane_im, -(c_re * p_im + c_im * p_re), c_re * p_re - c_im * p_im)

        win_ref[g] = jnp.concatenate([times_b(T - 1 - j, j) for j in range(T)], axis=0).astype(BF)
        woutt_ref[g] = jnp.concatenate([times_c(i + 1, T - i) for i in range(T)], axis=0).astype(BF)
        blocks = []
        for b in range(2 * T - 1):
            m = T - 1 - b
            w = times_b(abs(m), abs(m))
            if m > 0:
                w = jnp.where(lane_bwd, 0.0, w)
            elif m < 0:
                w = jnp.where(lane_bwd, w, 0.0)
            blocks.append(w)
        ystack = jnp.concatenate(blocks + [zeros], axis=0)
        c_signed = jnp.where(lane_im, -c_im, c_re)
        krev = lax.dot_general(c_signed, ystack, (((1,), (1,)), ((), ())), preferred_element_type=F32,
                               precision=HIGHEST)
        mt_ref[g] = jnp.concatenate(
            [pltpu.roll(krev, (lanes - H * (T - 1 - i)) % lanes, 1)[:, :S5_CL] for i in range(T)],
            axis=0).astype(BF)
        at_re_ref[g] = pw_re[T:T + 1, :]
        at_im_ref[g] = pw_im[T:T + 1, :]


def _s5_operators(a_re, a_im, log_dt, b_re, b_im, c_re, c_im):
    G, P, H = S5_GROUPS, S5_STATE, S5_GROUP
    n = DEPTH * G
    rep = lambda a: jnp.broadcast_to(a.transpose(0, 2, 1, 3)[:, :, None, :, :],
                                     (DEPTH, G, 2, 2, P)).reshape(n, 1, S5_NS)
    dt4 = rep(jnp.broadcast_to(log_dt[..., None], (DEPTH, 2, G, P)))
    c4 = lambda c: jnp.broadcast_to(c.transpose(0, 2, 3, 1, 4)[:, :, :, None, :, :],
                                    (DEPTH, G, H, 2, 2, P)).reshape(n, H, S5_NS)
    b4 = lambda b: c4(b.transpose(0, 1, 2, 4, 3))
    blk = lambda r, c: pl.BlockSpec((S5_PREP_GB, r, c), lambda i: (i, 0, 0))
    op = jax.ShapeDtypeStruct((n, S5_CL, S5_CL), BF)
    row = jax.ShapeDtypeStruct((n, 1, S5_NS), F32)
    mt, win, woutt, at_re, at_im = pl.pallas_call(
        _s5_prep_kernel,
        grid=(n // S5_PREP_GB,),
        in_specs=[blk(1, S5_NS), blk(1, S5_NS), blk(1, S5_NS), blk(H, S5_NS), blk(H, S5_NS), blk(H, S5_NS),
                  blk(H, S5_NS)],
        out_specs=[blk(S5_CL, S5_CL), blk(S5_CL, S5_NS), blk(S5_CL, S5_NS), blk(1, S5_NS), blk(1, S5_NS)],
        out_shape=[op, op, op, row, row],
        compiler_params=_params(1),
        name="s5_prep",
    )(rep(a_re), rep(a_im), dt4, b4(b_re), b4(b_im), c4(c_re), c4(c_im))
    return dict(mt=mt, win=win, woutt=woutt, at_re=at_re, at_im=at_im)


def _token_slab_copies(hbm_ref, buf_ref, sem_ref, block, slot, to_hbm):
    copies = []
    for j in range(S5_T):
        hbm = hbm_ref.at[pl.ds(block * S5_CB, S5_CB), j, :]
        vmem = buf_ref.at[slot, j]
        src, dst = (vmem, hbm) if to_hbm else (hbm, vmem)
        copies.append(pltpu.make_async_copy(src, dst, sem_ref.at[slot, j]))
    return copies


def _s5_in_kernel(x_hbm, mod_ref, g_ref, wt_ref, zt_ref, xbuf, sem, *, mod_rows):
    g = g_ref[...]
    rows = S5_CB // mod_rows
    i = pl.program_id(0)
    slot = i % 2
    fetch = functools.partial(_token_slab_copies, x_hbm, xbuf, sem, to_hbm=False)

    @pl.when(i == 0)
    def _():
        for c in fetch(0, 0):
            c.start()

    @pl.when(i + 1 < pl.num_programs(0))
    def _():
        for c in fetch(i + 1, 1 - slot):
            c.start()

    for c in fetch(i, slot):
        c.wait()

    def normed(j):
        x = xbuf[slot, j]
        parts = [_mod_norm(x[m * rows:(m + 1) * rows, :], g, mod_ref[m, 4:5, :], mod_ref[m, 3:4, :])
                 for m in range(mod_rows)]
        return parts[0] if mod_rows == 1 else jnp.concatenate(parts, axis=0)

    for j in range(0, S5_T, 2):
        h = jnp.concatenate([normed(j), normed(j + 1)], axis=0).astype(BF)
        zt = lax.dot_general(wt_ref[...], h, (((1,), (1,)), ((), ())), preferred_element_type=F32)
        zt_ref[j] = zt[:, :S5_CB]
        zt_ref[j + 1] = zt[:, S5_CB:]


def _s5_in(x, mod, norm_g, w_in_t, *, layer, seq_chunks):
    nc = x.shape[0] // S5_T
    xv = x.reshape(nc, S5_T, D_MODEL)
    mod_rows = 1 if mod.shape[0] == 1 else S5_CB // seq_chunks
    in_specs = [
        pl.BlockSpec(memory_space=pl.ANY),
        _mod_spec(mod.shape[0], 1, rows=mod_rows),
        _const_spec((None, 1, D_MODEL), (3 * layer + 1, 0, 0)),
        _const_spec((None, S5_WIDTH, D_MODEL), (layer, 0, 0)),
    ]
    return pl.pallas_call(
        functools.partial(_s5_in_kernel, mod_rows=mod_rows),
        grid=(nc // S5_CB,),
        in_specs=in_specs,
        out_specs=pl.BlockSpec((S5_T, S5_WIDTH, S5_CB), lambda i: (0, 0, i)),
        out_shape=jax.ShapeDtypeStruct((S5_T, S5_WIDTH, nc), F32),
        scratch_shapes=[pltpu.VMEM((2, S5_T, S5_CB, D_MODEL), F32), pltpu.SemaphoreType.DMA((2, S5_T))],
        compiler_params=_params(1),
        name="s5_in",
    )(xv, mod, norm_g, w_in_t)


def _s5_core_kernel(*refs, groups, seq_chunks, has_h0, want_final):
    refs = list(refs)
    zt_ref, mt_ref, win_ref, woutt_ref, at_re_ref, at_im_ref = refs[:6]
    refs = refs[6:]
    h0_ref = refs.pop(0) if has_h0 else None
    yt_ref = refs.pop(0)
    hf_ref = refs.pop(0) if want_final else None
    T, K = S5_T, seq_chunks
    half = S5_NS // 2
    nc = zt_ref.shape[2]
    n_seq = nc // K
    n_steps = K.bit_length() - 1
    fwd = lax.broadcasted_iota(jnp.int32, (nc, half), 1) < S5_STATE
    bwd = jnp.logical_not(fwd)
    kpos = lax.broadcasted_iota(jnp.int32, (nc, half), 0) % K
    first = (fwd & (kpos == 0)) | (bwd & (kpos == K - 1))
    valid = [(fwd & (kpos >= (1 << t))) | (bwd & (kpos < K - (1 << t))) for t in range(n_steps)]
    shift = lambda x, s: jnp.where(fwd, pltpu.roll(x, s, 0), pltpu.roll(x, nc - s, 0))
    if has_h0:
        chunk = lax.broadcasted_iota(jnp.int32, (nc, n_seq), 0)
        seq = lax.broadcasted_iota(jnp.int32, (nc, n_seq), 1)
        place_f = (chunk == seq * K).astype(F32)
        place_b = (chunk == seq * K + (K - 1)).astype(F32)
        fwd_full = jnp.concatenate([fwd, fwd], axis=1)
    if want_final:
        seq = lax.broadcasted_iota(jnp.int32, (LANES, nc), 0)
        chunk = lax.broadcasted_iota(jnp.int32, (LANES, nc), 1)
        pick_f = (chunk == seq * K + (K - 1)).astype(F32)
        pick_b = (chunk == seq * K).astype(F32)
        fwd_row = lax.broadcasted_iota(jnp.int32, (1, S5_NS), 1) % half < S5_STATE

    xts, vs, h0s = [], [], []
    for g in range(groups):
        r0 = g * S5_GROUP
        xt = jnp.concatenate([zt_ref[j, r0:r0 + S5_GROUP, :] for j in range(T)], axis=0).astype(BF)
        xts.append(xt)
        vs.append(lax.dot_general(xt, win_ref[g], (((0,), (0,)), ((), ())), preferred_element_type=F32))
        if has_h0:
            h0s.append(jnp.where(fwd_full, _dot_exact(place_f, h0_ref[g]), _dot_exact(place_b, h0_ref[g])))
    yield
    s_ins, fins = [], []
    for g in range(groups):
        vr, vi = vs[g][:, :half], vs[g][:, half:]
        cr, ci = at_re_ref[g, :, :half], at_im_ref[g, :, :half]
        ar, ai = cr, ci
        hr, hi = (h0s[g][:, :half], h0s[g][:, half:]) if has_h0 else (0.0, 0.0)
        sr = jnp.where(first, hr, shift(vr, 1))
        si = jnp.where(first, hi, shift(vi, 1))
        for t in range(n_steps):
            rr = jnp.where(valid[t], shift(sr, 1 << t), 0.0)
            ri = jnp.where(valid[t], shift(si, 1 << t), 0.0)
            sr, si = sr + (cr * rr - ci * ri), si + (cr * ri + ci * rr)
            cr, ci = cr * cr - ci * ci, 2.0 * (cr * ci)
        s_ins.append(jnp.concatenate([sr, si], axis=1).astype(BF))
        if want_final:
            fins.append(jnp.concatenate([ar * sr - ai * si + vr, ar * si + ai * sr + vi], axis=1))
    yield
    for g in range(groups):
        r0 = g * S5_GROUP
        y = _dot(mt_ref[g], xts[g]) + lax.dot_general(woutt_ref[g], s_ins[g], (((1,), (1,)), ((), ())),
                                                      preferred_element_type=F32)
        for i in range(T):
            yt_ref[i, r0:r0 + S5_GROUP, :] = y[i * S5_GROUP:(i + 1) * S5_GROUP, :]
        if want_final:
            hf_ref[g] = jnp.where(fwd_row, _dot_exact(pick_f, fins[g]), _dot_exact(pick_b, fins[g]))


def _s5_core_stage(zt, ops, *, layer, seq_chunks, steps, h0=None, want_final=False):
    nc = zt.shape[2]
    gb = S5_GROUPS // steps
    rows = gb * S5_GROUP
    gspec = lambda r, c: pl.BlockSpec((gb, r, c), lambda i: (layer * steps + i, 0, 0))
    slab = pl.BlockSpec((S5_T, rows, nc), lambda i: (0, i, 0))
    in_specs = [slab, gspec(S5_CL, S5_CL), gspec(S5_CL, S5_NS), gspec(S5_CL, S5_NS),
                gspec(1, S5_NS), gspec(1, S5_NS)]
    args = [zt, ops["mt"], ops["win"], ops["woutt"], ops["at_re"], ops["at_im"]]
    if h0 is not None:
        in_specs.append(pl.BlockSpec((gb, h0.shape[1], S5_NS), lambda i: (i, 0, 0)))
        args.append(h0)
    out_specs = [slab]
    out_shape = [jax.ShapeDtypeStruct(zt.shape, F32)]
    if want_final:
        assert nc // seq_chunks <= LANES
        out_specs.append(pl.BlockSpec((gb, LANES, S5_NS), lambda i: (i, 0, 0)))
        out_shape.append(jax.ShapeDtypeStruct((S5_GROUPS, LANES, S5_NS), F32))
    kern = functools.partial(_s5_core_kernel, groups=gb, seq_chunks=seq_chunks, has_h0=h0 is not None,
                             want_final=want_final)
    return dict(kernel=kern, grid=(steps,), in_specs=in_specs, args=args, out_specs=out_specs, out_shape=out_shape)


def _s5_out_kernel(yt_ref, zt_ref, d_ref, wglut_ref, wa_ref, o_hbm, obuf, sem):
    i = pl.program_id(0)
    last = pl.num_programs(0) - 1
    slot = i % 2
    store = functools.partial(_token_slab_copies, o_hbm, obuf, sem, to_hbm=True)

    @pl.when(i >= 2)
    def _():
        for c in store(i - 2, slot):
            c.wait()

    d = d_ref[...]
    for j in range(0, S5_T, 2):
        pre = jnp.concatenate([yt_ref[j] + d * zt_ref[j], yt_ref[j + 1] + d * zt_ref[j + 1]], axis=1)
        ya = jax.nn.gelu(pre)
        ya = ya * jax.nn.sigmoid(_dot(wglut_ref[...], ya.astype(BF)))
        pa = lax.dot_general(ya.astype(BF), wa_ref[...], (((0,), (0,)), ((), ())), preferred_element_type=F32)
        obuf[slot, j] = pa[:S5_CB, :]
        obuf[slot, j + 1] = pa[S5_CB:, :]
    for c in store(i, slot):
        c.start()

    @pl.when(i == last)
    def _():
        for c in store(i, slot):
            c.wait()

    @pl.when((i == last) & (i >= 1))
    def _():
        for c in store(i - 1, 1 - slot):
            c.wait()


def _s5_out(yt, zt, s5_d_col, w_glu_t, w_br_a, *, layer):
    nc = yt.shape[2]
    slab = pl.BlockSpec((S5_T, S5_WIDTH, S5_CB), lambda i: (0, 0, i))
    out = pl.pallas_call(
        _s5_out_kernel,
        grid=(nc // S5_CB,),
        in_specs=[slab, slab,
                  _const_spec((None, S5_WIDTH, 1), (layer, 0, 0)),
                  _const_spec((None, S5_WIDTH, S5_WIDTH), (layer, 0, 0)),
                  _const_spec((None, S5_WIDTH, D_MODEL), (layer, 0, 0))],
        out_specs=pl.BlockSpec(memory_space=pl.ANY),
        out_shape=jax.ShapeDtypeStruct((nc, S5_T, D_MODEL), F32),
        scratch_shapes=[pltpu.VMEM((2, S5_T, S5_CB, D_MODEL), F32), pltpu.SemaphoreType.DMA((2, S5_T))],
        compiler_params=_params(1),
        name="s5_out",
    )(yt, zt, s5_d_col, w_glu_t, w_br_a)
    return out.reshape(nc * S5_T, D_MODEL)


def _mix_out_kernel(x_ref, mod_ref, g_ref, pa_ref, yb_ref, yc_ref, wb_ref, wc_ref, wgate_ref, bgate_ref,
                    wout_ref, o_ref):
    x = x_ref[...]
    h = _mod_norm(x, g_ref[...], mod_ref[0, 4:5, :], mod_ref[0, 3:4, :]).astype(BF)

    def gate(k):
        lo = k * D_MODEL
        return jax.nn.sigmoid(_dot(h, wgate_ref[:, lo:lo + D_MODEL]) + bgate_ref[:, lo:lo + D_MODEL])

    merged = gate(0) * pa_ref[...]
    merged = merged + gate(1) * _dot(yb_ref[...], wb_ref[...])
    merged = merged + gate(2) * _dot(yc_ref[...], wc_ref[...])
    y = _dot(merged.astype(BF), wout_ref[...])
    o_ref[...] = x + mod_ref[0, 5:6, :] * y


def _mix_out(x, mod, norm_g, pa, yb, yc, w_b, w_c, w_gate, b_gate, w_out, *, layer, seq_tokens):
    n = x.shape[0]
    tm = FFN_TM
    tok = lambda c: pl.BlockSpec((tm, c), lambda i: (i, 0))
    return pl.pallas_call(
        _mix_out_kernel,
        grid=(n // tm,),
        in_specs=[
            tok(D_MODEL),
            _mod_spec(mod.shape[0], seq_tokens // tm),
            _const_spec((None, 1, D_MODEL), (3 * layer + 1, 0, 0)),
            tok(D_MODEL), tok(CONV_WIDTH), tok(SG_WIDTH),
            _const_spec((None, CONV_WIDTH, D_MODEL), (layer, 0, 0)),
            _const_spec((None, SG_WIDTH, D_MODEL), (layer, 0, 0)),
            _const_spec((None, D_MODEL, 3 * D_MODEL), (layer, 0, 0)),
            _const_spec((None, 1, 3 * D_MODEL), (layer, 0, 0)),
            _const_spec((None, D_MODEL, D_MODEL), (layer, 0, 0)),
        ],
        out_specs=tok(D_MODEL),
        out_shape=jax.ShapeDtypeStruct((n, D_MODEL), F32),
        compiler_params=_params(1),
        name="mix_out",
    )(x, mod, norm_g, pa, yb, yc, w_b, w_c, w_gate, b_gate, w_out)


def _grid_pos_embed(n_tokens, dim):
    rows = n_tokens // GRID_W
    rr, cc = jnp.meshgrid(jnp.arange(rows, dtype=F32), jnp.arange(GRID_W, dtype=F32), indexing='ij')
    quarter = dim // 4
    omega = 1.0 / (10000.0 ** (jnp.arange(quarter, dtype=F32) / quarter))

    def emb(p):
        ang = p.reshape(-1)[:, None] * omega[None, :]
        return jnp.concatenate([jnp.sin(ang), jnp.cos(ang)], axis=-1)

    return jnp.concatenate([emb(rr), emb(cc)], axis=-1)


def kernel(x_prompt, x_sample, state_ssm, c, c_ctx, w_mod, b_mod, norm_g, ffn_w1, ffn_w2, w_in, w_gate, b_gate,
           s5_a_re, s5_a_im, s5_log_dt, s5_b_re, s5_b_im, s5_c_re, s5_c_im, s5_d, s5_w_glu, w_br_a, conv_w,
           conv_b, conv_ln_g, conv_ln_b, w_br_b, sg_ln_g, sg_ln_b, sg_w, sg_b, w_br_c, w_out, final_g):
    batch, seq, _ = x_prompt.shape
    dec_batch, dec_seq, _ = x_sample.shape
    assert (batch * seq) % TOK_BLOCK == 0 and TOK_BLOCK % seq == 0 and dec_seq % TOK_BLOCK == 0
    assert seq % SG_CHUNK == 0 and seq % S5_T == 0 and dec_seq % S5_T == 0
    ctx_chunks, smp_chunks = seq // S5_T, dec_seq // S5_T
    assert S5_CB % ctx_chunks == 0 and S5_CB % smp_chunks == 0
    assert ctx_chunks & (ctx_chunks - 1) == 0 and smp_chunks & (smp_chunks - 1) == 0
    assert 1 + dec_batch <= MOD_ROWS

    cond = jnp.zeros((MOD_ROWS, D_MODEL), F32).at[0].set(c_ctx).at[1:1 + dec_batch].set(c)
    mod_all = _modulation(cond, w_mod, b_mod).reshape(DEPTH, MOD_ROWS, N_MOD, D_MODEL)
    pos = _grid_pos_embed(dec_seq, D_MODEL)

    w1 = ffn_w1.astype(BF)
    w2 = ffn_w2.astype(BF)
    w_in_b = w_in.astype(BF)
    w_in_t = w_in[:, :, :S5_WIDTH].transpose(0, 2, 1).astype(BF)
    w_glu_t = s5_w_glu.transpose(0, 2, 1).astype(BF)
    w_a, w_b, w_c = w_br_a.astype(BF), w_br_b.astype(BF), w_br_c.astype(BF)
    w_gate_b, w_out_b = w_gate.astype(BF), w_out.astype(BF)
    norm_rows = norm_g.reshape(DEPTH * 3, 1, D_MODEL)
    rows = lambda a: a.reshape(DEPTH, 1, -1)
    sgw_cat = sg_w.transpose(0, 2, 1, 3).reshape(DEPTH, SG_CHUNK, SG_HEADS * SG_CHUNK).astype(BF)
    sgb_full = jnp.repeat(sg_b.transpose(0, 2, 1), SG_HEAD_DIM, axis=2)
    s5_ops = _s5_operators(s5_a_re, s5_a_im, s5_log_dt, s5_b_re, s5_b_im, s5_c_re, s5_c_im)
    s5_d_col = s5_d.reshape(DEPTH, S5_WIDTH, 1)

    conv_w8 = jnp.broadcast_to(conv_w[:, :, None, :], (DEPTH, CONV_K, SUBLANES, CONV_WIDTH))

    ctx = dict(n_seq=batch, seq_len=seq, mod_rows=slice(0, 1), mod_tokens=batch * seq)
    smp = dict(n_seq=dec_batch, seq_len=dec_seq, mod_rows=slice(1, 1 + dec_batch), mod_tokens=dec_seq)
    x_ctx = x_prompt.reshape(batch * seq, D_MODEL)
    x_smp = x_sample.reshape(dec_batch * dec_seq, D_MODEL)
    ffn_steps = x_ctx.shape[0] // FFN_TM
    assert x_smp.shape[0] // FFN_TM == ffn_steps and S5_GROUPS % ffn_steps == 0

    def ffn_stage(gr, x, l, which, **kw):
        return _ffn_stage(x, mod_all[l, gr["mod_rows"]], norm_rows, w1, w2, layer=l, which=which,
                          seq_tokens=gr["mod_tokens"], **kw)

    def branches_in(gr, x, l):
        mod = mod_all[l, gr["mod_rows"]]
        yb, yc = _mix_in(x, mod, norm_rows, w_in_b, conv_w8, rows(conv_b), rows(conv_ln_g), rows(conv_ln_b),
                         rows(sg_ln_g), rows(sg_ln_b), sgw_cat, sgb_full, layer=l, seq_len=gr["seq_len"])
        zt = _s5_in(x, mod, norm_rows, w_in_t, layer=l, seq_chunks=gr["seq_len"] // S5_T)
        return yb, yc, zt

    def s5_stage(gr, zt, l, **kw):
        return _s5_core_stage(zt, s5_ops, layer=l, seq_chunks=gr["seq_len"] // S5_T, steps=ffn_steps, **kw)

    def branches_out(gr, x, l, yb, yc, zt, yt):
        pa = _s5_out(yt, zt, s5_d_col, w_glu_t, w_a, layer=l)
        return _mix_out(x, mod_all[l, gr["mod_rows"]], norm_rows, pa, yb, yc, w_b, w_c, w_gate_b, rows(b_gate),
                        w_out_b, layer=l, seq_tokens=gr["mod_tokens"])

    ctx_states = []
    for l in range(DEPTH):
        [x_ctx], = _call([ffn_stage(ctx, x_ctx, l, 0)], "ffn")
        yb_c, yc_c, zt_c = branches_in(ctx, x_ctx, l)
        [x_smp], [yt_c, hf] = _call([ffn_stage(smp, x_smp, l, 0, pos=pos if l == 0 else None),
                                      s5_stage(ctx, zt_c, l, want_final=True)], "ffn_s5")
        hf = hf[:, :batch, :].reshape(S5_GROUPS, batch, 2, 2, S5_STATE)
        ctx_states.append(hf.transpose(1, 3, 0, 4, 2))
        x_ctx = branches_out(ctx, x_ctx, l, yb_c, yc_c, zt_c, yt_c)
        yb_s, yc_s, zt_s = branches_in(smp, x_smp, l)
        h0 = state_ssm[:, l].transpose(2, 0, 4, 1, 3).reshape(S5_GROUPS, dec_batch, S5_NS)
        final = dict(final_g=final_g) if l == DEPTH - 1 else {}
        [x_ctx], [yt_s] = _call([ffn_stage(ctx, x_ctx, l, 1, **final), s5_stage(smp, zt_s, l, h0=h0)], "ffn_s5")
        x_smp = branches_out(smp, x_smp, l, yb_s, yc_s, zt_s, yt_s)
        [x_smp], = _call([ffn_stage(smp, x_smp, l, 1, **final)], "ffn")
    y_prompt = x_ctx.reshape(batch, seq, D_MODEL)
    y_sample = x_smp.reshape(dec_batch, dec_seq, D_MODEL)
    new_state_ssm = jnp.stack(ctx_states, axis=1)
    return (y_prompt, y_sample, new_state_ssm)
```

```python
import functools
import inspect

import jax
import jax.numpy as jnp
from jax import lax
from jax.experimental import pallas as pl
from jax.experimental.pallas import tpu as pltpu

D_MODEL = 1024
DEPTH = 2
GRID_W = 64
D_FF = 2816
S5_WIDTH = 512
S5_GROUP = 16
S5_GROUPS = 32
S5_STATE = 64
CONV_WIDTH = 256
CONV_K = 31
SG_WIDTH = 256
SG_CHUNK = 128
SG_HEADS = 4
SG_HEAD_DIM = SG_WIDTH // SG_HEADS
BRANCH_COLS = 2 * CONV_WIDTH
IN_COLS = S5_WIDTH + 2 * BRANCH_COLS
N_MOD = 9
EPS = 1e-6

LANES = 128
SUBLANES = 8
S5_T = 16
S5_CL = S5_T * S5_GROUP
S5_NS = 4 * S5_STATE
S5_PREP_GB = 4
S5_POW_ROWS = SUBLANES * (S5_T // SUBLANES + 1)
S5_CB = 128
MOD_ROWS = 16
MOD_TN = 1152
TOK_BLOCK = 1024
FFN_TM = 512
CONV_RC = 32
CONV_PAD = 16
CONV_SHIFT_ROWS = SUBLANES * ((CONV_PAD - CONV_K // 2 + CONV_K - 1) // SUBLANES)
VMEM_LIMIT = 56 * 1024 * 1024

BF = jnp.bfloat16
F32 = jnp.float32
HIGHEST = lax.Precision.HIGHEST


def _dot(a, b):
    return jnp.dot(a, b, preferred_element_type=F32)


def _dot_exact(a, b):
    return jnp.dot(a, b, preferred_element_type=F32, precision=HIGHEST)


def _const_spec(block, index):
    return pl.BlockSpec(block, lambda *_: index, pipeline_mode=pl.Buffered(1))


def _params(n_axes=1):
    return pltpu.CompilerParams(dimension_semantics=("arbitrary",) * n_axes,
                                vmem_limit_bytes=VMEM_LIMIT)


def _call(stages, name):
    grid = stages[0]["grid"]
    assert all(s["grid"] == grid for s in stages)
    n_in = [len(s["in_specs"]) for s in stages]
    n_out = [len(s["out_specs"]) for s in stages]
    total_in, total_out = sum(n_in), sum(n_out)

    def body(*refs):
        bound, i, o = [], 0, 0
        for s, ni, no in zip(stages, n_in, n_out):
            bound.append((s["kernel"], refs[i:i + ni] + refs[total_in + o:total_in + o + no]))
            i, o = i + ni, o + no
        live = [run for run in (kern(*r) for kern, r in reversed(bound)) if inspect.isgenerator(run)]
        while live:
            for gen in list(live):
                if next(gen, "done") == "done":
                    live.remove(gen)

    outs = pl.pallas_call(
        body,
        grid=grid,
        in_specs=[spec for s in stages for spec in s["in_specs"]],
        out_specs=[spec for s in stages for spec in s["out_specs"]],
        out_shape=[shape for s in stages for shape in s["out_shape"]],
        compiler_params=_params(len(grid)),
        name=name,
    )(*[a for s in stages for a in s["args"]])
    split, o = [], 0
    for no in n_out:
        split.append(list(outs[o:o + no]))
        o += no
    return split


def _mod_norm(x, g, sc, sh):
    var = jnp.mean(x * x, axis=-1, keepdims=True)
    return (x * lax.rsqrt(var + EPS) * g) * (1.0 + sc) + sh


def _layernorm(x, g, b):
    mu = jnp.mean(x, axis=-1, keepdims=True)
    xc = x - mu
    var = jnp.mean(xc * xc, axis=-1, keepdims=True)
    return xc * lax.rsqrt(var + EPS) * g + b


def _mod_spec(n_mod, blocks_per_mod, rows=1):
    if n_mod == 1:
        return pl.BlockSpec((1, N_MOD, D_MODEL), lambda i: (0, 0, 0))
    return pl.BlockSpec((rows, N_MOD, D_MODEL), lambda i: (i // blocks_per_mod, 0, 0))


def _mod_kernel(c_ref, w_ref, b_ref, o_ref):
    c = c_ref[...]
    a = (c * jax.nn.sigmoid(c)).astype(BF)
    o_ref[...] = _dot(a, w_ref[...].astype(BF)) + b_ref[...]


def _modulation(cond, w_mod, b_mod):
    n = N_MOD * D_MODEL
    return pl.pallas_call(
        _mod_kernel,
        grid=(DEPTH, n // MOD_TN),
        in_specs=[
            pl.BlockSpec((MOD_ROWS, D_MODEL), lambda l, j: (0, 0)),
            pl.BlockSpec((None, D_MODEL, MOD_TN), lambda l, j: (l, 0, j)),
            pl.BlockSpec((None, 1, MOD_TN), lambda l, j: (l, 0, j)),
        ],
        out_specs=pl.BlockSpec((None, MOD_ROWS, MOD_TN), lambda l, j: (l, 0, j)),
        out_shape=jax.ShapeDtypeStruct((DEPTH, MOD_ROWS, n), F32),
        compiler_params=_params(2),
        name="modulation",
    )(cond, w_mod, b_mod.reshape(DEPTH, 1, n))


def _ffn_kernel(*refs, mod_base, add_pos, final):
    refs = list(refs)
    x_ref = refs.pop(0)
    pos_ref = refs.pop(0) if add_pos else None
    mod_ref, g_ref, w1g_ref, w1u_ref, w2_ref = refs[:5]
    refs = refs[5:]
    fg_ref = refs.pop(0) if final else None
    o_ref = refs.pop(0)

    x = x_ref[...]
    if add_pos:
        x = x + pos_ref[...]
    sh = mod_ref[0, mod_base:mod_base + 1, :]
    sc = mod_ref[0, mod_base + 1:mod_base + 2, :]
    gt = mod_ref[0, mod_base + 2:mod_base + 3, :]
    h = _mod_norm(x, g_ref[...], sc, sh).astype(BF)
    g = _dot(h, w1g_ref[...])
    yield
    u = _dot(h, w1u_ref[...])
    a = (g * jax.nn.sigmoid(g) * u).astype(BF)
    yield
    y = _dot(a, w2_ref[...])
    xn = x + (0.5 * gt) * y
    if final:
        var = jnp.mean(xn * xn, axis=-1, keepdims=True)
        xn = xn * lax.rsqrt(var + EPS) * fg_ref[...]
    o_ref[...] = xn
    yield


def _cast_kernel(x_ref, o_ref):
    o_ref[...] = x_ref[...].astype(o_ref.dtype)


def _cast_stage(w, index, steps):
    rows, cols = w.shape[-2:]
    lead = (None,) * len(index)
    return dict(kernel=_cast_kernel, grid=(steps,), args=[w],
                in_specs=[pl.BlockSpec(lead + (rows // steps, cols), lambda i: tuple(index) + (i, 0))],
                out_specs=[pl.BlockSpec((rows // steps, cols), lambda i: (i, 0))],
                out_shape=[jax.ShapeDtypeStruct((rows, cols), BF)])


def _ffn_stage(x, mod, norm_g, w1, w2, *, layer, which, seq_tokens, pos=None, final_g=None):
    n = x.shape[0]
    tm = FFN_TM
    in_specs = [pl.BlockSpec((tm, D_MODEL), lambda i: (i, 0))]
    args = [x]
    if pos is not None:
        pos_blocks = pos.shape[0] // tm
        in_specs.append(pl.BlockSpec((tm, D_MODEL), lambda i: (i % pos_blocks, 0)))
        args.append(pos)
    in_specs += [
        _mod_spec(mod.shape[0], seq_tokens // tm),
        _const_spec((None, 1, D_MODEL), (3 * layer + 2 * which, 0, 0)),
        _const_spec((D_MODEL, D_FF), (0, 0)),
        _const_spec((D_MODEL, D_FF), (0, 1)),
        _const_spec((D_FF, D_MODEL), (0, 0)),
    ]
    args += [mod, norm_g, w1, w1, w2]
    if final_g is not None:
        in_specs.append(_const_spec((1, D_MODEL), (0, 0)))
        args.append(final_g.reshape(1, D_MODEL))
    kern = functools.partial(_ffn_kernel, mod_base=6 * which, add_pos=pos is not None,
                             final=final_g is not None)
    return dict(kernel=kern, grid=(n // tm,), in_specs=in_specs, args=args,
                out_specs=[pl.BlockSpec((tm, D_MODEL), lambda i: (i, 0))],
                out_shape=[jax.ShapeDtypeStruct((n, D_MODEL), F32)])


def _mix_in_kernel(x_ref, mod_ref, g_ref, winb_ref, winc_ref, cw_ref, cb_ref, clg_ref, clb_ref,
                   sglg_ref, sglb_ref, sgw_ref, sgb_ref, yb_ref, yc_ref, pad_ref, conv_ref, *, seq_len):
    n_seq = TOK_BLOCK // seq_len
    x = x_ref[...]
    h = _mod_norm(x, g_ref[...], mod_ref[0, 4:5, :], mod_ref[0, 3:4, :]).astype(BF)

    zb = _dot(h, winb_ref[...])
    gl = zb[:, :CONV_WIDTH] * jax.nn.sigmoid(zb[:, CONV_WIDTH:])
    zeros = jnp.zeros((CONV_PAD, CONV_WIDTH), F32)
    shifted_rows = seq_len + CONV_SHIFT_ROWS
    padded_rows = seq_len + 2 * CONV_PAD
    for s in range(n_seq):
        padded = jnp.concatenate([zeros, gl[s * seq_len:(s + 1) * seq_len, :], zeros], axis=0)
        pad_ref[0, s] = padded[:shifted_rows, :]
        for r in range(1, SUBLANES):
            pad_ref[r, s] = pltpu.roll(padded, padded_rows - r, 0)[:shifted_rows, :]
    first = CONV_PAD - CONV_K // 2
    tiles = CONV_RC // SUBLANES
    n_q = CONV_SHIFT_ROWS // SUBLANES + 1
    for s in range(n_seq):
        def conv_rows(r, carry, s=s):
            r0 = pl.multiple_of(r * CONV_RC, CONV_RC)
            accs = [None] * n_q
            for shift in range(SUBLANES):
                span = pad_ref[shift, s, pl.ds(r0, CONV_RC + CONV_SHIFT_ROWS), :]
                span = span.reshape(tiles + n_q - 1, SUBLANES, CONV_WIDTH)
                for q in range(n_q):
                    k = SUBLANES * q + shift - first
                    if 0 <= k < CONV_K:
                        term = cw_ref[k] * span[q:q + tiles]
                        accs[q] = term if accs[q] is None else accs[q] + term
            acc = (accs[0] + accs[1]) + (accs[2] + accs[3])
            conv_ref[pl.ds(pl.multiple_of(s * seq_len + r0, CONV_RC), CONV_RC), :] = acc.reshape(CONV_RC, CONV_WIDTH)
            return carry
        lax.fori_loop(0, seq_len // CONV_RC, conv_rows, 0)
    y = _layernorm(conv_ref[...] + cb_ref[...], clg_ref[...], clb_ref[...])
    yb_ref[...] = (y * jax.nn.sigmoid(y)).astype(BF)

    zc = jax.nn.gelu(_dot(h, winc_ref[...]))
    u = zc[:, :SG_WIDTH]
    v = _layernorm(zc[:, SG_WIDTH:], sglg_ref[...], sglb_ref[...]).astype(BF)
    head = lax.broadcasted_iota(jnp.int32, (SG_CHUNK, SG_WIDTH), 1) // SG_HEAD_DIM
    zero = jnp.zeros((SG_CHUNK, SG_WIDTH), BF)
    for n in range(TOK_BLOCK // SG_CHUNK):
        vn = v[n * SG_CHUNK:(n + 1) * SG_CHUNK, :]
        vcat = jnp.concatenate([jnp.where(head == hh, vn, zero) for hh in range(SG_HEADS)], axis=0)
        sgate = _dot(sgw_ref[...], vcat) + sgb_ref[...]
        yc_ref[n * SG_CHUNK:(n + 1) * SG_CHUNK, :] = (u[n * SG_CHUNK:(n + 1) * SG_CHUNK, :] * sgate).astype(BF)


def _mix_in(x, mod, norm_g, w_in, conv_w, conv_b, conv_ln_g, conv_ln_b, sg_ln_g, sg_ln_b, sgw_cat, sgb_full,
            *, layer, seq_len):
    n = x.shape[0]
    n_seq = TOK_BLOCK // seq_len
    vec = lambda width: _const_spec((None, 1, width), (layer, 0, 0))
    kern = functools.partial(_mix_in_kernel, seq_len=seq_len)
    return pl.pallas_call(
        kern,
        grid=(n // TOK_BLOCK,),
        in_specs=[
            pl.BlockSpec((TOK_BLOCK, D_MODEL), lambda i: (i, 0)),
            _mod_spec(mod.shape[0], max(seq_len // TOK_BLOCK, 1)),
            _const_spec((None, 1, D_MODEL), (3 * layer + 1, 0, 0)),
            _const_spec((None, D_MODEL, BRANCH_COLS), (layer, 0, S5_WIDTH // BRANCH_COLS)),
            _const_spec((None, D_MODEL, BRANCH_COLS), (layer, 0, S5_WIDTH // BRANCH_COLS + 1)),
            _const_spec((None, CONV_K, SUBLANES, CONV_WIDTH), (layer, 0, 0, 0)),
            vec(CONV_WIDTH), vec(CONV_WIDTH), vec(CONV_WIDTH), vec(SG_WIDTH), vec(SG_WIDTH),
            _const_spec((None, SG_CHUNK, SG_HEADS * SG_CHUNK), (layer, 0, 0)),
            _const_spec((None, SG_CHUNK, SG_WIDTH), (layer, 0, 0)),
        ],
        out_specs=[
            pl.BlockSpec((TOK_BLOCK, CONV_WIDTH), lambda i: (i, 0)),
            pl.BlockSpec((TOK_BLOCK, SG_WIDTH), lambda i: (i, 0)),
        ],
        out_shape=[
            jax.ShapeDtypeStruct((n, CONV_WIDTH), BF),
            jax.ShapeDtypeStruct((n, SG_WIDTH), BF),
        ],
        scratch_shapes=[pltpu.VMEM((SUBLANES, n_seq, seq_len + CONV_SHIFT_ROWS, CONV_WIDTH), F32),
                        pltpu.VMEM((TOK_BLOCK, CONV_WIDTH), F32)],
        compiler_params=_params(1),
        name="mix_in",
    )(x, mod, norm_g, w_in, w_in, conv_w, conv_b, conv_ln_g, conv_ln_b, sg_ln_g, sg_ln_b, sgw_cat, sgb_full)


def _s5_prep_kernel(arow_re_ref, arow_im_ref, dtrow_ref, b4_re_ref, b4_im_ref, c4_re_ref, c4_im_ref,
                    mt_ref, win_ref, woutt_ref, at_re_ref, at_im_ref):
    T, H = S5_T, S5_GROUP
    lanes = 2 * S5_CL
    lane = lax.broadcasted_iota(jnp.int32, (1, S5_NS), 1)
    lane_im = lane >= 2 * S5_STATE
    lane_bwd = (lane // S5_STATE) % 2 == 1
    e_col = lax.broadcasted_iota(jnp.int32, (S5_POW_ROWS, 1), 0).astype(F32)
    zeros = jnp.zeros((H, S5_NS), F32)
    for g in range(S5_PREP_GB):
        a_re, a_im = arow_re_ref[g], arow_im_ref[g]
        dt = jnp.exp(dtrow_ref[g])
        mag = jnp.exp(a_re * dt * e_col)
        ang = a_im * dt * e_col
        pw_re, pw_im = mag * jnp.cos(ang), mag * jnp.sin(ang)
        den = a_re * a_re + a_im * a_im
        nr, ni = pw_re[1:2, :] - 1.0, pw_im[1:2, :]
        q_re = (nr * a_re + ni * a_im) / den
        q_im = (ni * a_re - nr * a_im) / den
        b_re, b_im = b4_re_ref[g], b4_im_ref[g]
        bb_re = q_re * b_re - q_im * b_im
        bb_im = q_re * b_im + q_im * b_re
        c_re, c_im = c4_re_ref[g], c4_im_ref[g]

        def power(e_f, e_b, pw_re=pw_re, pw_im=pw_im):
            return (jnp.where(lane_bwd, pw_re[e_b:e_b + 1, :], pw_re[e_f:e_f + 1, :]),
                    jnp.where(lane_bwd, pw_im[e_b:e_b + 1, :], pw_im[e_f:e_f + 1, :]))

        def times_b(e_f, e_b, bb_re=bb_re, bb_im=bb_im, power=power):
            p_re, p_im = power(e_f, e_b)
            return jnp.where(lane_im, p_re * bb_im + p_im * bb_re, p_re * bb_re - p_im * bb_im)

        def times_c(e_f, e_b, c_re=c_re, c_im=c_im, power=power):
            p_re, p_im = power(e_f, e_b)
            return jnp.where(lane_im, -(c_re * p_im + c_im * p_re), c_re * p_re - c_im * p_im)

        win_ref[g] = jnp.concatenate([times_b(T - 1 - j, j) for j in range(T)], axis=0).astype(BF)
        woutt_ref[g] = jnp.concatenate([times_c(i + 1, T - i) for i in range(T)], axis=0).astype(BF)
        blocks = []
        for b in range(2 * T - 1):
            m = T - 1 - b
            w = times_b(abs(m), abs(m))
            if m > 0:
                w = jnp.where(lane_bwd, 0.0, w)
            elif m < 0:
                w = jnp.where(lane_bwd, w, 0.0)
            blocks.append(w)
        ystack = jnp.concatenate(blocks + [zeros], axis=0)
        c_signed = jnp.where(lane_im, -c_im, c_re)
        krev = lax.dot_general(c_signed, ystack, (((1,), (1,)), ((), ())), preferred_element_type=F32,
                               precision=HIGHEST)
        mt_ref[g] = jnp.concatenate(
            [pltpu.roll(krev, (lanes - H * (T - 1 - i)) % lanes, 1)[:, :S5_CL] for i in range(T)],
            axis=0).astype(BF)
        at_re_ref[g] = pw_re[T:T + 1, :]
        at_im_ref[g] = pw_im[T:T + 1, :]


S5_OPS = ("mt", "win", "woutt", "at_re", "at_im")


def _s5_prep_stage(a_re, a_im, log_dt, b_re, b_im, c_re, c_im):
    G, P, H = S5_GROUPS, S5_STATE, S5_GROUP
    n = DEPTH * G
    rep = lambda a: jnp.broadcast_to(a.transpose(0, 2, 1, 3)[:, :, None, :, :],
                                     (DEPTH, G, 2, 2, P)).reshape(n, 1, S5_NS)
    dt4 = rep(jnp.broadcast_to(log_dt[..., None], (DEPTH, 2, G, P)))
    c4 = lambda c: jnp.broadcast_to(c.transpose(0, 2, 3, 1, 4)[:, :, :, None, :, :],
                                    (DEPTH, G, H, 2, 2, P)).reshape(n, H, S5_NS)
    b4 = lambda b: c4(b.transpose(0, 1, 2, 4, 3))
    blk = lambda r, c: pl.BlockSpec((S5_PREP_GB, r, c), lambda i: (i, 0, 0))
    op = jax.ShapeDtypeStruct((n, S5_CL, S5_CL), BF)
    row = jax.ShapeDtypeStruct((n, 1, S5_NS), F32)
    return dict(
        kernel=_s5_prep_kernel, grid=(n // S5_PREP_GB,),
        in_specs=[blk(1, S5_NS), blk(1, S5_NS), blk(1, S5_NS), blk(H, S5_NS), blk(H, S5_NS), blk(H, S5_NS),
                  blk(H, S5_NS)],
        args=[rep(a_re), rep(a_im), dt4, b4(b_re), b4(b_im), c4(c_re), c4(c_im)],
        out_specs=[blk(S5_CL, S5_CL), blk(S5_CL, S5_NS), blk(S5_CL, S5_NS), blk(1, S5_NS), blk(1, S5_NS)],
        out_shape=[op, op, op, row, row])


def _token_slab_copies(hbm_ref, buf_ref, sem_ref, block, slot, to_hbm):
    copies = []
    for j in range(S5_T):
        hbm = hbm_ref.at[pl.ds(block * S5_CB, S5_CB), j, :]
        vmem = buf_ref.at[slot, j]
        src, dst = (vmem, hbm) if to_hbm else (hbm, vmem)
        copies.append(pltpu.make_async_copy(src, dst, sem_ref.at[slot, j]))
    return copies


def _s5_in_kernel(x_hbm, mod_ref, g_ref, wt_ref, zt_ref, xbuf, sem, *, mod_rows):
    g = g_ref[...]
    rows = S5_CB // mod_rows
    i = pl.program_id(0)
    slot = i % 2
    fetch = functools.partial(_token_slab_copies, x_hbm, xbuf, sem, to_hbm=False)

    @pl.when(i == 0)
    def _():
        for c in fetch(0, 0):
            c.start()

    @pl.when(i + 1 < pl.num_programs(0))
    def _():
        for c in fetch(i + 1, 1 - slot):
            c.start()

    for c in fetch(i, slot):
        c.wait()

    def normed(j):
        x = xbuf[slot, j]
        parts = [_mod_norm(x[m * rows:(m + 1) * rows, :], g, mod_ref[m, 4:5, :], mod_ref[m, 3:4, :])
                 for m in range(mod_rows)]
        return parts[0] if mod_rows == 1 else jnp.concatenate(parts, axis=0)

    for j in range(0, S5_T, 2):
        h = jnp.concatenate([normed(j), normed(j + 1)], axis=0).astype(BF)
        zt = lax.dot_general(wt_ref[...], h, (((1,), (1,)), ((), ())), preferred_element_type=F32)
        zt_ref[j] = zt[:, :S5_CB]
        zt_ref[j + 1] = zt[:, S5_CB:]


def _s5_in(x, mod, norm_g, w_in_t, *, layer, seq_chunks):
    nc = x.shape[0] // S5_T
    xv = x.reshape(nc, S5_T, D_MODEL)
    mod_rows = 1 if mod.shape[0] == 1 else S5_CB // seq_chunks
    in_specs = [
        pl.BlockSpec(memory_space=pl.ANY),
        _mod_spec(mod.shape[0], 1, rows=mod_rows),
        _const_spec((None, 1, D_MODEL), (3 * layer + 1, 0, 0)),
        _const_spec((None, S5_WIDTH, D_MODEL), (layer, 0, 0)),
    ]
    return pl.pallas_call(
        functools.partial(_s5_in_kernel, mod_rows=mod_rows),
        grid=(nc // S5_CB,),
        in_specs=in_specs,
        out_specs=pl.BlockSpec((S5_T, S5_WIDTH, S5_CB), lambda i: (0, 0, i)),
        out_shape=jax.ShapeDtypeStruct((S5_T, S5_WIDTH, nc), F32),
        scratch_shapes=[pltpu.VMEM((2, S5_T, S5_CB, D_MODEL), F32), pltpu.SemaphoreType.DMA((2, S5_T))],
        compiler_params=_params(1),
        name="s5_in",
    )(xv, mod, norm_g, w_in_t)


def _s5_core_kernel(*refs, groups, seq_chunks, has_h0, want_final):
    refs = list(refs)
    zt_ref, mt_ref, win_ref, woutt_ref, at_re_ref, at_im_ref = refs[:6]
    refs = refs[6:]
    h0_ref = refs.pop(0) if has_h0 else None
    yt_ref = refs.pop(0)
    hf_ref = refs.pop(0) if want_final else None
    T, K = S5_T, seq_chunks
    half = S5_NS // 2
    nc = zt_ref.shape[2]
    n_seq = nc // K
    n_steps = K.bit_length() - 1
    fwd = lax.broadcasted_iota(jnp.int32, (nc, half), 1) < S5_STATE
    bwd = jnp.logical_not(fwd)
    kpos = lax.broadcasted_iota(jnp.int32, (nc, half), 0) % K
    first = (fwd & (kpos == 0)) | (bwd & (kpos == K - 1))
    valid = [(fwd & (kpos >= (1 << t))) | (bwd & (kpos < K - (1 << t))) for t in range(n_steps)]
    shift = lambda x, s: jnp.where(fwd, pltpu.roll(x, s, 0), pltpu.roll(x, nc - s, 0))
    if has_h0:
        chunk = lax.broadcasted_iota(jnp.int32, (nc, n_seq), 0)
        seq = lax.broadcasted_iota(jnp.int32, (nc, n_seq), 1)
        place_f = (chunk == seq * K).astype(F32)
        place_b = (chunk == seq * K + (K - 1)).astype(F32)
        fwd_full = jnp.concatenate([fwd, fwd], axis=1)
    if want_final:
        seq = lax.broadcasted_iota(jnp.int32, (LANES, nc), 0)
        chunk = lax.broadcasted_iota(jnp.int32, (LANES, nc), 1)
        pick_f = (chunk == seq * K + (K - 1)).astype(F32)
        pick_b = (chunk == seq * K).astype(F32)
        fwd_row = lax.broadcasted_iota(jnp.int32, (1, S5_NS), 1) % half < S5_STATE

    xts, vs, h0s = [], [], []
    for g in range(groups):
        r0 = g * S5_GROUP
        xt = jnp.concatenate([zt_ref[j, r0:r0 + S5_GROUP, :] for j in range(T)], axis=0).astype(BF)
        xts.append(xt)
        vs.append(lax.dot_general(xt, win_ref[g], (((0,), (0,)), ((), ())), preferred_element_type=F32))
        if has_h0:
            h0s.append(jnp.where(fwd_full, _dot_exact(place_f, h0_ref[g]), _dot_exact(place_b, h0_ref[g])))
    yield
    s_ins, fins = [], []
    for g in range(groups):
        vr, vi = vs[g][:, :half], vs[g][:, half:]
        cr, ci = at_re_ref[g, :, :half], at_im_ref[g, :, :half]
        ar, ai = cr, ci
        hr, hi = (h0s[g][:, :half], h0s[g][:, half:]) if has_h0 else (0.0, 0.0)
        sr = jnp.where(first, hr, shift(vr, 1))
        si = jnp.where(first, hi, shift(vi, 1))
        for t in range(n_steps):
            rr = jnp.where(valid[t], shift(sr, 1 << t), 0.0)
            ri = jnp.where(valid[t], shift(si, 1 << t), 0.0)
            sr, si = sr + (cr * rr - ci * ri), si + (cr * ri + ci * rr)
            cr, ci = cr * cr - ci * ci, 2.0 * (cr * ci)
        s_ins.append(jnp.concatenate([sr, si], axis=1).astype(BF))
        if want_final:
            fins.append(jnp.concatenate([ar * sr - ai * si + vr, ar * si + ai * sr + vi], axis=1))
    yield
    for g in range(groups):
        r0 = g * S5_GROUP
        y = _dot(mt_ref[g], xts[g]) + lax.dot_general(woutt_ref[g], s_ins[g], (((1,), (1,)), ((), ())),
                                                      preferred_element_type=F32)
        for i in range(T):
            yt_ref[i, r0:r0 + S5_GROUP, :] = y[i * S5_GROUP:(i + 1) * S5_GROUP, :]
        if want_final:
            hf_ref[g] = jnp.where(fwd_row, _dot_exact(pick_f, fins[g]), _dot_exact(pick_b, fins[g]))


def _s5_core_stage(zt, ops, *, layer, seq_chunks, steps, h0=None, want_final=False):
    nc = zt.shape[2]
    gb = S5_GROUPS // steps
    rows = gb * S5_GROUP
    gspec = lambda r, c: pl.BlockSpec((gb, r, c), lambda i: (layer * steps + i, 0, 0))
    slab = pl.BlockSpec((S5_T, rows, nc), lambda i: (0, i, 0))
    in_specs = [slab, gspec(S5_CL, S5_CL), gspec(S5_CL, S5_NS), gspec(S5_CL, S5_NS),
                gspec(1, S5_NS), gspec(1, S5_NS)]
    args = [zt, ops["mt"], ops["win"], ops["woutt"], ops["at_re"], ops["at_im"]]
    if h0 is not None:
        in_specs.append(pl.BlockSpec((gb, h0.shape[1], S5_NS), lambda i: (i, 0, 0)))
        args.append(h0)
    out_specs = [slab]
    out_shape = [jax.ShapeDtypeStruct(zt.shape, F32)]
    if want_final:
        assert nc // seq_chunks <= LANES
        out_specs.append(pl.BlockSpec((gb, LANES, S5_NS), lambda i: (i, 0, 0)))
        out_shape.append(jax.ShapeDtypeStruct((S5_GROUPS, LANES, S5_NS), F32))
    kern = functools.partial(_s5_core_kernel, groups=gb, seq_chunks=seq_chunks, has_h0=h0 is not None,
                             want_final=want_final)
    return dict(kernel=kern, grid=(steps,), in_specs=in_specs, args=args, out_specs=out_specs, out_shape=out_shape)


def _s5_out_kernel(yt_ref, zt_ref, d_ref, wglut_ref, wa_ref, o_hbm, obuf, sem):
    i = pl.program_id(0)
    last = pl.num_programs(0) - 1
    slot = i % 2
    store = functools.partial(_token_slab_copies, o_hbm, obuf, sem, to_hbm=True)

    @pl.when(i >= 2)
    def _():
        for c in store(i - 2, slot):
            c.wait()

    d = d_ref[...]
    for j in range(0, S5_T, 2):
        pre = jnp.concatenate([yt_ref[j] + d * zt_ref[j], yt_ref[j + 1] + d * zt_ref[j + 1]], axis=1)
        ya = jax.nn.gelu(pre)
        ya = ya * jax.nn.sigmoid(_dot(wglut_ref[...], ya.astype(BF)))
        pa = lax.dot_general(ya.astype(BF), wa_ref[...], (((0,), (0,)), ((), ())), preferred_element_type=F32)
        obuf[slot, j] = pa[:S5_CB, :]
        obuf[slot, j + 1] = pa[S5_CB:, :]
    for c in store(i, slot):
        c.start()

    @pl.when(i == last)
    def _():
        for c in store(i, slot):
            c.wait()

    @pl.when((i == last) & (i >= 1))
    def _():
        for c in store(i - 1, 1 - slot):
            c.wait()


def _s5_out(yt, zt, s5_d_col, w_glu_t, w_br_a, *, layer):
    nc = yt.shape[2]
    slab = pl.BlockSpec((S5_T, S5_WIDTH, S5_CB), lambda i: (0, 0, i))
    out = pl.pallas_call(
        _s5_out_kernel,
        grid=(nc // S5_CB,),
        in_specs=[slab, slab,
                  _const_spec((None, S5_WIDTH, 1), (layer, 0, 0)),
                  _const_spec((None, S5_WIDTH, S5_WIDTH), (layer, 0, 0)),
                  _const_spec((None, S5_WIDTH, D_MODEL), (layer, 0, 0))],
        out_specs=pl.BlockSpec(memory_space=pl.ANY),
        out_shape=jax.ShapeDtypeStruct((nc, S5_T, D_MODEL), F32),
        scratch_shapes=[pltpu.VMEM((2, S5_T, S5_CB, D_MODEL), F32), pltpu.SemaphoreType.DMA((2, S5_T))],
        compiler_params=_params(1),
        name="s5_out",
    )(yt, zt, s5_d_col, w_glu_t, w_br_a)
    return out.reshape(nc * S5_T, D_MODEL)


def _mix_out_kernel(x_ref, mod_ref, g_ref, pa_ref, yb_ref, yc_ref, wb_ref, wc_ref, wgate_ref, bgate_ref,
                    wout_ref, o_ref):
    x = x_ref[...]
    h = _mod_norm(x, g_ref[...], mod_ref[0, 4:5, :], mod_ref[0, 3:4, :]).astype(BF)

    def gate(k):
        lo = k * D_MODEL
        return jax.nn.sigmoid(_dot(h, wgate_ref[:, lo:lo + D_MODEL]) + bgate_ref[:, lo:lo + D_MODEL])

    merged = gate(0) * pa_ref[...]
    merged = merged + gate(1) * _dot(yb_ref[...], wb_ref[...])
    merged = merged + gate(2) * _dot(yc_ref[...], wc_ref[...])
    y = _dot(merged.astype(BF), wout_ref[...])
    o_ref[...] = x + mod_ref[0, 5:6, :] * y


def _mix_out(x, mod, norm_g, pa, yb, yc, w_b, w_c, w_gate, b_gate, w_out, *, layer, seq_tokens):
    n = x.shape[0]
    tm = FFN_TM
    tok = lambda c: pl.BlockSpec((tm, c), lambda i: (i, 0))
    return pl.pallas_call(
        _mix_out_kernel,
        grid=(n // tm,),
        in_specs=[
            tok(D_MODEL),
            _mod_spec(mod.shape[0], seq_tokens // tm),
            _const_spec((None, 1, D_MODEL), (3 * layer + 1, 0, 0)),
            tok(D_MODEL), tok(CONV_WIDTH), tok(SG_WIDTH),
            _const_spec((None, CONV_WIDTH, D_MODEL), (layer, 0, 0)),
            _const_spec((None, SG_WIDTH, D_MODEL), (layer, 0, 0)),
            _const_spec((None, D_MODEL, 3 * D_MODEL), (layer, 0, 0)),
            _const_spec((None, 1, 3 * D_MODEL), (layer, 0, 0)),
            _const_spec((None, D_MODEL, D_MODEL), (layer, 0, 0)),
        ],
        out_specs=tok(D_MODEL),
        out_shape=jax.ShapeDtypeStruct((n, D_MODEL), F32),
        compiler_params=_params(1),
        name="mix_out",
    )(x, mod, norm_g, pa, yb, yc, w_b, w_c, w_gate, b_gate, w_out)


def _grid_pos_embed(n_tokens, dim):
    rows = n_tokens // GRID_W
    rr, cc = jnp.meshgrid(jnp.arange(rows, dtype=F32), jnp.arange(GRID_W, dtype=F32), indexing='ij')
    quarter = dim // 4
    omega = 1.0 / (10000.0 ** (jnp.arange(quarter, dtype=F32) / quarter))

    def emb(p):
        ang = p.reshape(-1)[:, None] * omega[None, :]
        return jnp.concatenate([jnp.sin(ang), jnp.cos(ang)], axis=-1)

    return jnp.concatenate([emb(rr), emb(cc)], axis=-1)


def kernel(x_prompt, x_sample, state_ssm, c, c_ctx, w_mod, b_mod, norm_g, ffn_w1, ffn_w2, w_in, w_gate, b_gate,
           s5_a_re, s5_a_im, s5_log_dt, s5_b_re, s5_b_im, s5_c_re, s5_c_im, s5_d, s5_w_glu, w_br_a, conv_w,
           conv_b, conv_ln_g, conv_ln_b, w_br_b, sg_ln_g, sg_ln_b, sg_w, sg_b, w_br_c, w_out, final_g):
    batch, seq, _ = x_prompt.shape
    dec_batch, dec_seq, _ = x_sample.shape
    assert (batch * seq) % TOK_BLOCK == 0 and TOK_BLOCK % seq == 0 and dec_seq % TOK_BLOCK == 0
    assert seq % SG_CHUNK == 0 and seq % S5_T == 0 and dec_seq % S5_T == 0
    ctx_chunks, smp_chunks = seq // S5_T, dec_seq // S5_T
    assert S5_CB % ctx_chunks == 0 and S5_CB % smp_chunks == 0
    assert ctx_chunks & (ctx_chunks - 1) == 0 and smp_chunks & (smp_chunks - 1) == 0
    assert 1 + dec_batch <= MOD_ROWS

    cond = jnp.zeros((MOD_ROWS, D_MODEL), F32).at[0].set(c_ctx).at[1:1 + dec_batch].set(c)
    mod_all = _modulation(cond, w_mod, b_mod).reshape(DEPTH, MOD_ROWS, N_MOD, D_MODEL)
    pos = _grid_pos_embed(dec_seq, D_MODEL)

    ffn_w = {(0, 0): (ffn_w1[0, 0].astype(BF), ffn_w2[0, 0].astype(BF))}
    w_in_b = w_in.astype(BF)
    w_in_t = w_in[:, :, :S5_WIDTH].transpose(0, 2, 1).astype(BF)
    w_glu_t = s5_w_glu.transpose(0, 2, 1).astype(BF)
    w_a, w_b, w_c = w_br_a.astype(BF), w_br_b.astype(BF), w_br_c.astype(BF)
    w_gate_b, w_out_b = w_gate.astype(BF), w_out.astype(BF)
    norm_rows = norm_g.reshape(DEPTH * 3, 1, D_MODEL)
    rows = lambda a: a.reshape(DEPTH, 1, -1)
    sgw_cat = sg_w.transpose(0, 2, 1, 3).reshape(DEPTH, SG_CHUNK, SG_HEADS * SG_CHUNK).astype(BF)
    sgb_full = jnp.repeat(sg_b.transpose(0, 2, 1), SG_HEAD_DIM, axis=2)
    s5_d_col = s5_d.reshape(DEPTH, S5_WIDTH, 1)

    conv_w8 = jnp.broadcast_to(conv_w[:, :, None, :], (DEPTH, CONV_K, SUBLANES, CONV_WIDTH))

    ctx = dict(n_seq=batch, seq_len=seq, mod_rows=slice(0, 1), mod_tokens=batch * seq)
    smp = dict(n_seq=dec_batch, seq_len=dec_seq, mod_rows=slice(1, 1 + dec_batch), mod_tokens=dec_seq)
    x_ctx = x_prompt.reshape(batch * seq, D_MODEL)
    x_smp = x_sample.reshape(dec_batch * dec_seq, D_MODEL)
    ffn_steps = x_ctx.shape[0] // FFN_TM
    assert x_smp.shape[0] // FFN_TM == ffn_steps and S5_GROUPS % ffn_steps == 0

    assert DEPTH * S5_GROUPS // S5_PREP_GB == ffn_steps

    def ffn_stage(gr, x, l, which, **kw):
        return _ffn_stage(x, mod_all[l, gr["mod_rows"]], norm_rows, *ffn_w[l, which], layer=l, which=which,
                          seq_tokens=gr["mod_tokens"], **kw)

    def cast_stages(l, which):
        if l >= DEPTH:
            return []
        return [_cast_stage(ffn_w1, (l, which), ffn_steps), _cast_stage(ffn_w2, (l, which), ffn_steps)]

    def keep_casts(l, which, outs):
        if outs:
            ffn_w[l, which] = (outs[0][0], outs[1][0])

    def branches_in(gr, x, l):
        mod = mod_all[l, gr["mod_rows"]]
        yb, yc = _mix_in(x, mod, norm_rows, w_in_b, conv_w8, rows(conv_b), rows(conv_ln_g), rows(conv_ln_b),
                         rows(sg_ln_g), rows(sg_ln_b), sgw_cat, sgb_full, layer=l, seq_len=gr["seq_len"])
        zt = _s5_in(x, mod, norm_rows, w_in_t, layer=l, seq_chunks=gr["seq_len"] // S5_T)
        return yb, yc, zt

    def s5_stage(gr, zt, l, **kw):
        return _s5_core_stage(zt, s5_ops, layer=l, seq_chunks=gr["seq_len"] // S5_T, steps=ffn_steps, **kw)

    def branches_out(gr, x, l, yb, yc, zt, yt):
        pa = _s5_out(yt, zt, s5_d_col, w_glu_t, w_a, layer=l)
        return _mix_out(x, mod_all[l, gr["mod_rows"]], norm_rows, pa, yb, yc, w_b, w_c, w_gate_b, rows(b_gate),
                        w_out_b, layer=l, seq_tokens=gr["mod_tokens"])

    ctx_states = []
    for l in range(DEPTH):
        riders = [_s5_prep_stage(s5_a_re, s5_a_im, s5_log_dt, s5_b_re, s5_b_im, s5_c_re, s5_c_im)] if l == 0 else []
        riders += cast_stages(0, 1) if l == 0 else []
        [x_ctx], *outs = _call([ffn_stage(ctx, x_ctx, l, 0)] + riders, "ffn")
        if l == 0:
            s5_ops = dict(zip(S5_OPS, outs[0]))
            keep_casts(0, 1, outs[1:])
        yb_c, yc_c, zt_c = branches_in(ctx, x_ctx, l)
        [x_smp], [yt_c, hf], *outs = _call([ffn_stage(smp, x_smp, l, 0, pos=pos if l == 0 else None),
                                             s5_stage(ctx, zt_c, l, want_final=True)] + cast_stages(l + 1, 0),
                                            "ffn_s5")
        keep_casts(l + 1, 0, outs)
        hf = hf[:, :batch, :].reshape(S5_GROUPS, batch, 2, 2, S5_STATE)
        ctx_states.append(hf.transpose(1, 3, 0, 4, 2))
        x_ctx = branches_out(ctx, x_ctx, l, yb_c, yc_c, zt_c, yt_c)
        yb_s, yc_s, zt_s = branches_in(smp, x_smp, l)
        h0 = state_ssm[:, l].transpose(2, 0, 4, 1, 3).reshape(S5_GROUPS, dec_batch, S5_NS)
        final = dict(final_g=final_g) if l == DEPTH - 1 else {}
        [x_ctx], [yt_s], *outs = _call([ffn_stage(ctx, x_ctx, l, 1, **final), s5_stage(smp, zt_s, l, h0=h0)]
                                       + cast_stages(l + 1, 1), "ffn_s5")
        keep_casts(l + 1, 1, outs)
        x_smp = branches_out(smp, x_smp, l, yb_s, yc_s, zt_s, yt_s)
        [x_smp], = _call([ffn_stage(smp, x_smp, l, 1, **final)], "ffn")
    y_prompt = x_ctx.reshape(batch, seq, D_MODEL)
    y_sample = x_smp.reshape(dec_batch, dec_seq, D_MODEL)
    new_state_ssm = jnp.stack(ctx_states, axis=1)
    return (y_prompt, y_sample, new_state_ssm)
```

```python
import functools
import inspect

import jax
import jax.numpy as jnp
from jax import lax
from jax.experimental import pallas as pl
from jax.experimental.pallas import tpu as pltpu

D_MODEL = 1024
DEPTH = 2
GRID_W = 64
D_FF = 2816
S5_WIDTH = 512
S5_GROUP = 16
S5_GROUPS = 32
S5_STATE = 64
CONV_WIDTH = 256
CONV_K = 31
SG_WIDTH = 256
SG_CHUNK = 128
SG_HEADS = 4
SG_HEAD_DIM = SG_WIDTH // SG_HEADS
BRANCH_COLS = 2 * CONV_WIDTH
IN_COLS = S5_WIDTH + 2 * BRANCH_COLS
N_MOD = 9
EPS = 1e-6

LANES = 128
SUBLANES = 8
S5_T = 16
S5_CL = S5_T * S5_GROUP
S5_NS = 4 * S5_STATE
S5_PREP_GB = 4
S5_POW_ROWS = SUBLANES * (S5_T // SUBLANES + 1)
S5_CB = 128
MOD_ROWS = 16
MOD_TN = 1152
TOK_BLOCK = 1024
FFN_TM = 512
CONV_RC = 32
CONV_PAD = 16
CONV_SHIFT_ROWS = SUBLANES * ((CONV_PAD - CONV_K // 2 + CONV_K - 1) // SUBLANES)
VMEM_LIMIT = 56 * 1024 * 1024

BF = jnp.bfloat16
F32 = jnp.float32
HIGHEST = lax.Precision.HIGHEST


def _dot(a, b):
    return jnp.dot(a, b, preferred_element_type=F32)


def _dot_exact(a, b):
    return jnp.dot(a, b, preferred_element_type=F32, precision=HIGHEST)


def _const_spec(block, index):
    return pl.BlockSpec(block, lambda *_: index, pipeline_mode=pl.Buffered(1))


def _params(n_axes=1):
    return pltpu.CompilerParams(dimension_semantics=("arbitrary",) * n_axes,
                                vmem_limit_bytes=VMEM_LIMIT)


def _call(stages, name):
    grid = stages[0]["grid"]
    assert all(s["grid"] == grid for s in stages)
    n_in = [len(s["in_specs"]) for s in stages]
    n_out = [len(s["out_specs"]) for s in stages]
    total_in, total_out = sum(n_in), sum(n_out)

    def body(*refs):
        bound, i, o = [], 0, 0
        for s, ni, no in zip(stages, n_in, n_out):
            bound.append((s["kernel"], refs[i:i + ni] + refs[total_in + o:total_in + o + no]))
            i, o = i + ni, o + no
        live = [run for run in (kern(*r) for kern, r in reversed(bound)) if inspect.isgenerator(run)]
        while live:
            for gen in list(live):
                if next(gen, "done") == "done":
                    live.remove(gen)

    outs = pl.pallas_call(
        body,
        grid=grid,
        in_specs=[spec for s in stages for spec in s["in_specs"]],
        out_specs=[spec for s in stages for spec in s["out_specs"]],
        out_shape=[shape for s in stages for shape in s["out_shape"]],
        compiler_params=_params(len(grid)),
        name=name,
    )(*[a for s in stages for a in s["args"]])
    split, o = [], 0
    for no in n_out:
        split.append(list(outs[o:o + no]))
        o += no
    return split


def _mod_norm(x, g, sc, sh):
    var = jnp.mean(x * x, axis=-1, keepdims=True)
    return (x * lax.rsqrt(var + EPS) * g) * (1.0 + sc) + sh


def _layernorm(x, g, b):
    mu = jnp.mean(x, axis=-1, keepdims=True)
    xc = x - mu
    var = jnp.mean(xc * xc, axis=-1, keepdims=True)
    return xc * lax.rsqrt(var + EPS) * g + b


def _mod_spec(n_mod, blocks_per_mod, rows=1):
    if n_mod == 1:
        return pl.BlockSpec((1, N_MOD, D_MODEL), lambda i: (0, 0, 0))
    return pl.BlockSpec((rows, N_MOD, D_MODEL), lambda i: (i // blocks_per_mod, 0, 0))


def _mod_kernel(c_ref, w_ref, b_ref, o_ref):
    c = c_ref[...]
    a = (c * jax.nn.sigmoid(c)).astype(BF)
    o_ref[...] = _dot(a, w_ref[...].astype(BF)) + b_ref[...]


def _modulation_stage(cond, w_mod, b_mod):
    n = N_MOD * D_MODEL
    tiles = n // MOD_TN
    tile = lambda i: (i // tiles, 0, i % tiles)
    return dict(
        kernel=_mod_kernel, grid=(DEPTH * tiles,), args=[cond, w_mod, b_mod.reshape(DEPTH, 1, n)],
        in_specs=[pl.BlockSpec((MOD_ROWS, D_MODEL), lambda i: (0, 0)),
                  pl.BlockSpec((None, D_MODEL, MOD_TN), tile),
                  pl.BlockSpec((None, 1, MOD_TN), tile)],
        out_specs=[pl.BlockSpec((None, MOD_ROWS, MOD_TN), tile)],
        out_shape=[jax.ShapeDtypeStruct((DEPTH, MOD_ROWS, n), F32)])


def _ffn_kernel(*refs, mod_base, add_pos, final):
    refs = list(refs)
    x_ref = refs.pop(0)
    pos_ref = refs.pop(0) if add_pos else None
    mod_ref, g_ref, w1g_ref, w1u_ref, w2_ref = refs[:5]
    refs = refs[5:]
    fg_ref = refs.pop(0) if final else None
    o_ref = refs.pop(0)

    x = x_ref[...]
    if add_pos:
        x = x + pos_ref[...]
    sh = mod_ref[0, mod_base:mod_base + 1, :]
    sc = mod_ref[0, mod_base + 1:mod_base + 2, :]
    gt = mod_ref[0, mod_base + 2:mod_base + 3, :]
    h = _mod_norm(x, g_ref[...], sc, sh).astype(BF)
    g = _dot(h, w1g_ref[...])
    yield
    u = _dot(h, w1u_ref[...])
    a = (g * jax.nn.sigmoid(g) * u).astype(BF)
    yield
    y = _dot(a, w2_ref[...])
    xn = x + (0.5 * gt) * y
    if final:
        var = jnp.mean(xn * xn, axis=-1, keepdims=True)
        xn = xn * lax.rsqrt(var + EPS) * fg_ref[...]
    o_ref[...] = xn
    yield


def _cast_kernel(x_ref, o_ref):
    o_ref[...] = x_ref[...].astype(o_ref.dtype)


def _cast_stage(w, index, steps):
    rows, cols = w.shape[-2:]
    lead = (None,) * len(index)
    return dict(kernel=_cast_kernel, grid=(steps,), args=[w],
                in_specs=[pl.BlockSpec(lead + (rows // steps, cols), lambda i: tuple(index) + (i, 0))],
                out_specs=[pl.BlockSpec((rows // steps, cols), lambda i: (i, 0))],
                out_shape=[jax.ShapeDtypeStruct((rows, cols), BF)])


def _ffn_stage(x, mod, norm_g, w1, w2, *, layer, which, seq_tokens, pos=None, final_g=None):
    n = x.shape[0]
    tm = FFN_TM
    in_specs = [pl.BlockSpec((tm, D_MODEL), lambda i: (i, 0))]
    args = [x]
    if pos is not None:
        pos_blocks = pos.shape[0] // tm
        in_specs.append(pl.BlockSpec((tm, D_MODEL), lambda i: (i % pos_blocks, 0)))
        args.append(pos)
    in_specs += [
        _mod_spec(mod.shape[0], seq_tokens // tm),
        _const_spec((None, 1, D_MODEL), (3 * layer + 2 * which, 0, 0)),
        _const_spec((D_MODEL, D_FF), (0, 0)),
        _const_spec((D_MODEL, D_FF), (0, 1)),
        _const_spec((D_FF, D_MODEL), (0, 0)),
    ]
    args += [mod, norm_g, w1, w1, w2]
    if final_g is not None:
        in_specs.append(_const_spec((1, D_MODEL), (0, 0)))
        args.append(final_g.reshape(1, D_MODEL))
    kern = functools.partial(_ffn_kernel, mod_base=6 * which, add_pos=pos is not None,
                             final=final_g is not None)
    return dict(kernel=kern, grid=(n // tm,), in_specs=in_specs, args=args,
                out_specs=[pl.BlockSpec((tm, D_MODEL), lambda i: (i, 0))],
                out_shape=[jax.ShapeDtypeStruct((n, D_MODEL), F32)])


def _mix_in_kernel(x_ref, mod_ref, g_ref, winb_ref, winc_ref, cw_ref, cb_ref, clg_ref, clb_ref,
                   sglg_ref, sglb_ref, sgw_ref, sgb_ref, yb_ref, yc_ref, pad_ref, conv_ref, *, seq_len):
    n_seq = TOK_BLOCK // seq_len
    x = x_ref[...]
    h = _mod_norm(x, g_ref[...], mod_ref[0, 4:5, :], mod_ref[0, 3:4, :]).astype(BF)

    zb = _dot(h, winb_ref[...])
    gl = zb[:, :CONV_WIDTH] * jax.nn.sigmoid(zb[:, CONV_WIDTH:])
    zeros = jnp.zeros((CONV_PAD, CONV_WIDTH), F32)
    shifted_rows = seq_len + CONV_SHIFT_ROWS
    padded_rows = seq_len + 2 * CONV_PAD
    for s in range(n_seq):
        padded = jnp.concatenate([zeros, gl[s * seq_len:(s + 1) * seq_len, :], zeros], axis=0)
        pad_ref[0, s] = padded[:shifted_rows, :]
        for r in range(1, SUBLANES):
            pad_ref[r, s] = pltpu.roll(padded, padded_rows - r, 0)[:shifted_rows, :]
    first = CONV_PAD - CONV_K // 2
    tiles = CONV_RC // SUBLANES
    n_q = CONV_SHIFT_ROWS // SUBLANES + 1
    for s in range(n_seq):
        def conv_rows(r, carry, s=s):
            r0 = pl.multiple_of(r * CONV_RC, CONV_RC)
            accs = [None] * n_q
            for shift in range(SUBLANES):
                span = pad_ref[shift, s, pl.ds(r0, CONV_RC + CONV_SHIFT_ROWS), :]
                span = span.reshape(tiles + n_q - 1, SUBLANES, CONV_WIDTH)
                for q in range(n_q):
                    k = SUBLANES * q + shift - first
                    if 0 <= k < CONV_K:
                        term = cw_ref[k] * span[q:q + tiles]
                        accs[q] = term if accs[q] is None else accs[q] + term
            acc = (accs[0] + accs[1]) + (accs[2] + accs[3])
            conv_ref[pl.ds(pl.multiple_of(s * seq_len + r0, CONV_RC), CONV_RC), :] = acc.reshape(CONV_RC, CONV_WIDTH)
            return carry
        lax.fori_loop(0, seq_len // CONV_RC, conv_rows, 0)
    y = _layernorm(conv_ref[...] + cb_ref[...], clg_ref[...], clb_ref[...])
    yb_ref[...] = (y * jax.nn.sigmoid(y)).astype(BF)

    zc = jax.nn.gelu(_dot(h, winc_ref[...]))
    u = zc[:, :SG_WIDTH]
    v = _layernorm(zc[:, SG_WIDTH:], sglg_ref[...], sglb_ref[...]).astype(BF)
    head = lax.broadcasted_iota(jnp.int32, (SG_CHUNK, SG_WIDTH), 1) // SG_HEAD_DIM
    zero = jnp.zeros((SG_CHUNK, SG_WIDTH), BF)
    for n in range(TOK_BLOCK // SG_CHUNK):
        vn = v[n * SG_CHUNK:(n + 1) * SG_CHUNK, :]
        vcat = jnp.concatenate([jnp.where(head == hh, vn, zero) for hh in range(SG_HEADS)], axis=0)
        sgate = _dot(sgw_ref[...], vcat) + sgb_ref[...]
        yc_ref[n * SG_CHUNK:(n + 1) * SG_CHUNK, :] = (u[n * SG_CHUNK:(n + 1) * SG_CHUNK, :] * sgate).astype(BF)


def _mix_in(x, mod, norm_g, w_in, conv_w, conv_b, conv_ln_g, conv_ln_b, sg_ln_g, sg_ln_b, sgw_cat, sgb_full,
            *, layer, seq_len):
    n = x.shape[0]
    n_seq = TOK_BLOCK // seq_len
    vec = lambda width: _const_spec((None, 1, width), (layer, 0, 0))
    kern = functools.partial(_mix_in_kernel, seq_len=seq_len)
    return pl.pallas_call(
        kern,
        grid=(n // TOK_BLOCK,),
        in_specs=[
            pl.BlockSpec((TOK_BLOCK, D_MODEL), lambda i: (i, 0)),
            _mod_spec(mod.shape[0], max(seq_len // TOK_BLOCK, 1)),
            _const_spec((None, 1, D_MODEL), (3 * layer + 1, 0, 0)),
            _const_spec((None, D_MODEL, BRANCH_COLS), (layer, 0, S5_WIDTH // BRANCH_COLS)),
            _const_spec((None, D_MODEL, BRANCH_COLS), (layer, 0, S5_WIDTH // BRANCH_COLS + 1)),
            _const_spec((None, CONV_K, SUBLANES, CONV_WIDTH), (layer, 0, 0, 0)),
            vec(CONV_WIDTH), vec(CONV_WIDTH), vec(CONV_WIDTH), vec(SG_WIDTH), vec(SG_WIDTH),
            _const_spec((None, SG_CHUNK, SG_HEADS * SG_CHUNK), (layer, 0, 0)),
            _const_spec((None, SG_CHUNK, SG_WIDTH), (layer, 0, 0)),
        ],
        out_specs=[
            pl.BlockSpec((TOK_BLOCK, CONV_WIDTH), lambda i: (i, 0)),
            pl.BlockSpec((TOK_BLOCK, SG_WIDTH), lambda i: (i, 0)),
        ],
        out_shape=[
            jax.ShapeDtypeStruct((n, CONV_WIDTH), BF),
            jax.ShapeDtypeStruct((n, SG_WIDTH), BF),
        ],
        scratch_shapes=[pltpu.VMEM((SUBLANES, n_seq, seq_len + CONV_SHIFT_ROWS, CONV_WIDTH), F32),
                        pltpu.VMEM((TOK_BLOCK, CONV_WIDTH), F32)],
        compiler_params=_params(1),
        name="mix_in",
    )(x, mod, norm_g, w_in, w_in, conv_w, conv_b, conv_ln_g, conv_ln_b, sg_ln_g, sg_ln_b, sgw_cat, sgb_full)


def _s5_prep_kernel(arow_re_ref, arow_im_ref, dtrow_ref, b4_re_ref, b4_im_ref, c4_re_ref, c4_im_ref,
                    mt_ref, win_ref, woutt_ref, at_re_ref, at_im_ref):
    T, H = S5_T, S5_GROUP
    lanes = 2 * S5_CL
    lane = lax.broadcasted_iota(jnp.int32, (1, S5_NS), 1)
    lane_im = lane >= 2 * S5_STATE
    lane_bwd = (lane // S5_STATE) % 2 == 1
    e_col = lax.broadcasted_iota(jnp.int32, (S5_POW_ROWS, 1), 0).astype(F32)
    zeros = jnp.zeros((H, S5_NS), F32)
    for g in range(S5_PREP_GB):
        a_re, a_im = arow_re_ref[g], arow_im_ref[g]
        dt = jnp.exp(dtrow_ref[g])
        mag = jnp.exp(a_re * dt * e_col)
        ang = a_im * dt * e_col
        pw_re, pw_im = mag * jnp.cos(ang), mag * jnp.sin(ang)
        den = a_re * a_re + a_im * a_im
        nr, ni = pw_re[1:2, :] - 1.0, pw_im[1:2, :]
        q_re = (nr * a_re + ni * a_im) / den
        q_im = (ni * a_re - nr * a_im) / den
        b_re, b_im = b4_re_ref[g], b4_im_ref[g]
        bb_re = q_re * b_re - q_im * b_im
        bb_im = q_re * b_im + q_im * b_re
        c_re, c_im = c4_re_ref[g], c4_im_ref[g]

        def power(e_f, e_b, pw_re=pw_re, pw_im=pw_im):
            return (jnp.where(lane_bwd, pw_re[e_b:e_b + 1, :], pw_re[e_f:e_f + 1, :]),
                    jnp.where(lane_bwd, pw_im[e_b:e_b + 1, :], pw_im[e_f:e_f + 1, :]))

        def times_b(e_f, e_b, bb_re=bb_re, bb_im=bb_im, power=power):
            p_re, p_im = power(e_f, e_b)
            return jnp.where(lane_im, p_re * bb_im + p_im * bb_re, p_re * bb_re - p_im * bb_im)

        def times_c(e_f, e_b, c_re=c_re, c_im=c_im, power=power):
            p_re, p_im = power(e_f, e_b)
            return jnp.where(lane_im, -(c_re * p_im + c_im * p_re), c_re * p_re - c_im * p_im)

        win_ref[g] = jnp.concatenate([times_b(T - 1 - j, j) for j in range(T)], axis=0).astype(BF)
        woutt_ref[g] = jnp.concatenate([times_c(i + 1, T - i) for i in range(T)], axis=0).astype(BF)
        blocks = []
        for b in range(2 * T - 1):
            m = T - 1 - b
            w = times_b(abs(m), abs(m))
            if m > 0:
                w = jnp.where(lane_bwd, 0.0, w)
            elif m < 0:
                w = jnp.where(lane_bwd, w, 0.0)
            blocks.append(w)
        ystack = jnp.concatenate(blocks + [zeros], axis=0)
        c_signed = jnp.where(lane_im, -c_im, c_re)
        krev = lax.dot_general(c_signed, ystack, (((1,), (1,)), ((), ())), preferred_element_type=F32,
                               precision=HIGHEST)
        mt_ref[g] = jnp.concatenate(
            [pltpu.roll(krev, (lanes - H * (T - 1 - i)) % lanes, 1)[:, :S5_CL] for i in range(T)],
            axis=0).astype(BF)
        at_re_ref[g] = pw_re[T:T + 1, :]
        at_im_ref[g] = pw_im[T:T + 1, :]


S5_OPS = ("mt", "win", "woutt", "at_re", "at_im")


def _s5_prep_stage(a_re, a_im, log_dt, b_re, b_im, c_re, c_im):
    G, P, H = S5_GROUPS, S5_STATE, S5_GROUP
    n = DEPTH * G
    rep = lambda a: jnp.broadcast_to(a.transpose(0, 2, 1, 3)[:, :, None, :, :],
                                     (DEPTH, G, 2, 2, P)).reshape(n, 1, S5_NS)
    dt4 = rep(jnp.broadcast_to(log_dt[..., None], (DEPTH, 2, G, P)))
    c4 = lambda c: jnp.broadcast_to(c.transpose(0, 2, 3, 1, 4)[:, :, :, None, :, :],
                                    (DEPTH, G, H, 2, 2, P)).reshape(n, H, S5_NS)
    b4 = lambda b: c4(b.transpose(0, 1, 2, 4, 3))
    blk = lambda r, c: pl.BlockSpec((S5_PREP_GB, r, c), lambda i: (i, 0, 0))
    op = jax.ShapeDtypeStruct((n, S5_CL, S5_CL), BF)
    row = jax.ShapeDtypeStruct((n, 1, S5_NS), F32)
    return dict(
        kernel=_s5_prep_kernel, grid=(n // S5_PREP_GB,),
        in_specs=[blk(1, S5_NS), blk(1, S5_NS), blk(1, S5_NS), blk(H, S5_NS), blk(H, S5_NS), blk(H, S5_NS),
                  blk(H, S5_NS)],
        args=[rep(a_re), rep(a_im), dt4, b4(b_re), b4(b_im), c4(c_re), c4(c_im)],
        out_specs=[blk(S5_CL, S5_CL), blk(S5_CL, S5_NS), blk(S5_CL, S5_NS), blk(1, S5_NS), blk(1, S5_NS)],
        out_shape=[op, op, op, row, row])


def _token_slab_copies(hbm_ref, buf_ref, sem_ref, block, slot, to_hbm):
    copies = []
    for j in range(S5_T):
        hbm = hbm_ref.at[pl.ds(block * S5_CB, S5_CB), j, :]
        vmem = buf_ref.at[slot, j]
        src, dst = (vmem, hbm) if to_hbm else (hbm, vmem)
        copies.append(pltpu.make_async_copy(src, dst, sem_ref.at[slot, j]))
    return copies


def _s5_in_kernel(x_hbm, mod_ref, g_ref, wt_ref, zt_ref, xbuf, sem, *, mod_rows):
    g = g_ref[...]
    rows = S5_CB // mod_rows
    i = pl.program_id(0)
    slot = i % 2
    fetch = functools.partial(_token_slab_copies, x_hbm, xbuf, sem, to_hbm=False)

    @pl.when(i == 0)
    def _():
        for c in fetch(0, 0):
            c.start()

    @pl.when(i + 1 < pl.num_programs(0))
    def _():
        for c in fetch(i + 1, 1 - slot):
            c.start()

    for c in fetch(i, slot):
        c.wait()

    def normed(j):
        x = xbuf[slot, j]
        parts = [_mod_norm(x[m * rows:(m + 1) * rows, :], g, mod_ref[m, 4:5, :], mod_ref[m, 3:4, :])
                 for m in range(mod_rows)]
        return parts[0] if mod_rows == 1 else jnp.concatenate(parts, axis=0)

    for j in range(0, S5_T, 2):
        h = jnp.concatenate([normed(j), normed(j + 1)], axis=0).astype(BF)
        zt = lax.dot_general(wt_ref[...], h, (((1,), (1,)), ((), ())), preferred_element_type=F32)
        zt_ref[j] = zt[:, :S5_CB]
        zt_ref[j + 1] = zt[:, S5_CB:]


def _s5_in(x, mod, norm_g, w_in_t, *, layer, seq_chunks):
    nc = x.shape[0] // S5_T
    xv = x.reshape(nc, S5_T, D_MODEL)
    mod_rows = 1 if mod.shape[0] == 1 else S5_CB // seq_chunks
    in_specs = [
        pl.BlockSpec(memory_space=pl.ANY),
        _mod_spec(mod.shape[0], 1, rows=mod_rows),
        _const_spec((None, 1, D_MODEL), (3 * layer + 1, 0, 0)),
        _const_spec((None, S5_WIDTH, D_MODEL), (layer, 0, 0)),
    ]
    return pl.pallas_call(
        functools.partial(_s5_in_kernel, mod_rows=mod_rows),
        grid=(nc // S5_CB,),
        in_specs=in_specs,
        out_specs=pl.BlockSpec((S5_T, S5_WIDTH, S5_CB), lambda i: (0, 0, i)),
        out_shape=jax.ShapeDtypeStruct((S5_T, S5_WIDTH, nc), F32),
        scratch_shapes=[pltpu.VMEM((2, S5_T, S5_CB, D_MODEL), F32), pltpu.SemaphoreType.DMA((2, S5_T))],
        compiler_params=_params(1),
        name="s5_in",
    )(xv, mod, norm_g, w_in_t)


def _s5_core_kernel(*refs, groups, seq_chunks, has_h0, want_final):
    refs = list(refs)
    zt_ref, mt_ref, win_ref, woutt_ref, at_re_ref, at_im_ref = refs[:6]
    refs = refs[6:]
    h0_ref = refs.pop(0) if has_h0 else None
    yt_ref = refs.pop(0)
    hf_ref = refs.pop(0) if want_final else None
    T, K = S5_T, seq_chunks
    half = S5_NS // 2
    nc = zt_ref.shape[2]
    n_seq = nc // K
    n_steps = K.bit_length() - 1
    fwd = lax.broadcasted_iota(jnp.int32, (nc, half), 1) < S5_STATE
    bwd = jnp.logical_not(fwd)
    kpos = lax.broadcasted_iota(jnp.int32, (nc, half), 0) % K
    first = (fwd & (kpos == 0)) | (bwd & (kpos == K - 1))
    valid = [(fwd & (kpos >= (1 << t))) | (bwd & (kpos < K - (1 << t))) for t in range(n_steps)]
    shift = lambda x, s: jnp.where(fwd, pltpu.roll(x, s, 0), pltpu.roll(x, nc - s, 0))
    if has_h0:
        chunk = lax.broadcasted_iota(jnp.int32, (nc, n_seq), 0)
        seq = lax.broadcasted_iota(jnp.int32, (nc, n_seq), 1)
        place_f = (chunk == seq * K).astype(F32)
        place_b = (chunk == seq * K + (K - 1)).astype(F32)
        fwd_full = jnp.concatenate([fwd, fwd], axis=1)
    if want_final:
        seq = lax.broadcasted_iota(jnp.int32, (LANES, nc), 0)
        chunk = lax.broadcasted_iota(jnp.int32, (LANES, nc), 1)
        pick_f = (chunk == seq * K + (K - 1)).astype(F32)
        pick_b = (chunk == seq * K).astype(F32)
        fwd_row = lax.broadcasted_iota(jnp.int32, (1, S5_NS), 1) % half < S5_STATE

    xts, vs, h0s = [], [], []
    for g in range(groups):
        r0 = g * S5_GROUP
        xt = jnp.concatenate([zt_ref[j, r0:r0 + S5_GROUP, :] for j in range(T)], axis=0).astype(BF)
        xts.append(xt)
        vs.append(lax.dot_general(xt, win_ref[g], (((0,), (0,)), ((), ())), preferred_element_type=F32))
        if has_h0:
            h0s.append(jnp.where(fwd_full, _dot_exact(place_f, h0_ref[g]), _dot_exact(place_b, h0_ref[g])))
    yield
    s_ins, fins = [], []
    for g in range(groups):
        vr, vi = vs[g][:, :half], vs[g][:, half:]
        cr, ci = at_re_ref[g, :, :half], at_im_ref[g, :, :half]
        ar, ai = cr, ci
        hr, hi = (h0s[g][:, :half], h0s[g][:, half:]) if has_h0 else (0.0, 0.0)
        sr = jnp.where(first, hr, shift(vr, 1))
        si = jnp.where(first, hi, shift(vi, 1))
        for t in range(n_steps):
            rr = jnp.where(valid[t], shift(sr, 1 << t), 0.0)
            ri = jnp.where(valid[t], shift(si, 1 << t), 0.0)
            sr, si = sr + (cr * rr - ci * ri), si + (cr * ri + ci * rr)
            cr, ci = cr * cr - ci * ci, 2.0 * (cr * ci)
        s_ins.append(jnp.concatenate([sr, si], axis=1).astype(BF))
        if want_final:
            fins.append(jnp.concatenate([ar * sr - ai * si + vr, ar * si + ai * sr + vi], axis=1))
    yield
    for g in range(groups):
        r0 = g * S5_GROUP
        y = _dot(mt_ref[g], xts[g]) + lax.dot_general(woutt_ref[g], s_ins[g], (((1,), (1,)), ((), ())),
                                                      preferred_element_type=F32)
        for i in range(T):
            yt_ref[i, r0:r0 + S5_GROUP, :] = y[i * S5_GROUP:(i + 1) * S5_GROUP, :]
        if want_final:
            hf_ref[g] = jnp.where(fwd_row, _dot_exact(pick_f, fins[g]), _dot_exact(pick_b, fins[g]))


def _s5_core_stage(zt, ops, *, layer, seq_chunks, steps, h0=None, want_final=False):
    nc = zt.shape[2]
    gb = S5_GROUPS // steps
    rows = gb * S5_GROUP
    gspec = lambda r, c: pl.BlockSpec((gb, r, c), lambda i: (layer * steps + i, 0, 0))
    slab = pl.BlockSpec((S5_T, rows, nc), lambda i: (0, i, 0))
    in_specs = [slab, gspec(S5_CL, S5_CL), gspec(S5_CL, S5_NS), gspec(S5_CL, S5_NS),
                gspec(1, S5_NS), gspec(1, S5_NS)]
    args = [zt, ops["mt"], ops["win"], ops["woutt"], ops["at_re"], ops["at_im"]]
    if h0 is not None:
        in_specs.append(pl.BlockSpec((gb, h0.shape[1], S5_NS), lambda i: (i, 0, 0)))
        args.append(h0)
    out_specs = [slab]
    out_shape = [jax.ShapeDtypeStruct(zt.shape, F32)]
    if want_final:
        assert nc // seq_chunks <= LANES
        out_specs.append(pl.BlockSpec((gb, LANES, S5_NS), lambda i: (i, 0, 0)))
        out_shape.append(jax.ShapeDtypeStruct((S5_GROUPS, LANES, S5_NS), F32))
    kern = functools.partial(_s5_core_kernel, groups=gb, seq_chunks=seq_chunks, has_h0=h0 is not None,
                             want_final=want_final)
    return dict(kernel=kern, grid=(steps,), in_specs=in_specs, args=args, out_specs=out_specs, out_shape=out_shape)


def _s5_out_kernel(yt_ref, zt_ref, d_ref, wglut_ref, wa_ref, o_hbm, obuf, sem):
    i = pl.program_id(0)
    last = pl.num_programs(0) - 1
    slot = i % 2
    store = functools.partial(_token_slab_copies, o_hbm, obuf, sem, to_hbm=True)

    @pl.when(i >= 2)
    def _():
        for c in store(i - 2, slot):
            c.wait()

    d = d_ref[...]
    for j in range(0, S5_T, 2):
        pre = jnp.concatenate([yt_ref[j] + d * zt_ref[j], yt_ref[j + 1] + d * zt_ref[j + 1]], axis=1)
        ya = jax.nn.gelu(pre)
        ya = ya * jax.nn.sigmoid(_dot(wglut_ref[...], ya.astype(BF)))
        pa = lax.dot_general(ya.astype(BF), wa_ref[...], (((0,), (0,)), ((), ())), preferred_element_type=F32)
        obuf[slot, j] = pa[:S5_CB, :]
        obuf[slot, j + 1] = pa[S5_CB:, :]
    for c in store(i, slot):
        c.start()

    @pl.when(i == last)
    def _():
        for c in store(i, slot):
            c.wait()

    @pl.when((i == last) & (i >= 1))
    def _():
        for c in store(i - 1, 1 - slot):
            c.wait()


def _s5_out(yt, zt, s5_d_col, w_glu_t, w_br_a, *, layer):
    nc = yt.shape[2]
    slab = pl.BlockSpec((S5_T, S5_WIDTH, S5_CB), lambda i: (0, 0, i))
    out = pl.pallas_call(
        _s5_out_kernel,
        grid=(nc // S5_CB,),
        in_specs=[slab, slab,
                  _const_spec((None, S5_WIDTH, 1), (layer, 0, 0)),
                  _const_spec((None, S5_WIDTH, S5_WIDTH), (layer, 0, 0)),
                  _const_spec((None, S5_WIDTH, D_MODEL), (layer, 0, 0))],
        out_specs=pl.BlockSpec(memory_space=pl.ANY),
        out_shape=jax.ShapeDtypeStruct((nc, S5_T, D_MODEL), F32),
        scratch_shapes=[pltpu.VMEM((2, S5_T, S5_CB, D_MODEL), F32), pltpu.SemaphoreType.DMA((2, S5_T))],
        compiler_params=_params(1),
        name="s5_out",
    )(yt, zt, s5_d_col, w_glu_t, w_br_a)
    return out.reshape(nc * S5_T, D_MODEL)


def _mix_out_kernel(x_ref, mod_ref, g_ref, pa_ref, yb_ref, yc_ref, wb_ref, wc_ref, wgate_ref, bgate_ref,
                    wout_ref, o_ref):
    x = x_ref[...]
    h = _mod_norm(x, g_ref[...], mod_ref[0, 4:5, :], mod_ref[0, 3:4, :]).astype(BF)

    def gate(k):
        lo = k * D_MODEL
        return jax.nn.sigmoid(_dot(h, wgate_ref[:, lo:lo + D_MODEL]) + bgate_ref[:, lo:lo + D_MODEL])

    merged = gate(0) * pa_ref[...]
    merged = merged + gate(1) * _dot(yb_ref[...], wb_ref[...])
    merged = merged + gate(2) * _dot(yc_ref[...], wc_ref[...])
    y = _dot(merged.astype(BF), wout_ref[...])
    o_ref[...] = x + mod_ref[0, 5:6, :] * y


def _mix_out(x, mod, norm_g, pa, yb, yc, w_b, w_c, w_gate, b_gate, w_out, *, layer, seq_tokens):
    n = x.shape[0]
    tm = FFN_TM
    tok = lambda c: pl.BlockSpec((tm, c), lambda i: (i, 0))
    return pl.pallas_call(
        _mix_out_kernel,
        grid=(n // tm,),
        in_specs=[
            tok(D_MODEL),
            _mod_spec(mod.shape[0], seq_tokens // tm),
            _const_spec((None, 1, D_MODEL), (3 * layer + 1, 0, 0)),
            tok(D_MODEL), tok(CONV_WIDTH), tok(SG_WIDTH),
            _const_spec((None, CONV_WIDTH, D_MODEL), (layer, 0, 0)),
            _const_spec((None, SG_WIDTH, D_MODEL), (layer, 0, 0)),
            _const_spec((None, D_MODEL, 3 * D_MODEL), (layer, 0, 0)),
            _const_spec((None, 1, 3 * D_MODEL), (layer, 0, 0)),
            _const_spec((None, D_MODEL, D_MODEL), (layer, 0, 0)),
        ],
        out_specs=tok(D_MODEL),
        out_shape=jax.ShapeDtypeStruct((n, D_MODEL), F32),
        compiler_params=_params(1),
        name="mix_out",
    )(x, mod, norm_g, pa, yb, yc, w_b, w_c, w_gate, b_gate, w_out)


def _grid_pos_embed(n_tokens, dim):
    rows = n_tokens // GRID_W
    rr, cc = jnp.meshgrid(jnp.arange(rows, dtype=F32), jnp.arange(GRID_W, dtype=F32), indexing='ij')
    quarter = dim // 4
    omega = 1.0 / (10000.0 ** (jnp.arange(quarter, dtype=F32) / quarter))

    def emb(p):
        ang = p.reshape(-1)[:, None] * omega[None, :]
        return jnp.concatenate([jnp.sin(ang), jnp.cos(ang)], axis=-1)

    return jnp.concatenate([emb(rr), emb(cc)], axis=-1)


def kernel(x_prompt, x_sample, state_ssm, c, c_ctx, w_mod, b_mod, norm_g, ffn_w1, ffn_w2, w_in, w_gate, b_gate,
           s5_a_re, s5_a_im, s5_log_dt, s5_b_re, s5_b_im, s5_c_re, s5_c_im, s5_d, s5_w_glu, w_br_a, conv_w,
           conv_b, conv_ln_g, conv_ln_b, w_br_b, sg_ln_g, sg_ln_b, sg_w, sg_b, w_br_c, w_out, final_g):
    batch, seq, _ = x_prompt.shape
    dec_batch, dec_seq, _ = x_sample.shape
    assert (batch * seq) % TOK_BLOCK == 0 and TOK_BLOCK % seq == 0 and dec_seq % TOK_BLOCK == 0
    assert seq % SG_CHUNK == 0 and seq % S5_T == 0 and dec_seq % S5_T == 0
    ctx_chunks, smp_chunks = seq // S5_T, dec_seq // S5_T
    assert S5_CB % ctx_chunks == 0 and S5_CB % smp_chunks == 0
    assert ctx_chunks & (ctx_chunks - 1) == 0 and smp_chunks & (smp_chunks - 1) == 0
    assert 1 + dec_batch <= MOD_ROWS

    cond = jnp.zeros((MOD_ROWS, D_MODEL), F32).at[0].set(c_ctx).at[1:1 + dec_batch].set(c)
    [mod_all], prep_out = _call([_modulation_stage(cond, w_mod, b_mod),
                                 _s5_prep_stage(s5_a_re, s5_a_im, s5_log_dt, s5_b_re, s5_b_im, s5_c_re, s5_c_im)],
                                "mod_prep")
    mod_all = mod_all.reshape(DEPTH, MOD_ROWS, N_MOD, D_MODEL)
    s5_ops = dict(zip(S5_OPS, prep_out))
    pos = _grid_pos_embed(dec_seq, D_MODEL)

    ffn_w = {(0, 0): (ffn_w1[0, 0].astype(BF), ffn_w2[0, 0].astype(BF))}
    mixer_w = dict(w_in=w_in, w_gate=w_gate, w_out=w_out, w_a=w_br_a, w_b=w_br_b, w_c=w_br_c)
    w_in_t = w_in[:, :, :S5_WIDTH].transpose(0, 2, 1).astype(BF)
    w_glu_t = s5_w_glu.transpose(0, 2, 1).astype(BF)
    norm_rows = norm_g.reshape(DEPTH * 3, 1, D_MODEL)
    rows = lambda a: a.reshape(DEPTH, 1, -1)
    sgw_cat = sg_w.transpose(0, 2, 1, 3).reshape(DEPTH, SG_CHUNK, SG_HEADS * SG_CHUNK).astype(BF)
    sgb_full = jnp.repeat(sg_b.transpose(0, 2, 1), SG_HEAD_DIM, axis=2)
    s5_d_col = s5_d.reshape(DEPTH, S5_WIDTH, 1)

    conv_w8 = jnp.broadcast_to(conv_w[:, :, None, :], (DEPTH, CONV_K, SUBLANES, CONV_WIDTH))

    ctx = dict(n_seq=batch, seq_len=seq, mod_rows=slice(0, 1), mod_tokens=batch * seq)
    smp = dict(n_seq=dec_batch, seq_len=dec_seq, mod_rows=slice(1, 1 + dec_batch), mod_tokens=dec_seq)
    x_ctx = x_prompt.reshape(batch * seq, D_MODEL)
    x_smp = x_sample.reshape(dec_batch * dec_seq, D_MODEL)
    ffn_steps = x_ctx.shape[0] // FFN_TM
    assert x_smp.shape[0] // FFN_TM == ffn_steps and S5_GROUPS % ffn_steps == 0

    assert DEPTH * S5_GROUPS // S5_PREP_GB == ffn_steps

    def ffn_stage(gr, x, l, which, **kw):
        return _ffn_stage(x, mod_all[l, gr["mod_rows"]], norm_rows, *ffn_w[l, which], layer=l, which=which,
                          seq_tokens=gr["mod_tokens"], **kw)

    def cast_stages(l, which):
        if l >= DEPTH:
            return []
        return [_cast_stage(ffn_w1, (l, which), ffn_steps), _cast_stage(ffn_w2, (l, which), ffn_steps)]

    def keep_casts(l, which, outs):
        if outs:
            ffn_w[l, which] = (outs[0][0], outs[1][0])

    def branches_in(gr, x, l):
        mod = mod_all[l, gr["mod_rows"]]
        yb, yc = _mix_in(x, mod, norm_rows, mixer_w["w_in"], conv_w8, rows(conv_b), rows(conv_ln_g), rows(conv_ln_b),
                         rows(sg_ln_g), rows(sg_ln_b), sgw_cat, sgb_full, layer=l, seq_len=gr["seq_len"])
        zt = _s5_in(x, mod, norm_rows, w_in_t, layer=l, seq_chunks=gr["seq_len"] // S5_T)
        return yb, yc, zt

    def s5_stage(gr, zt, l, **kw):
        return _s5_core_stage(zt, s5_ops, layer=l, seq_chunks=gr["seq_len"] // S5_T, steps=ffn_steps, **kw)

    def branches_out(gr, x, l, yb, yc, zt, yt):
        pa = _s5_out(yt, zt, s5_d_col, w_glu_t, mixer_w["w_a"], layer=l)
        return _mix_out(x, mod_all[l, gr["mod_rows"]], norm_rows, pa, yb, yc, mixer_w["w_b"], mixer_w["w_c"],
                        mixer_w["w_gate"], rows(b_gate), mixer_w["w_out"], layer=l, seq_tokens=gr["mod_tokens"])

    ctx_states = []
    for l in range(DEPTH):
        riders = []
        if l == 0:
            riders = cast_stages(0, 1) + [_cast_stage(w.reshape(-1, w.shape[-1]), (), ffn_steps)
                                          for w in mixer_w.values()]
        [x_ctx], *outs = _call([ffn_stage(ctx, x_ctx, l, 0)] + riders, "ffn")
        if l == 0:
            keep_casts(0, 1, outs[:2])
            mixer_w = {k: o[0].reshape(mixer_w[k].shape) for k, o in zip(mixer_w, outs[2:])}
        yb_c, yc_c, zt_c = branches_in(ctx, x_ctx, l)
        [x_smp], [yt_c, hf], *outs = _call([ffn_stage(smp, x_smp, l, 0, pos=pos if l == 0 else None),
                                             s5_stage(ctx, zt_c, l, want_final=True)] + cast_stages(l + 1, 0),
                                            "ffn_s5")
        keep_casts(l + 1, 0, outs)
        hf = hf[:, :batch, :].reshape(S5_GROUPS, batch, 2, 2, S5_STATE)
        ctx_states.append(hf.transpose(1, 3, 0, 4, 2))
        x_ctx = branches_out(ctx, x_ctx, l, yb_c, yc_c, zt_c, yt_c)
        yb_s, yc_s, zt_s = branches_in(smp, x_smp, l)
        h0 = state_ssm[:, l].transpose(2, 0, 4, 1, 3).reshape(S5_GROUPS, dec_batch, S5_NS)
        final = dict(final_g=final_g) if l == DEPTH - 1 else {}
        [x_ctx], [yt_s], *outs = _call([ffn_stage(ctx, x_ctx, l, 1, **final), s5_stage(smp, zt_s, l, h0=h0)]
                                       + cast_stages(l + 1, 1), "ffn_s5")
        keep_casts(l + 1, 1, outs)
        x_smp = branches_out(smp, x_smp, l, yb_s, yc_s, zt_s, yt_s)
        [x_smp], = _call([ffn_stage(smp, x_smp, l, 1, **final)], "ffn")
    y_prompt = x_ctx.reshape(batch, seq, D_MODEL)
    y_sample = x_smp.reshape(dec_batch, dec_seq, D_MODEL)
    new_state_ssm = jnp.stack(ctx_states, axis=1)
    return (y_prompt, y_sample, new_state_ssm)
```

```python
import functools
import inspect

import jax
import jax.numpy as jnp
from jax import lax
from jax.experimental import pallas as pl
from jax.experimental.pallas import tpu as pltpu

D_MODEL = 1024
DEPTH = 2
GRID_W = 64
D_FF = 2816
S5_WIDTH = 512
S5_GROUP = 16
S5_GROUPS = 32
S5_STATE = 64
CONV_WIDTH = 256
CONV_K = 31
SG_WIDTH = 256
SG_CHUNK = 128
SG_HEADS = 4
SG_HEAD_DIM = SG_WIDTH // SG_HEADS
BRANCH_COLS = 2 * CONV_WIDTH
IN_COLS = S5_WIDTH + 2 * BRANCH_COLS
N_MOD = 9
EPS = 1e-6

LANES = 128
SUBLANES = 8
S5_T = 16
S5_CL = S5_T * S5_GROUP
S5_NS = 4 * S5_STATE
S5_PREP_GB = 4
S5_POW_ROWS = SUBLANES * (S5_T // SUBLANES + 1)
S5_CB = 128
MOD_ROWS = 16
MOD_TN = 1152
TOK_BLOCK = 1024
FFN_TM = 512
CONV_RC = 32
CONV_PAD = 16
CONV_SHIFT_ROWS = SUBLANES * ((CONV_PAD - CONV_K // 2 + CONV_K - 1) // SUBLANES)
VMEM_LIMIT = 56 * 1024 * 1024

BF = jnp.bfloat16
F32 = jnp.float32
HIGHEST = lax.Precision.HIGHEST


def _dot(a, b):
    return jnp.dot(a, b, preferred_element_type=F32)


def _dot_exact(a, b):
    return jnp.dot(a, b, preferred_element_type=F32, precision=HIGHEST)


def _const_spec(block, index):
    return pl.BlockSpec(block, lambda *_: index, pipeline_mode=pl.Buffered(1))


def _params(n_axes=1):
    return pltpu.CompilerParams(dimension_semantics=("arbitrary",) * n_axes,
                                vmem_limit_bytes=VMEM_LIMIT)


def _call(stages, name):
    grid = stages[0]["grid"]
    assert all(s["grid"] == grid for s in stages)
    n_in = [len(s["in_specs"]) for s in stages]
    n_out = [len(s["out_specs"]) for s in stages]
    total_in, total_out = sum(n_in), sum(n_out)

    def body(*refs):
        bound, i, o = [], 0, 0
        for s, ni, no in zip(stages, n_in, n_out):
            bound.append((s["kernel"], refs[i:i + ni] + refs[total_in + o:total_in + o + no]))
            i, o = i + ni, o + no
        live = [run for run in (kern(*r) for kern, r in reversed(bound)) if inspect.isgenerator(run)]
        while live:
            for gen in list(live):
                if next(gen, "done") == "done":
                    live.remove(gen)

    outs = pl.pallas_call(
        body,
        grid=grid,
        in_specs=[spec for s in stages for spec in s["in_specs"]],
        out_specs=[spec for s in stages for spec in s["out_specs"]],
        out_shape=[shape for s in stages for shape in s["out_shape"]],
        compiler_params=_params(len(grid)),
        name=name,
    )(*[a for s in stages for a in s["args"]])
    split, o = [], 0
    for no in n_out:
        split.append(list(outs[o:o + no]))
        o += no
    return split


def _mod_norm(x, g, sc, sh):
    var = jnp.mean(x * x, axis=-1, keepdims=True)
    return (x * lax.rsqrt(var + EPS) * g) * (1.0 + sc) + sh


def _layernorm(x, g, b):
    mu = jnp.mean(x, axis=-1, keepdims=True)
    xc = x - mu
    var = jnp.mean(xc * xc, axis=-1, keepdims=True)
    return xc * lax.rsqrt(var + EPS) * g + b


def _mod_spec(n_mod, blocks_per_mod, rows=1):
    if n_mod == 1:
        return pl.BlockSpec((1, N_MOD, D_MODEL), lambda i: (0, 0, 0))
    return pl.BlockSpec((rows, N_MOD, D_MODEL), lambda i: (i // blocks_per_mod, 0, 0))


def _mod_kernel(c_ref, w_ref, b_ref, o_ref):
    c = c_ref[...]
    a = (c * jax.nn.sigmoid(c)).astype(BF)
    o_ref[...] = _dot(a, w_ref[...].astype(BF)) + b_ref[...]


def _modulation_stage(cond, w_mod, b_mod):
    n = N_MOD * D_MODEL
    tiles = n // MOD_TN
    tile = lambda i: (i // tiles, 0, i % tiles)
    return dict(
        kernel=_mod_kernel, grid=(DEPTH * tiles,), args=[cond, w_mod, b_mod.reshape(DEPTH, 1, n)],
        in_specs=[pl.BlockSpec((MOD_ROWS, D_MODEL), lambda i: (0, 0)),
                  pl.BlockSpec((None, D_MODEL, MOD_TN), tile),
                  pl.BlockSpec((None, 1, MOD_TN), tile)],
        out_specs=[pl.BlockSpec((None, MOD_ROWS, MOD_TN), tile)],
        out_shape=[jax.ShapeDtypeStruct((DEPTH, MOD_ROWS, n), F32)])


def _ffn_kernel(*refs, mod_base, add_pos, final):
    refs = list(refs)
    x_ref = refs.pop(0)
    pos_ref = refs.pop(0) if add_pos else None
    mod_ref, g_ref, w1g_ref, w1u_ref, w2_ref = refs[:5]
    refs = refs[5:]
    fg_ref = refs.pop(0) if final else None
    o_ref = refs.pop(0)

    x = x_ref[...]
    if add_pos:
        x = x + pos_ref[...]
    sh = mod_ref[0, mod_base:mod_base + 1, :]
    sc = mod_ref[0, mod_base + 1:mod_base + 2, :]
    gt = mod_ref[0, mod_base + 2:mod_base + 3, :]
    h = _mod_norm(x, g_ref[...], sc, sh).astype(BF)
    g = _dot(h, w1g_ref[...])
    yield
    u = _dot(h, w1u_ref[...])
    a = (g * jax.nn.sigmoid(g) * u).astype(BF)
    yield
    y = _dot(a, w2_ref[...])
    xn = x + (0.5 * gt) * y
    if final:
        var = jnp.mean(xn * xn, axis=-1, keepdims=True)
        xn = xn * lax.rsqrt(var + EPS) * fg_ref[...]
    o_ref[...] = xn
    yield


def _cast_kernel(x_ref, o_ref):
    o_ref[...] = x_ref[...].astype(o_ref.dtype)


def _cast_stage(w, index, steps):
    rows, cols = w.shape[-2:]
    lead = (None,) * len(index)
    return dict(kernel=_cast_kernel, grid=(steps,), args=[w],
                in_specs=[pl.BlockSpec(lead + (rows // steps, cols), lambda i: tuple(index) + (i, 0))],
                out_specs=[pl.BlockSpec((rows // steps, cols), lambda i: (i, 0))],
                out_shape=[jax.ShapeDtypeStruct((rows, cols), BF)])


def _ffn_stage(x, mod, norm_g, w1, w2, *, layer, which, seq_tokens, pos=None, final_g=None):
    n = x.shape[0]
    tm = FFN_TM
    in_specs = [pl.BlockSpec((tm, D_MODEL), lambda i: (i, 0))]
    args = [x]
    if pos is not None:
        pos_blocks = pos.shape[0] // tm
        in_specs.append(pl.BlockSpec((tm, D_MODEL), lambda i: (i % pos_blocks, 0)))
        args.append(pos)
    in_specs += [
        _mod_spec(mod.shape[0], seq_tokens // tm),
        _const_spec((None, 1, D_MODEL), (3 * layer + 2 * which, 0, 0)),
        _const_spec((D_MODEL, D_FF), (0, 0)),
        _const_spec((D_MODEL, D_FF), (0, 1)),
        _const_spec((D_FF, D_MODEL), (0, 0)),
    ]
    args += [mod, norm_g, w1, w1, w2]
    if final_g is not None:
        in_specs.append(_const_spec((1, D_MODEL), (0, 0)))
        args.append(final_g.reshape(1, D_MODEL))
    kern = functools.partial(_ffn_kernel, mod_base=6 * which, add_pos=pos is not None,
                             final=final_g is not None)
    return dict(kernel=kern, grid=(n // tm,), in_specs=in_specs, args=args,
                out_specs=[pl.BlockSpec((tm, D_MODEL), lambda i: (i, 0))],
                out_shape=[jax.ShapeDtypeStruct((n, D_MODEL), F32)])


def _mix_in_kernel(x_ref, mod_ref, g_ref, winb_ref, winc_ref, cw_ref, cb_ref, clg_ref, clb_ref,
                   sglg_ref, sglb_ref, sgw_ref, sgb_ref, yb_ref, yc_ref, pad_ref, conv_ref, *, seq_len):
    n_seq = TOK_BLOCK // seq_len
    x = x_ref[...]
    h = _mod_norm(x, g_ref[...], mod_ref[0, 4:5, :], mod_ref[0, 3:4, :]).astype(BF)

    zb = _dot(h, winb_ref[...])
    gl = zb[:, :CONV_WIDTH] * jax.nn.sigmoid(zb[:, CONV_WIDTH:])
    zeros = jnp.zeros((CONV_PAD, CONV_WIDTH), F32)
    shifted_rows = seq_len + CONV_SHIFT_ROWS
    padded_rows = seq_len + 2 * CONV_PAD
    for s in range(n_seq):
        padded = jnp.concatenate([zeros, gl[s * seq_len:(s + 1) * seq_len, :], zeros], axis=0)
        pad_ref[0, s] = padded[:shifted_rows, :]
        for r in range(1, SUBLANES):
            pad_ref[r, s] = pltpu.roll(padded, padded_rows - r, 0)[:shifted_rows, :]
    first = CONV_PAD - CONV_K // 2
    tiles = CONV_RC // SUBLANES
    n_q = CONV_SHIFT_ROWS // SUBLANES + 1
    for s in range(n_seq):
        def conv_rows(r, carry, s=s):
            r0 = pl.multiple_of(r * CONV_RC, CONV_RC)
            accs = [None] * n_q
            for shift in range(SUBLANES):
                span = pad_ref[shift, s, pl.ds(r0, CONV_RC + CONV_SHIFT_ROWS), :]
                span = span.reshape(tiles + n_q - 1, SUBLANES, CONV_WIDTH)
                for q in range(n_q):
                    k = SUBLANES * q + shift - first
                    if 0 <= k < CONV_K:
                        term = cw_ref[k] * span[q:q + tiles]
                        accs[q] = term if accs[q] is None else accs[q] + term
            acc = (accs[0] + accs[1]) + (accs[2] + accs[3])
            conv_ref[pl.ds(pl.multiple_of(s * seq_len + r0, CONV_RC), CONV_RC), :] = acc.reshape(CONV_RC, CONV_WIDTH)
            return carry
        lax.fori_loop(0, seq_len // CONV_RC, conv_rows, 0)
    y = _layernorm(conv_ref[...] + cb_ref[...], clg_ref[...], clb_ref[...])
    yb_ref[...] = (y * jax.nn.sigmoid(y)).astype(BF)

    zc = jax.nn.gelu(_dot(h, winc_ref[...]))
    u = zc[:, :SG_WIDTH]
    v = _layernorm(zc[:, SG_WIDTH:], sglg_ref[...], sglb_ref[...]).astype(BF)
    head = lax.broadcasted_iota(jnp.int32, (SG_CHUNK, SG_WIDTH), 1) // SG_HEAD_DIM
    zero = jnp.zeros((SG_CHUNK, SG_WIDTH), BF)
    for n in range(TOK_BLOCK // SG_CHUNK):
        vn = v[n * SG_CHUNK:(n + 1) * SG_CHUNK, :]
        vcat = jnp.concatenate([jnp.where(head == hh, vn, zero) for hh in range(SG_HEADS)], axis=0)
        sgate = _dot(sgw_ref[...], vcat) + sgb_ref[...]
        yc_ref[n * SG_CHUNK:(n + 1) * SG_CHUNK, :] = (u[n * SG_CHUNK:(n + 1) * SG_CHUNK, :] * sgate).astype(BF)


def _mix_in(x, mod, norm_g, w_in, conv_w, conv_b, conv_ln_g, conv_ln_b, sg_ln_g, sg_ln_b, sgw_cat, sgb_full,
            *, layer, seq_len):
    n = x.shape[0]
    n_seq = TOK_BLOCK // seq_len
    vec = lambda width: _const_spec((None, 1, width), (layer, 0, 0))
    kern = functools.partial(_mix_in_kernel, seq_len=seq_len)
    return pl.pallas_call(
        kern,
        grid=(n // TOK_BLOCK,),
        in_specs=[
            pl.BlockSpec((TOK_BLOCK, D_MODEL), lambda i: (i, 0)),
            _mod_spec(mod.shape[0], max(seq_len // TOK_BLOCK, 1)),
            _const_spec((None, 1, D_MODEL), (3 * layer + 1, 0, 0)),
            _const_spec((None, D_MODEL, BRANCH_COLS), (layer, 0, S5_WIDTH // BRANCH_COLS)),
            _const_spec((None, D_MODEL, BRANCH_COLS), (layer, 0, S5_WIDTH // BRANCH_COLS + 1)),
            _const_spec((None, CONV_K, SUBLANES, CONV_WIDTH), (layer, 0, 0, 0)),
            vec(CONV_WIDTH), vec(CONV_WIDTH), vec(CONV_WIDTH), vec(SG_WIDTH), vec(SG_WIDTH),
            _const_spec((None, SG_CHUNK, SG_HEADS * SG_CHUNK), (layer, 0, 0)),
            _const_spec((None, SG_CHUNK, SG_WIDTH), (layer, 0, 0)),
        ],
        out_specs=[
            pl.BlockSpec((TOK_BLOCK, CONV_WIDTH), lambda i: (i, 0)),
            pl.BlockSpec((TOK_BLOCK, SG_WIDTH), lambda i: (i, 0)),
        ],
        out_shape=[
            jax.ShapeDtypeStruct((n, CONV_WIDTH), BF),
            jax.ShapeDtypeStruct((n, SG_WIDTH), BF),
        ],
        scratch_shapes=[pltpu.VMEM((SUBLANES, n_seq, seq_len + CONV_SHIFT_ROWS, CONV_WIDTH), F32),
                        pltpu.VMEM((TOK_BLOCK, CONV_WIDTH), F32)],
        compiler_params=_params(1),
        name="mix_in",
    )(x, mod, norm_g, w_in, w_in, conv_w, conv_b, conv_ln_g, conv_ln_b, sg_ln_g, sg_ln_b, sgw_cat, sgb_full)


def _s5_prep_kernel(arow_re_ref, arow_im_ref, dtrow_ref, b4_re_ref, b4_im_ref, c4_re_ref, c4_im_ref,
                    mt_ref, win_ref, woutt_ref, at_re_ref, at_im_ref):
    T, H = S5_T, S5_GROUP
    lanes = 2 * S5_CL
    lane = lax.broadcasted_iota(jnp.int32, (1, S5_NS), 1)
    lane_im = lane >= 2 * S5_STATE
    lane_bwd = (lane // S5_STATE) % 2 == 1
    e_col = lax.broadcasted_iota(jnp.int32, (S5_POW_ROWS, 1), 0).astype(F32)
    zeros = jnp.zeros((H, S5_NS), F32)
    for g in range(S5_PREP_GB):
        a_re, a_im = arow_re_ref[g], arow_im_ref[g]
        dt = jnp.exp(dtrow_ref[g])
        mag = jnp.exp(a_re * dt * e_col)
        ang = a_im * dt * e_col
        pw_re, pw_im = mag * jnp.cos(ang), mag * jnp.sin(ang)
        den = a_re * a_re + a_im * a_im
        nr, ni = pw_re[1:2, :] - 1.0, pw_im[1:2, :]
        q_re = (nr * a_re + ni * a_im) / den
        q_im = (ni * a_re - nr * a_im) / den
        b_re, b_im = b4_re_ref[g], b4_im_ref[g]
        bb_re = q_re * b_re - q_im * b_im
        bb_im = q_re * b_im + q_im * b_re
        c_re, c_im = c4_re_ref[g], c4_im_ref[g]

        def power(e_f, e_b, pw_re=pw_re, pw_im=pw_im):
            return (jnp.where(lane_bwd, pw_re[e_b:e_b + 1, :], pw_re[e_f:e_f + 1, :]),
                    jnp.where(lane_bwd, pw_im[e_b:e_b + 1, :], pw_im[e_f:e_f + 1, :]))

        def times_b(e_f, e_b, bb_re=bb_re, bb_im=bb_im, power=power):
            p_re, p_im = power(e_f, e_b)
            return jnp.where(lane_im, p_re * bb_im + p_im * bb_re, p_re * bb_re - p_im * bb_im)

        def times_c(e_f, e_b, c_re=c_re, c_im=c_im, power=power):
            p_re, p_im = power(e_f, e_b)
            return jnp.where(lane_im, -(c_re * p_im + c_im * p_re), c_re * p_re - c_im * p_im)

        win_ref[g] = jnp.concatenate([times_b(T - 1 - j, j) for j in range(T)], axis=0).astype(BF)
        woutt_ref[g] = jnp.concatenate([times_c(i + 1, T - i) for i in range(T)], axis=0).astype(BF)
        blocks = []
        for b in range(2 * T - 1):
            m = T - 1 - b
            w = times_b(abs(m), abs(m))
            if m > 0:
                w = jnp.where(lane_bwd, 0.0, w)
            elif m < 0:
                w = jnp.where(lane_bwd, w, 0.0)
            blocks.append(w)
        ystack = jnp.concatenate(blocks + [zeros], axis=0)
        c_signed = jnp.where(lane_im, -c_im, c_re)
        krev = lax.dot_general(c_signed, ystack, (((1,), (1,)), ((), ())), preferred_element_type=F32,
                               precision=HIGHEST)
        mt_ref[g] = jnp.concatenate(
            [pltpu.roll(krev, (lanes - H * (T - 1 - i)) % lanes, 1)[:, :S5_CL] for i in range(T)],
            axis=0).astype(BF)
        at_re_ref[g] = pw_re[T:T + 1, :]
        at_im_ref[g] = pw_im[T:T + 1, :]


S5_OPS = ("mt", "win", "woutt", "at_re", "at_im")


def _s5_prep_stage(a_re, a_im, log_dt, b_re, b_im, c_re, c_im):
    G, P, H = S5_GROUPS, S5_STATE, S5_GROUP
    n = DEPTH * G
    rep = lambda a: jnp.broadcast_to(a.transpose(0, 2, 1, 3)[:, :, None, :, :],
                                     (DEPTH, G, 2, 2, P)).reshape(n, 1, S5_NS)
    dt4 = rep(jnp.broadcast_to(log_dt[..., None], (DEPTH, 2, G, P)))
    c4 = lambda c: jnp.broadcast_to(c.transpose(0, 2, 3, 1, 4)[:, :, :, None, :, :],
                                    (DEPTH, G, H, 2, 2, P)).reshape(n, H, S5_NS)
    b4 = lambda b: c4(b.transpose(0, 1, 2, 4, 3))
    blk = lambda r, c: pl.BlockSpec((S5_PREP_GB, r, c), lambda i: (i, 0, 0))
    op = jax.ShapeDtypeStruct((n, S5_CL, S5_CL), BF)
    row = jax.ShapeDtypeStruct((n, 1, S5_NS), F32)
    return dict(
        kernel=_s5_prep_kernel, grid=(n // S5_PREP_GB,),
        in_specs=[blk(1, S5_NS), blk(1, S5_NS), blk(1, S5_NS), blk(H, S5_NS), blk(H, S5_NS), blk(H, S5_NS),
                  blk(H, S5_NS)],
        args=[rep(a_re), rep(a_im), dt4, b4(b_re), b4(b_im), c4(c_re), c4(c_im)],
        out_specs=[blk(S5_CL, S5_CL), blk(S5_CL, S5_NS), blk(S5_CL, S5_NS), blk(1, S5_NS), blk(1, S5_NS)],
        out_shape=[op, op, op, row, row])


def _token_slab_copies(hbm_ref, buf_ref, sem_ref, block, slot, to_hbm):
    copies = []
    for j in range(S5_T):
        hbm = hbm_ref.at[pl.ds(block * S5_CB, S5_CB), j, :]
        vmem = buf_ref.at[slot, j]
        src, dst = (vmem, hbm) if to_hbm else (hbm, vmem)
        copies.append(pltpu.make_async_copy(src, dst, sem_ref.at[slot, j]))
    return copies


def _s5_in_kernel(x_hbm, mod_ref, g_ref, w_ref, zt_ref, xbuf, sem, *, mod_rows):
    g = g_ref[...]
    rows = S5_CB // mod_rows
    i = pl.program_id(0)
    slot = i % 2
    fetch = functools.partial(_token_slab_copies, x_hbm, xbuf, sem, to_hbm=False)

    @pl.when(i == 0)
    def _():
        for c in fetch(0, 0):
            c.start()

    @pl.when(i + 1 < pl.num_programs(0))
    def _():
        for c in fetch(i + 1, 1 - slot):
            c.start()

    for c in fetch(i, slot):
        c.wait()

    def normed(j):
        x = xbuf[slot, j]
        parts = [_mod_norm(x[m * rows:(m + 1) * rows, :], g, mod_ref[m, 4:5, :], mod_ref[m, 3:4, :])
                 for m in range(mod_rows)]
        return parts[0] if mod_rows == 1 else jnp.concatenate(parts, axis=0)

    wt = w_ref[...].T
    for j in range(0, S5_T, 2):
        h = jnp.concatenate([normed(j), normed(j + 1)], axis=0).astype(BF)
        zt = lax.dot_general(wt, h, (((1,), (1,)), ((), ())), preferred_element_type=F32)
        zt_ref[j] = zt[:, :S5_CB]
        zt_ref[j + 1] = zt[:, S5_CB:]


def _s5_in(x, mod, norm_g, w_in, *, layer, seq_chunks):
    nc = x.shape[0] // S5_T
    xv = x.reshape(nc, S5_T, D_MODEL)
    mod_rows = 1 if mod.shape[0] == 1 else S5_CB // seq_chunks
    in_specs = [
        pl.BlockSpec(memory_space=pl.ANY),
        _mod_spec(mod.shape[0], 1, rows=mod_rows),
        _const_spec((None, 1, D_MODEL), (3 * layer + 1, 0, 0)),
        _const_spec((None, D_MODEL, S5_WIDTH), (layer, 0, 0)),
    ]
    return pl.pallas_call(
        functools.partial(_s5_in_kernel, mod_rows=mod_rows),
        grid=(nc // S5_CB,),
        in_specs=in_specs,
        out_specs=pl.BlockSpec((S5_T, S5_WIDTH, S5_CB), lambda i: (0, 0, i)),
        out_shape=jax.ShapeDtypeStruct((S5_T, S5_WIDTH, nc), F32),
        scratch_shapes=[pltpu.VMEM((2, S5_T, S5_CB, D_MODEL), F32), pltpu.SemaphoreType.DMA((2, S5_T))],
        compiler_params=_params(1),
        name="s5_in",
    )(xv, mod, norm_g, w_in)


def _s5_core_kernel(*refs, groups, seq_chunks, has_h0, want_final):
    refs = list(refs)
    zt_ref, mt_ref, win_ref, woutt_ref, at_re_ref, at_im_ref = refs[:6]
    refs = refs[6:]
    h0_ref = refs.pop(0) if has_h0 else None
    yt_ref = refs.pop(0)
    hf_ref = refs.pop(0) if want_final else None
    T, K = S5_T, seq_chunks
    half = S5_NS // 2
    nc = zt_ref.shape[2]
    n_seq = nc // K
    n_steps = K.bit_length() - 1
    fwd = lax.broadcasted_iota(jnp.int32, (nc, half), 1) < S5_STATE
    bwd = jnp.logical_not(fwd)
    kpos = lax.broadcasted_iota(jnp.int32, (nc, half), 0) % K
    first = (fwd & (kpos == 0)) | (bwd & (kpos == K - 1))
    valid = [(fwd & (kpos >= (1 << t))) | (bwd & (kpos < K - (1 << t))) for t in range(n_steps)]
    shift = lambda x, s: jnp.where(fwd, pltpu.roll(x, s, 0), pltpu.roll(x, nc - s, 0))
    if has_h0:
        chunk = lax.broadcasted_iota(jnp.int32, (nc, n_seq), 0)
        seq = lax.broadcasted_iota(jnp.int32, (nc, n_seq), 1)
        place_f = (chunk == seq * K).astype(F32)
        place_b = (chunk == seq * K + (K - 1)).astype(F32)
        fwd_full = jnp.concatenate([fwd, fwd], axis=1)
    if want_final:
        seq = lax.broadcasted_iota(jnp.int32, (LANES, nc), 0)
        chunk = lax.broadcasted_iota(jnp.int32, (LANES, nc), 1)
        pick_f = (chunk == seq * K + (K - 1)).astype(F32)
        pick_b = (chunk == seq * K).astype(F32)
        fwd_row = lax.broadcasted_iota(jnp.int32, (1, S5_NS), 1) % half < S5_STATE

    xts, vs, h0s = [], [], []
    for g in range(groups):
        r0 = g * S5_GROUP
        xt = jnp.concatenate([zt_ref[j, r0:r0 + S5_GROUP, :] for j in range(T)], axis=0).astype(BF)
        xts.append(xt)
        vs.append(lax.dot_general(xt, win_ref[g], (((0,), (0,)), ((), ())), preferred_element_type=F32))
        if has_h0:
            h0s.append(jnp.where(fwd_full, _dot_exact(place_f, h0_ref[g]), _dot_exact(place_b, h0_ref[g])))
    yield
    s_ins, fins = [], []
    for g in range(groups):
        vr, vi = vs[g][:, :half], vs[g][:, half:]
        cr, ci = at_re_ref[g, :, :half], at_im_ref[g, :, :half]
        ar, ai = cr, ci
        hr, hi = (h0s[g][:, :half], h0s[g][:, half:]) if has_h0 else (0.0, 0.0)
        sr = jnp.where(first, hr, shift(vr, 1))
        si = jnp.where(first, hi, shift(vi, 1))
        for t in range(n_steps):
            rr = jnp.where(valid[t], shift(sr, 1 << t), 0.0)
            ri = jnp.where(valid[t], shift(si, 1 << t), 0.0)
            sr, si = sr + (cr * rr - ci * ri), si + (cr * ri + ci * rr)
            cr, ci = cr * cr - ci * ci, 2.0 * (cr * ci)
        s_ins.append(jnp.concatenate([sr, si], axis=1).astype(BF))
        if want_final:
            fins.append(jnp.concatenate([ar * sr - ai * si + vr, ar * si + ai * sr + vi], axis=1))
    yield
    for g in range(groups):
        r0 = g * S5_GROUP
        y = _dot(mt_ref[g], xts[g]) + lax.dot_general(woutt_ref[g], s_ins[g], (((1,), (1,)), ((), ())),
                                                      preferred_element_type=F32)
        for i in range(T):
            yt_ref[i, r0:r0 + S5_GROUP, :] = y[i * S5_GROUP:(i + 1) * S5_GROUP, :]
        if want_final:
            hf_ref[g] = jnp.where(fwd_row, _dot_exact(pick_f, fins[g]), _dot_exact(pick_b, fins[g]))


def _s5_core_stage(zt, ops, *, layer, seq_chunks, steps, h0=None, want_final=False):
    nc = zt.shape[2]
    gb = S5_GROUPS // steps
    rows = gb * S5_GROUP
    gspec = lambda r, c: pl.BlockSpec((gb, r, c), lambda i: (layer * steps + i, 0, 0))
    slab = pl.BlockSpec((S5_T, rows, nc), lambda i: (0, i, 0))
    in_specs = [slab, gspec(S5_CL, S5_CL), gspec(S5_CL, S5_NS), gspec(S5_CL, S5_NS),
                gspec(1, S5_NS), gspec(1, S5_NS)]
    args = [zt, ops["mt"], ops["win"], ops["woutt"], ops["at_re"], ops["at_im"]]
    if h0 is not None:
        in_specs.append(pl.BlockSpec((gb, h0.shape[1], S5_NS), lambda i: (i, 0, 0)))
        args.append(h0)
    out_specs = [slab]
    out_shape = [jax.ShapeDtypeStruct(zt.shape, F32)]
    if want_final:
        assert nc // seq_chunks <= LANES
        out_specs.append(pl.BlockSpec((gb, LANES, S5_NS), lambda i: (i, 0, 0)))
        out_shape.append(jax.ShapeDtypeStruct((S5_GROUPS, LANES, S5_NS), F32))
    kern = functools.partial(_s5_core_kernel, groups=gb, seq_chunks=seq_chunks, has_h0=h0 is not None,
                             want_final=want_final)
    return dict(kernel=kern, grid=(steps,), in_specs=in_specs, args=args, out_specs=out_specs, out_shape=out_shape)


def _s5_out_kernel(yt_ref, zt_ref, d_ref, wglut_ref, wa_ref, o_hbm, obuf, sem):
    i = pl.program_id(0)
    last = pl.num_programs(0) - 1
    slot = i % 2
    store = functools.partial(_token_slab_copies, o_hbm, obuf, sem, to_hbm=True)

    @pl.when(i >= 2)
    def _():
        for c in store(i - 2, slot):
            c.wait()

    d = d_ref[...]
    for j in range(0, S5_T, 2):
        pre = jnp.concatenate([yt_ref[j] + d * zt_ref[j], yt_ref[j + 1] + d * zt_ref[j + 1]], axis=1)
        ya = jax.nn.gelu(pre)
        ya = ya * jax.nn.sigmoid(_dot(wglut_ref[...], ya.astype(BF)))
        pa = lax.dot_general(ya.astype(BF), wa_ref[...], (((0,), (0,)), ((), ())), preferred_element_type=F32)
        obuf[slot, j] = pa[:S5_CB, :]
        obuf[slot, j + 1] = pa[S5_CB:, :]
    for c in store(i, slot):
        c.start()

    @pl.when(i == last)
    def _():
        for c in store(i, slot):
            c.wait()

    @pl.when((i == last) & (i >= 1))
    def _():
        for c in store(i - 1, 1 - slot):
            c.wait()


def _s5_out(yt, zt, s5_d_col, w_glu_t, w_br_a, *, layer):
    nc = yt.shape[2]
    slab = pl.BlockSpec((S5_T, S5_WIDTH, S5_CB), lambda i: (0, 0, i))
    out = pl.pallas_call(
        _s5_out_kernel,
        grid=(nc // S5_CB,),
        in_specs=[slab, slab,
                  _const_spec((None, S5_WIDTH, 1), (layer, 0, 0)),
                  _const_spec((None, S5_WIDTH, S5_WIDTH), (layer, 0, 0)),
                  _const_spec((None, S5_WIDTH, D_MODEL), (layer, 0, 0))],
        out_specs=pl.BlockSpec(memory_space=pl.ANY),
        out_shape=jax.ShapeDtypeStruct((nc, S5_T, D_MODEL), F32),
        scratch_shapes=[pltpu.VMEM((2, S5_T, S5_CB, D_MODEL), F32), pltpu.SemaphoreType.DMA((2, S5_T))],
        compiler_params=_params(1),
        name="s5_out",
    )(yt, zt, s5_d_col, w_glu_t, w_br_a)
    return out.reshape(nc * S5_T, D_MODEL)


def _mix_out_kernel(x_ref, mod_ref, g_ref, pa_ref, yb_ref, yc_ref, wb_ref, wc_ref, wgate_ref, bgate_ref,
                    wout_ref, o_ref):
    x = x_ref[...]
    h = _mod_norm(x, g_ref[...], mod_ref[0, 4:5, :], mod_ref[0, 3:4, :]).astype(BF)

    def gate(k):
        lo = k * D_MODEL
        return jax.nn.sigmoid(_dot(h, wgate_ref[:, lo:lo + D_MODEL]) + bgate_ref[:, lo:lo + D_MODEL])

    merged = gate(0) * pa_ref[...]
    merged = merged + gate(1) * _dot(yb_ref[...], wb_ref[...])
    merged = merged + gate(2) * _dot(yc_ref[...], wc_ref[...])
    y = _dot(merged.astype(BF), wout_ref[...])
    o_ref[...] = x + mod_ref[0, 5:6, :] * y


def _mix_out(x, mod, norm_g, pa, yb, yc, w_b, w_c, w_gate, b_gate, w_out, *, layer, seq_tokens):
    n = x.shape[0]
    tm = FFN_TM
    tok = lambda c: pl.BlockSpec((tm, c), lambda i: (i, 0))
    return pl.pallas_call(
        _mix_out_kernel,
        grid=(n // tm,),
        in_specs=[
            tok(D_MODEL),
            _mod_spec(mod.shape[0], seq_tokens // tm),
            _const_spec((None, 1, D_MODEL), (3 * layer + 1, 0, 0)),
            tok(D_MODEL), tok(CONV_WIDTH), tok(SG_WIDTH),
            _const_spec((None, CONV_WIDTH, D_MODEL), (layer, 0, 0)),
            _const_spec((None, SG_WIDTH, D_MODEL), (layer, 0, 0)),
            _const_spec((None, D_MODEL, 3 * D_MODEL), (layer, 0, 0)),
            _const_spec((None, 1, 3 * D_MODEL), (layer, 0, 0)),
            _const_spec((None, D_MODEL, D_MODEL), (layer, 0, 0)),
        ],
        out_specs=tok(D_MODEL),
        out_shape=jax.ShapeDtypeStruct((n, D_MODEL), F32),
        compiler_params=_params(1),
        name="mix_out",
    )(x, mod, norm_g, pa, yb, yc, w_b, w_c, w_gate, b_gate, w_out)


def _grid_pos_embed(n_tokens, dim):
    rows = n_tokens // GRID_W
    rr, cc = jnp.meshgrid(jnp.arange(rows, dtype=F32), jnp.arange(GRID_W, dtype=F32), indexing='ij')
    quarter = dim // 4
    omega = 1.0 / (10000.0 ** (jnp.arange(quarter, dtype=F32) / quarter))

    def emb(p):
        ang = p.reshape(-1)[:, None] * omega[None, :]
        return jnp.concatenate([jnp.sin(ang), jnp.cos(ang)], axis=-1)

    return jnp.concatenate([emb(rr), emb(cc)], axis=-1)


def kernel(x_prompt, x_sample, state_ssm, c, c_ctx, w_mod, b_mod, norm_g, ffn_w1, ffn_w2, w_in, w_gate, b_gate,
           s5_a_re, s5_a_im, s5_log_dt, s5_b_re, s5_b_im, s5_c_re, s5_c_im, s5_d, s5_w_glu, w_br_a, conv_w,
           conv_b, conv_ln_g, conv_ln_b, w_br_b, sg_ln_g, sg_ln_b, sg_w, sg_b, w_br_c, w_out, final_g):
    batch, seq, _ = x_prompt.shape
    dec_batch, dec_seq, _ = x_sample.shape
    assert (batch * seq) % TOK_BLOCK == 0 and TOK_BLOCK % seq == 0 and dec_seq % TOK_BLOCK == 0
    assert seq % SG_CHUNK == 0 and seq % S5_T == 0 and dec_seq % S5_T == 0
    ctx_chunks, smp_chunks = seq // S5_T, dec_seq // S5_T
    assert S5_CB % ctx_chunks == 0 and S5_CB % smp_chunks == 0
    assert ctx_chunks & (ctx_chunks - 1) == 0 and smp_chunks & (smp_chunks - 1) == 0
    assert 1 + dec_batch <= MOD_ROWS

    cond = jnp.zeros((MOD_ROWS, D_MODEL), F32).at[0].set(c_ctx).at[1:1 + dec_batch].set(c)
    [mod_all], prep_out = _call([_modulation_stage(cond, w_mod, b_mod),
                                 _s5_prep_stage(s5_a_re, s5_a_im, s5_log_dt, s5_b_re, s5_b_im, s5_c_re, s5_c_im)],
                                "mod_prep")
    mod_all = mod_all.reshape(DEPTH, MOD_ROWS, N_MOD, D_MODEL)
    s5_ops = dict(zip(S5_OPS, prep_out))
    pos = _grid_pos_embed(dec_seq, D_MODEL)

    ffn_w = {(0, 0): (ffn_w1[0, 0].astype(BF), ffn_w2[0, 0].astype(BF))}
    mixer_w = dict(w_in=w_in, w_gate=w_gate, w_out=w_out, w_a=w_br_a, w_b=w_br_b, w_c=w_br_c)
    w_glu_t = s5_w_glu.transpose(0, 2, 1).astype(BF)
    norm_rows = norm_g.reshape(DEPTH * 3, 1, D_MODEL)
    rows = lambda a: a.reshape(DEPTH, 1, -1)
    sgw_cat = sg_w.transpose(0, 2, 1, 3).reshape(DEPTH, SG_CHUNK, SG_HEADS * SG_CHUNK).astype(BF)
    sgb_full = jnp.repeat(sg_b.transpose(0, 2, 1), SG_HEAD_DIM, axis=2)
    s5_d_col = s5_d.reshape(DEPTH, S5_WIDTH, 1)

    conv_w8 = jnp.broadcast_to(conv_w[:, :, None, :], (DEPTH, CONV_K, SUBLANES, CONV_WIDTH))

    ctx = dict(n_seq=batch, seq_len=seq, mod_rows=slice(0, 1), mod_tokens=batch * seq)
    smp = dict(n_seq=dec_batch, seq_len=dec_seq, mod_rows=slice(1, 1 + dec_batch), mod_tokens=dec_seq)
    x_ctx = x_prompt.reshape(batch * seq, D_MODEL)
    x_smp = x_sample.reshape(dec_batch * dec_seq, D_MODEL)
    ffn_steps = x_ctx.shape[0] // FFN_TM
    assert x_smp.shape[0] // FFN_TM == ffn_steps and S5_GROUPS % ffn_steps == 0

    assert DEPTH * S5_GROUPS // S5_PREP_GB == ffn_steps

    def ffn_stage(gr, x, l, which, **kw):
        return _ffn_stage(x, mod_all[l, gr["mod_rows"]], norm_rows, *ffn_w[l, which], layer=l, which=which,
                          seq_tokens=gr["mod_tokens"], **kw)

    def cast_stages(l, which):
        if l >= DEPTH:
            return []
        return [_cast_stage(ffn_w1, (l, which), ffn_steps), _cast_stage(ffn_w2, (l, which), ffn_steps)]

    def keep_casts(l, which, outs):
        if outs:
            ffn_w[l, which] = (outs[0][0], outs[1][0])

    def branches_in(gr, x, l):
        mod = mod_all[l, gr["mod_rows"]]
        yb, yc = _mix_in(x, mod, norm_rows, mixer_w["w_in"], conv_w8, rows(conv_b), rows(conv_ln_g), rows(conv_ln_b),
                         rows(sg_ln_g), rows(sg_ln_b), sgw_cat, sgb_full, layer=l, seq_len=gr["seq_len"])
        zt = _s5_in(x, mod, norm_rows, mixer_w["w_in"], layer=l, seq_chunks=gr["seq_len"] // S5_T)
        return yb, yc, zt

    def s5_stage(gr, zt, l, **kw):
        return _s5_core_stage(zt, s5_ops, layer=l, seq_chunks=gr["seq_len"] // S5_T, steps=ffn_steps, **kw)

    def branches_out(gr, x, l, yb, yc, zt, yt):
        pa = _s5_out(yt, zt, s5_d_col, w_glu_t, mixer_w["w_a"], layer=l)
        return _mix_out(x, mod_all[l, gr["mod_rows"]], norm_rows, pa, yb, yc, mixer_w["w_b"], mixer_w["w_c"],
                        mixer_w["w_gate"], rows(b_gate), mixer_w["w_out"], layer=l, seq_tokens=gr["mod_tokens"])

    ctx_states = []
    for l in range(DEPTH):
        riders = []
        if l == 0:
            riders = cast_stages(0, 1) + [_cast_stage(w.reshape(-1, w.shape[-1]), (), ffn_steps)
                                          for w in mixer_w.values()]
        [x_ctx], *outs = _call([ffn_stage(ctx, x_ctx, l, 0)] + riders, "ffn")
        if l == 0:
            keep_casts(0, 1, outs[:2])
            mixer_w = {k: o[0].reshape(mixer_w[k].shape) for k, o in zip(mixer_w, outs[2:])}
        yb_c, yc_c, zt_c = branches_in(ctx, x_ctx, l)
        [x_smp], [yt_c, hf], *outs = _call([ffn_stage(smp, x_smp, l, 0, pos=pos if l == 0 else None),
                                             s5_stage(ctx, zt_c, l, want_final=True)] + cast_stages(l + 1, 0),
                                            "ffn_s5")
        keep_casts(l + 1, 0, outs)
        hf = hf[:, :batch, :].reshape(S5_GROUPS, batch, 2, 2, S5_STATE)
        ctx_states.append(hf.transpose(1, 3, 0, 4, 2))
        x_ctx = branches_out(ctx, x_ctx, l, yb_c, yc_c, zt_c, yt_c)
        yb_s, yc_s, zt_s = branches_in(smp, x_smp, l)
        h0 = state_ssm[:, l].transpose(2, 0, 4, 1, 3).reshape(S5_GROUPS, dec_batch, S5_NS)
        final = dict(final_g=final_g) if l == DEPTH - 1 else {}
        [x_ctx], [yt_s], *outs = _call([ffn_stage(ctx, x_ctx, l, 1, **final), s5_stage(smp, zt_s, l, h0=h0)]
                                       + cast_stages(l + 1, 1), "ffn_s5")
        keep_casts(l + 1, 1, outs)
        x_smp = branches_out(smp, x_smp, l, yb_s, yc_s, zt_s, yt_s)
        [x_smp], = _call([ffn_stage(smp, x_smp, l, 1, **final)], "ffn")
    y_prompt = x_ctx.reshape(batch, seq, D_MODEL)
    y_sample = x_smp.reshape(dec_batch, dec_seq, D_MODEL)
    new_state_ssm = jnp.stack(ctx_states, axis=1)
    return (y_prompt, y_sample, new_state_ssm)
```

```python
import functools
import inspect

import jax
import jax.numpy as jnp
from jax import lax
from jax.experimental import pallas as pl
from jax.experimental.pallas import tpu as pltpu

D_MODEL = 1024
DEPTH = 2
GRID_W = 64
D_FF = 2816
S5_WIDTH = 512
S5_GROUP = 16
S5_GROUPS = 32
S5_STATE = 64
CONV_WIDTH = 256
CONV_K = 31
SG_WIDTH = 256
SG_CHUNK = 128
SG_HEADS = 4
SG_HEAD_DIM = SG_WIDTH // SG_HEADS
BRANCH_COLS = 2 * CONV_WIDTH
IN_COLS = S5_WIDTH + 2 * BRANCH_COLS
N_MOD = 9
EPS = 1e-6

LANES = 128
SUBLANES = 8
S5_T = 16
S5_CL = S5_T * S5_GROUP
S5_NS = 4 * S5_STATE
S5_PREP_GB = 4
S5_POW_ROWS = SUBLANES * (S5_T // SUBLANES + 1)
S5_CB = 128
MOD_ROWS = 16
MOD_TN = 1152
TOK_BLOCK = 1024
FFN_TM = 512
FFN_SUBTILES = 4
CONV_RC = 32
CONV_PAD = 16
CONV_SHIFT_ROWS = SUBLANES * ((CONV_PAD - CONV_K // 2 + CONV_K - 1) // SUBLANES)
VMEM_LIMIT = 56 * 1024 * 1024

BF = jnp.bfloat16
F32 = jnp.float32
HIGHEST = lax.Precision.HIGHEST


def _dot(a, b):
    return jnp.dot(a, b, preferred_element_type=F32)


def _dot_exact(a, b):
    return jnp.dot(a, b, preferred_element_type=F32, precision=HIGHEST)


def _const_spec(block, index):
    return pl.BlockSpec(block, lambda *_: index, pipeline_mode=pl.Buffered(1))


def _params(n_axes=1):
    return pltpu.CompilerParams(dimension_semantics=("arbitrary",) * n_axes,
                                vmem_limit_bytes=VMEM_LIMIT)


def _call(stages, name):
    grid = stages[0]["grid"]
    assert all(s["grid"] == grid for s in stages)
    n_in = [len(s["in_specs"]) for s in stages]
    n_out = [len(s["out_specs"]) for s in stages]
    total_in, total_out = sum(n_in), sum(n_out)

    def body(*refs):
        bound, i, o = [], 0, 0
        for s, ni, no in zip(stages, n_in, n_out):
            bound.append((s["kernel"], refs[i:i + ni] + refs[total_in + o:total_in + o + no]))
            i, o = i + ni, o + no
        live = [run for run in (kern(*r) for kern, r in reversed(bound)) if inspect.isgenerator(run)]
        while live:
            for gen in list(live):
                if next(gen, "done") == "done":
                    live.remove(gen)

    outs = pl.pallas_call(
        body,
        grid=grid,
        in_specs=[spec for s in stages for spec in s["in_specs"]],
        out_specs=[spec for s in stages for spec in s["out_specs"]],
        out_shape=[shape for s in stages for shape in s["out_shape"]],
        compiler_params=_params(len(grid)),
        name=name,
    )(*[a for s in stages for a in s["args"]])
    split, o = [], 0
    for no in n_out:
        split.append(list(outs[o:o + no]))
        o += no
    return split


def _mod_norm(x, g, sc, sh):
    var = jnp.mean(x * x, axis=-1, keepdims=True)
    return (x * lax.rsqrt(var + EPS) * g) * (1.0 + sc) + sh


def _layernorm(x, g, b):
    mu = jnp.mean(x, axis=-1, keepdims=True)
    xc = x - mu
    var = jnp.mean(xc * xc, axis=-1, keepdims=True)
    return xc * lax.rsqrt(var + EPS) * g + b


def _mod_spec(n_mod, blocks_per_mod, rows=1):
    if n_mod == 1:
        return pl.BlockSpec((1, N_MOD, D_MODEL), lambda i: (0, 0, 0))
    return pl.BlockSpec((rows, N_MOD, D_MODEL), lambda i: (i // blocks_per_mod, 0, 0))


def _mod_kernel(c_ref, w_ref, b_ref, o_ref):
    c = c_ref[...]
    a = (c * jax.nn.sigmoid(c)).astype(BF)
    o_ref[...] = _dot(a, w_ref[...].astype(BF)) + b_ref[...]


def _modulation_stage(cond, w_mod, b_mod):
    n = N_MOD * D_MODEL
    tiles = n // MOD_TN
    tile = lambda i: (i // tiles, 0, i % tiles)
    return dict(
        kernel=_mod_kernel, grid=(DEPTH * tiles,), args=[cond, w_mod, b_mod.reshape(DEPTH, 1, n)],
        in_specs=[pl.BlockSpec((MOD_ROWS, D_MODEL), lambda i: (0, 0)),
                  pl.BlockSpec((None, D_MODEL, MOD_TN), tile),
                  pl.BlockSpec((None, 1, MOD_TN), tile)],
        out_specs=[pl.BlockSpec((None, MOD_ROWS, MOD_TN), tile)],
        out_shape=[jax.ShapeDtypeStruct((DEPTH, MOD_ROWS, n), F32)])


def _ffn_kernel(*refs, mod_base, add_pos, final):
    refs = list(refs)
    x_ref = refs.pop(0)
    pos_ref = refs.pop(0) if add_pos else None
    mod_ref, g_ref, w1g_ref, w1u_ref, w2_ref = refs[:5]
    refs = refs[5:]
    fg_ref = refs.pop(0) if final else None
    o_ref = refs.pop(0)

    sh = mod_ref[0, mod_base:mod_base + 1, :]
    sc = mod_ref[0, mod_base + 1:mod_base + 2, :]
    gt = mod_ref[0, mod_base + 2:mod_base + 3, :]
    rows = x_ref.shape[0] // FFN_SUBTILES
    sub = [slice(r * rows, (r + 1) * rows) for r in range(FFN_SUBTILES)]
    xs, hs, gs, accs = [], [], [], []
    for r in sub:
        x = x_ref[r, :] + pos_ref[r, :] if add_pos else x_ref[r, :]
        xs.append(x)
        hs.append(_mod_norm(x, g_ref[...], sc, sh).astype(BF))
        gs.append(_dot(hs[-1], w1g_ref[...]))
    yield
    for h, g in zip(hs, gs):
        u = _dot(h, w1u_ref[...])
        accs.append((g * jax.nn.sigmoid(g) * u).astype(BF))
    yield
    for r, x, a in zip(sub, xs, accs):
        y = _dot(a, w2_ref[...])
        xn = x + (0.5 * gt) * y
        if final:
            var = jnp.mean(xn * xn, axis=-1, keepdims=True)
            xn = xn * lax.rsqrt(var + EPS) * fg_ref[...]
        o_ref[r, :] = xn
    yield


def _cast_kernel(x_ref, o_ref):
    o_ref[...] = x_ref[...].astype(o_ref.dtype)


def _cast_stage(w, index, steps):
    rows, cols = w.shape[-2:]
    lead = (None,) * len(index)
    return dict(kernel=_cast_kernel, grid=(steps,), args=[w],
                in_specs=[pl.BlockSpec(lead + (rows // steps, cols), lambda i: tuple(index) + (i, 0))],
                out_specs=[pl.BlockSpec((rows // steps, cols), lambda i: (i, 0))],
                out_shape=[jax.ShapeDtypeStruct((rows, cols), BF)])


def _ffn_stage(x, mod, norm_g, w1, w2, *, layer, which, seq_tokens, pos=None, final_g=None):
    n = x.shape[0]
    tm = FFN_TM
    in_specs = [pl.BlockSpec((tm, D_MODEL), lambda i: (i, 0))]
    args = [x]
    if pos is not None:
        pos_blocks = pos.shape[0] // tm
        in_specs.append(pl.BlockSpec((tm, D_MODEL), lambda i: (i % pos_blocks, 0)))
        args.append(pos)
    in_specs += [
        _mod_spec(mod.shape[0], seq_tokens // tm),
        _const_spec((None, 1, D_MODEL), (3 * layer + 2 * which, 0, 0)),
        _const_spec((D_MODEL, D_FF), (0, 0)),
        _const_spec((D_MODEL, D_FF), (0, 1)),
        _const_spec((D_FF, D_MODEL), (0, 0)),
    ]
    args += [mod, norm_g, w1, w1, w2]
    if final_g is not None:
        in_specs.append(_const_spec((1, D_MODEL), (0, 0)))
        args.append(final_g.reshape(1, D_MODEL))
    kern = functools.partial(_ffn_kernel, mod_base=6 * which, add_pos=pos is not None,
                             final=final_g is not None)
    return dict(kernel=kern, grid=(n // tm,), in_specs=in_specs, args=args,
                out_specs=[pl.BlockSpec((tm, D_MODEL), lambda i: (i, 0))],
                out_shape=[jax.ShapeDtypeStruct((n, D_MODEL), F32)])


def _mix_in_kernel(x_ref, mod_ref, g_ref, winb_ref, winc_ref, cw_ref, cb_ref, clg_ref, clb_ref,
                   sglg_ref, sglb_ref, sgw_ref, sgb_ref, yb_ref, yc_ref, pad_ref, conv_ref, *, seq_len):
    n_seq = TOK_BLOCK // seq_len
    x = x_ref[...]
    h = _mod_norm(x, g_ref[...], mod_ref[0, 4:5, :], mod_ref[0, 3:4, :]).astype(BF)

    zb = _dot(h, winb_ref[...])
    gl = zb[:, :CONV_WIDTH] * jax.nn.sigmoid(zb[:, CONV_WIDTH:])
    zeros = jnp.zeros((CONV_PAD, CONV_WIDTH), F32)
    shifted_rows = seq_len + CONV_SHIFT_ROWS
    padded_rows = seq_len + 2 * CONV_PAD
    for s in range(n_seq):
        padded = jnp.concatenate([zeros, gl[s * seq_len:(s + 1) * seq_len, :], zeros], axis=0)
        pad_ref[0, s] = padded[:shifted_rows, :]
        for r in range(1, SUBLANES):
            pad_ref[r, s] = pltpu.roll(padded, padded_rows - r, 0)[:shifted_rows, :]
    first = CONV_PAD - CONV_K // 2
    tiles = CONV_RC // SUBLANES
    n_q = CONV_SHIFT_ROWS // SUBLANES + 1
    for s in range(n_seq):
        def conv_rows(r, carry, s=s):
            r0 = pl.multiple_of(r * CONV_RC, CONV_RC)
            accs = [None] * n_q
            for shift in range(SUBLANES):
                span = pad_ref[shift, s, pl.ds(r0, CONV_RC + CONV_SHIFT_ROWS), :]
                span = span.reshape(tiles + n_q - 1, SUBLANES, CONV_WIDTH)
                for q in range(n_q):
                    k = SUBLANES * q + shift - first
                    if 0 <= k < CONV_K:
                        term = cw_ref[k] * span[q:q + tiles]
                        accs[q] = term if accs[q] is None else accs[q] + term
            acc = (accs[0] + accs[1]) + (accs[2] + accs[3])
            conv_ref[pl.ds(pl.multiple_of(s * seq_len + r0, CONV_RC), CONV_RC), :] = acc.reshape(CONV_RC, CONV_WIDTH)
            return carry
        lax.fori_loop(0, seq_len // CONV_RC, conv_rows, 0)
    y = _layernorm(conv_ref[...] + cb_ref[...], clg_ref[...], clb_ref[...])
    yb_ref[...] = (y * jax.nn.sigmoid(y)).astype(BF)

    zc = jax.nn.gelu(_dot(h, winc_ref[...]))
    u = zc[:, :SG_WIDTH]
    v = _layernorm(zc[:, SG_WIDTH:], sglg_ref[...], sglb_ref[...]).astype(BF)
    head = lax.broadcasted_iota(jnp.int32, (SG_CHUNK, SG_WIDTH), 1) // SG_HEAD_DIM
    zero = jnp.zeros((SG_CHUNK, SG_WIDTH), BF)
    for n in range(TOK_BLOCK // SG_CHUNK):
        vn = v[n * SG_CHUNK:(n + 1) * SG_CHUNK, :]
        vcat = jnp.concatenate([jnp.where(head == hh, vn, zero) for hh in range(SG_HEADS)], axis=0)
        sgate = _dot(sgw_ref[...], vcat) + sgb_ref[...]
        yc_ref[n * SG_CHUNK:(n + 1) * SG_CHUNK, :] = (u[n * SG_CHUNK:(n + 1) * SG_CHUNK, :] * sgate).astype(BF)


def _mix_in(x, mod, norm_g, w_in, conv_w, conv_b, conv_ln_g, conv_ln_b, sg_ln_g, sg_ln_b, sgw_cat, sgb_full,
            *, layer, seq_len):
    n = x.shape[0]
    n_seq = TOK_BLOCK // seq_len
    vec = lambda width: _const_spec((None, 1, width), (layer, 0, 0))
    kern = functools.partial(_mix_in_kernel, seq_len=seq_len)
    return pl.pallas_call(
        kern,
        grid=(n // TOK_BLOCK,),
        in_specs=[
            pl.BlockSpec((TOK_BLOCK, D_MODEL), lambda i: (i, 0)),
            _mod_spec(mod.shape[0], max(seq_len // TOK_BLOCK, 1)),
            _const_spec((None, 1, D_MODEL), (3 * layer + 1, 0, 0)),
            _const_spec((None, D_MODEL, BRANCH_COLS), (layer, 0, S5_WIDTH // BRANCH_COLS)),
            _const_spec((None, D_MODEL, BRANCH_COLS), (layer, 0, S5_WIDTH // BRANCH_COLS + 1)),
            _const_spec((None, CONV_K, SUBLANES, CONV_WIDTH), (layer, 0, 0, 0)),
            vec(CONV_WIDTH), vec(CONV_WIDTH), vec(CONV_WIDTH), vec(SG_WIDTH), vec(SG_WIDTH),
            _const_spec((None, SG_CHUNK, SG_HEADS * SG_CHUNK), (layer, 0, 0)),
            _const_spec((None, SG_CHUNK, SG_WIDTH), (layer, 0, 0)),
        ],
        out_specs=[
            pl.BlockSpec((TOK_BLOCK, CONV_WIDTH), lambda i: (i, 0)),
            pl.BlockSpec((TOK_BLOCK, SG_WIDTH), lambda i: (i, 0)),
        ],
        out_shape=[
            jax.ShapeDtypeStruct((n, CONV_WIDTH), BF),
            jax.ShapeDtypeStruct((n, SG_WIDTH), BF),
        ],
        scratch_shapes=[pltpu.VMEM((SUBLANES, n_seq, seq_len + CONV_SHIFT_ROWS, CONV_WIDTH), F32),
                        pltpu.VMEM((TOK_BLOCK, CONV_WIDTH), F32)],
        compiler_params=_params(1),
        name="mix_in",
    )(x, mod, norm_g, w_in, w_in, conv_w, conv_b, conv_ln_g, conv_ln_b, sg_ln_g, sg_ln_b, sgw_cat, sgb_full)


def _s5_prep_kernel(arow_re_ref, arow_im_ref, dtrow_ref, b4_re_ref, b4_im_ref, c4_re_ref, c4_im_ref,
                    mt_ref, win_ref, woutt_ref, at_re_ref, at_im_ref):
    T, H = S5_T, S5_GROUP
    lanes = 2 * S5_CL
    lane = lax.broadcasted_iota(jnp.int32, (1, S5_NS), 1)
    lane_im = lane >= 2 * S5_STATE
    lane_bwd = (lane // S5_STATE) % 2 == 1
    e_col = lax.broadcasted_iota(jnp.int32, (S5_POW_ROWS, 1), 0).astype(F32)
    zeros = jnp.zeros((H, S5_NS), F32)
    for g in range(S5_PREP_GB):
        a_re, a_im = arow_re_ref[g], arow_im_ref[g]
        dt = jnp.exp(dtrow_ref[g])
        mag = jnp.exp(a_re * dt * e_col)
        ang = a_im * dt * e_col
        pw_re, pw_im = mag * jnp.cos(ang), mag * jnp.sin(ang)
        den = a_re * a_re + a_im * a_im
        nr, ni = pw_re[1:2, :] - 1.0, pw_im[1:2, :]
        q_re = (nr * a_re + ni * a_im) / den
        q_im = (ni * a_re - nr * a_im) / den
        b_re, b_im = b4_re_ref[g], b4_im_ref[g]
        bb_re = q_re * b_re - q_im * b_im
        bb_im = q_re * b_im + q_im * b_re
        c_re, c_im = c4_re_ref[g], c4_im_ref[g]

        def power(e_f, e_b, pw_re=pw_re, pw_im=pw_im):
            return (jnp.where(lane_bwd, pw_re[e_b:e_b + 1, :], pw_re[e_f:e_f + 1, :]),
                    jnp.where(lane_bwd, pw_im[e_b:e_b + 1, :], pw_im[e_f:e_f + 1, :]))

        def times_b(e_f, e_b, bb_re=bb_re, bb_im=bb_im, power=power):
            p_re, p_im = power(e_f, e_b)
            return jnp.where(lane_im, p_re * bb_im + p_im * bb_re, p_re * bb_re - p_im * bb_im)

        def times_c(e_f, e_b, c_re=c_re, c_im=c_im, power=power):
            p_re, p_im = power(e_f, e_b)
            return jnp.where(lane_im, -(c_re * p_im + c_im * p_re), c_re * p_re - c_im * p_im)

        win_ref[g] = jnp.concatenate([times_b(T - 1 - j, j) for j in range(T)], axis=0).astype(BF)
        woutt_ref[g] = jnp.concatenate([times_c(i + 1, T - i) for i in range(T)], axis=0).astype(BF)
        blocks = []
        for b in range(2 * T - 1):
            m = T - 1 - b
            w = times_b(abs(m), abs(m))
            if m > 0:
                w = jnp.where(lane_bwd, 0.0, w)
            elif m < 0:
                w = jnp.where(lane_bwd, w, 0.0)
            blocks.append(w)
        ystack = jnp.concatenate(blocks + [zeros], axis=0)
        c_signed = jnp.where(lane_im, -c_im, c_re)
        krev = lax.dot_general(c_signed, ystack, (((1,), (1,)), ((), ())), preferred_element_type=F32,
                               precision=HIGHEST)
        mt_ref[g] = jnp.concatenate(
            [pltpu.roll(krev, (lanes - H * (T - 1 - i)) % lanes, 1)[:, :S5_CL] for i in range(T)],
            axis=0).astype(BF)
        at_re_ref[g] = pw_re[T:T + 1, :]
        at_im_ref[g] = pw_im[T:T + 1, :]


S5_OPS = ("mt", "win", "woutt", "at_re", "at_im")


def _s5_prep_stage(a_re, a_im, log_dt, b_re, b_im, c_re, c_im):
    G, P, H = S5_GROUPS, S5_STATE, S5_GROUP
    n = DEPTH * G
    rep = lambda a: jnp.broadcast_to(a.transpose(0, 2, 1, 3)[:, :, None, :, :],
                                     (DEPTH, G, 2, 2, P)).reshape(n, 1, S5_NS)
    dt4 = rep(jnp.broadcast_to(log_dt[..., None], (DEPTH, 2, G, P)))
    c4 = lambda c: jnp.broadcast_to(c.transpose(0, 2, 3, 1, 4)[:, :, :, None, :, :],
                                    (DEPTH, G, H, 2, 2, P)).reshape(n, H, S5_NS)
    b4 = lambda b: c4(b.transpose(0, 1, 2, 4, 3))
    blk = lambda r, c: pl.BlockSpec((S5_PREP_GB, r, c), lambda i: (i, 0, 0))
    op = jax.ShapeDtypeStruct((n, S5_CL, S5_CL), BF)
    row = jax.ShapeDtypeStruct((n, 1, S5_NS), F32)
    return dict(
        kernel=_s5_prep_kernel, grid=(n // S5_PREP_GB,),
        in_specs=[blk(1, S5_NS), blk(1, S5_NS), blk(1, S5_NS), blk(H, S5_NS), blk(H, S5_NS), blk(H, S5_NS),
                  blk(H, S5_NS)],
        args=[rep(a_re), rep(a_im), dt4, b4(b_re), b4(b_im), c4(c_re), c4(c_im)],
        out_specs=[blk(S5_CL, S5_CL), blk(S5_CL, S5_NS), blk(S5_CL, S5_NS), blk(1, S5_NS), blk(1, S5_NS)],
        out_shape=[op, op, op, row, row])


def _token_slab_copies(hbm_ref, buf_ref, sem_ref, block, slot, to_hbm):
    copies = []
    for j in range(S5_T):
        hbm = hbm_ref.at[pl.ds(block * S5_CB, S5_CB), j, :]
        vmem = buf_ref.at[slot, j]
        src, dst = (vmem, hbm) if to_hbm else (hbm, vmem)
        copies.append(pltpu.make_async_copy(src, dst, sem_ref.at[slot, j]))
    return copies


def _s5_in_kernel(x_hbm, mod_ref, g_ref, w_ref, zt_ref, xbuf, sem, *, mod_rows):
    g = g_ref[...]
    rows = S5_CB // mod_rows
    i = pl.program_id(0)
    slot = i % 2
    fetch = functools.partial(_token_slab_copies, x_hbm, xbuf, sem, to_hbm=False)

    @pl.when(i == 0)
    def _():
        for c in fetch(0, 0):
            c.start()

    @pl.when(i + 1 < pl.num_programs(0))
    def _():
        for c in fetch(i + 1, 1 - slot):
            c.start()

    for c in fetch(i, slot):
        c.wait()

    def normed(j):
        x = xbuf[slot, j]
        parts = [_mod_norm(x[m * rows:(m + 1) * rows, :], g, mod_ref[m, 4:5, :], mod_ref[m, 3:4, :])
                 for m in range(mod_rows)]
        return parts[0] if mod_rows == 1 else jnp.concatenate(parts, axis=0)

    wt = w_ref[...].T
    for j in range(0, S5_T, 2):
        h = jnp.concatenate([normed(j), normed(j + 1)], axis=0).astype(BF)
        zt = lax.dot_general(wt, h, (((1,), (1,)), ((), ())), preferred_element_type=F32)
        zt_ref[j] = zt[:, :S5_CB]
        zt_ref[j + 1] = zt[:, S5_CB:]


def _s5_in(x, mod, norm_g, w_in, *, layer, seq_chunks):
    nc = x.shape[0] // S5_T
    xv = x.reshape(nc, S5_T, D_MODEL)
    mod_rows = 1 if mod.shape[0] == 1 else S5_CB // seq_chunks
    in_specs = [
        pl.BlockSpec(memory_space=pl.ANY),
        _mod_spec(mod.shape[0], 1, rows=mod_rows),
        _const_spec((None, 1, D_MODEL), (3 * layer + 1, 0, 0)),
        _const_spec((None, D_MODEL, S5_WIDTH), (layer, 0, 0)),
    ]
    return pl.pallas_call(
        functools.partial(_s5_in_kernel, mod_rows=mod_rows),
        grid=(nc // S5_CB,),
        in_specs=in_specs,
        out_specs=pl.BlockSpec((S5_T, S5_WIDTH, S5_CB), lambda i: (0, 0, i)),
        out_shape=jax.ShapeDtypeStruct((S5_T, S5_WIDTH, nc), F32),
        scratch_shapes=[pltpu.VMEM((2, S5_T, S5_CB, D_MODEL), F32), pltpu.SemaphoreType.DMA((2, S5_T))],
        compiler_params=_params(1),
        name="s5_in",
    )(xv, mod, norm_g, w_in)


def _s5_core_kernel(*refs, groups, seq_chunks, has_h0, want_final):
    refs = list(refs)
    zt_ref, mt_ref, win_ref, woutt_ref, at_re_ref, at_im_ref = refs[:6]
    refs = refs[6:]
    h0_ref = refs.pop(0) if has_h0 else None
    yt_ref = refs.pop(0)
    hf_ref = refs.pop(0) if want_final else None
    T, K = S5_T, seq_chunks
    half = S5_NS // 2
    nc = zt_ref.shape[2]
    n_seq = nc // K
    n_steps = K.bit_length() - 1
    fwd = lax.broadcasted_iota(jnp.int32, (nc, half), 1) < S5_STATE
    bwd = jnp.logical_not(fwd)
    kpos = lax.broadcasted_iota(jnp.int32, (nc, half), 0) % K
    first = (fwd & (kpos == 0)) | (bwd & (kpos == K - 1))
    valid = [(fwd & (kpos >= (1 << t))) | (bwd & (kpos < K - (1 << t))) for t in range(n_steps)]
    shift = lambda x, s: jnp.where(fwd, pltpu.roll(x, s, 0), pltpu.roll(x, nc - s, 0))
    if has_h0:
        chunk = lax.broadcasted_iota(jnp.int32, (nc, n_seq), 0)
        seq = lax.broadcasted_iota(jnp.int32, (nc, n_seq), 1)
        place_f = (chunk == seq * K).astype(F32)
        place_b = (chunk == seq * K + (K - 1)).astype(F32)
        fwd_full = jnp.concatenate([fwd, fwd], axis=1)
    if want_final:
        seq = lax.broadcasted_iota(jnp.int32, (LANES, nc), 0)
        chunk = lax.broadcasted_iota(jnp.int32, (LANES, nc), 1)
        pick_f = (chunk == seq * K + (K - 1)).astype(F32)
        pick_b = (chunk == seq * K).astype(F32)
        fwd_row = lax.broadcasted_iota(jnp.int32, (1, S5_NS), 1) % half < S5_STATE

    xts, vs, h0s = [], [], []
    for g in range(groups):
        r0 = g * S5_GROUP
        xt = jnp.concatenate([zt_ref[j, r0:r0 + S5_GROUP, :] for j in range(T)], axis=0).astype(BF)
        xts.append(xt)
        vs.append(lax.dot_general(xt, win_ref[g], (((0,), (0,)), ((), ())), preferred_element_type=F32))
        if has_h0:
            h0s.append(jnp.where(fwd_full, _dot_exact(place_f, h0_ref[g]), _dot_exact(place_b, h0_ref[g])))
    yield
    s_ins, fins = [], []
    for g in range(groups):
        vr, vi = vs[g][:, :half], vs[g][:, half:]
        cr, ci = at_re_ref[g, :, :half], at_im_ref[g, :, :half]
        ar, ai = cr, ci
        hr, hi = (h0s[g][:, :half], h0s[g][:, half:]) if has_h0 else (0.0, 0.0)
        sr = jnp.where(first, hr, shift(vr, 1))
        si = jnp.where(first, hi, shift(vi, 1))
        for t in range(n_steps):
            rr = jnp.where(valid[t], shift(sr, 1 << t), 0.0)
            ri = jnp.where(valid[t], shift(si, 1 << t), 0.0)
            sr, si = sr + (cr * rr - ci * ri), si + (cr * ri + ci * rr)
            cr, ci = cr * cr - ci * ci, 2.0 * (cr * ci)
        s_ins.append(jnp.concatenate([sr, si], axis=1).astype(BF))
        if want_final:
            fins.append(jnp.concatenate([ar * sr - ai * si + vr, ar * si + ai * sr + vi], axis=1))
    yield
    for g in range(groups):
        r0 = g * S5_GROUP
        y = _dot(mt_ref[g], xts[g]) + lax.dot_general(woutt_ref[g], s_ins[g], (((1,), (1,)), ((), ())),
                                                      preferred_element_type=F32)
        for i in range(T):
            yt_ref[i, r0:r0 + S5_GROUP, :] = y[i * S5_GROUP:(i + 1) * S5_GROUP, :]
        if want_final:
            hf_ref[g] = jnp.where(fwd_row, _dot_exact(pick_f, fins[g]), _dot_exact(pick_b, fins[g]))


def _s5_core_stage(zt, ops, *, layer, seq_chunks, steps, h0=None, want_final=False):
    nc = zt.shape[2]
    gb = S5_GROUPS // steps
    rows = gb * S5_GROUP
    gspec = lambda r, c: pl.BlockSpec((gb, r, c), lambda i: (layer * steps + i, 0, 0))
    slab = pl.BlockSpec((S5_T, rows, nc), lambda i: (0, i, 0))
    in_specs = [slab, gspec(S5_CL, S5_CL), gspec(S5_CL, S5_NS), gspec(S5_CL, S5_NS),
                gspec(1, S5_NS), gspec(1, S5_NS)]
    args = [zt, ops["mt"], ops["win"], ops["woutt"], ops["at_re"], ops["at_im"]]
    if h0 is not None:
        in_specs.append(pl.BlockSpec((gb, h0.shape[1], S5_NS), lambda i: (i, 0, 0)))
        args.append(h0)
    out_specs = [slab]
    out_shape = [jax.ShapeDtypeStruct(zt.shape, F32)]
    if want_final:
        assert nc // seq_chunks <= LANES
        out_specs.append(pl.BlockSpec((gb, LANES, S5_NS), lambda i: (i, 0, 0)))
        out_shape.append(jax.ShapeDtypeStruct((S5_GROUPS, LANES, S5_NS), F32))
    kern = functools.partial(_s5_core_kernel, groups=gb, seq_chunks=seq_chunks, has_h0=h0 is not None,
                             want_final=want_final)
    return dict(kernel=kern, grid=(steps,), in_specs=in_specs, args=args, out_specs=out_specs, out_shape=out_shape)


def _s5_out_kernel(yt_ref, zt_ref, d_ref, wglut_ref, wa_ref, o_hbm, obuf, sem):
    i = pl.program_id(0)
    last = pl.num_programs(0) - 1
    slot = i % 2
    store = functools.partial(_token_slab_copies, o_hbm, obuf, sem, to_hbm=True)

    @pl.when(i >= 2)
    def _():
        for c in store(i - 2, slot):
            c.wait()

    d = d_ref[...]
    for j in range(0, S5_T, 2):
        pre = jnp.concatenate([yt_ref[j] + d * zt_ref[j], yt_ref[j + 1] + d * zt_ref[j + 1]], axis=1)
        ya = jax.nn.gelu(pre)
        ya = ya * jax.nn.sigmoid(_dot(wglut_ref[...], ya.astype(BF)))
        pa = lax.dot_general(ya.astype(BF), wa_ref[...], (((0,), (0,)), ((), ())), preferred_element_type=F32)
        obuf[slot, j] = pa[:S5_CB, :]
        obuf[slot, j + 1] = pa[S5_CB:, :]
    for c in store(i, slot):
        c.start()

    @pl.when(i == last)
    def _():
        for c in store(i, slot):
            c.wait()

    @pl.when((i == last) & (i >= 1))
    def _():
        for c in store(i - 1, 1 - slot):
            c.wait()


def _s5_out(yt, zt, s5_d_col, w_glu_t, w_br_a, *, layer):
    nc = yt.shape[2]
    slab = pl.BlockSpec((S5_T, S5_WIDTH, S5_CB), lambda i: (0, 0, i))
    out = pl.pallas_call(
        _s5_out_kernel,
        grid=(nc // S5_CB,),
        in_specs=[slab, slab,
                  _const_spec((None, S5_WIDTH, 1), (layer, 0, 0)),
                  _const_spec((None, S5_WIDTH, S5_WIDTH), (layer, 0, 0)),
                  _const_spec((None, S5_WIDTH, D_MODEL), (layer, 0, 0))],
        out_specs=pl.BlockSpec(memory_space=pl.ANY),
        out_shape=jax.ShapeDtypeStruct((nc, S5_T, D_MODEL), F32),
        scratch_shapes=[pltpu.VMEM((2, S5_T, S5_CB, D_MODEL), F32), pltpu.SemaphoreType.DMA((2, S5_T))],
        compiler_params=_params(1),
        name="s5_out",
    )(yt, zt, s5_d_col, w_glu_t, w_br_a)
    return out.reshape(nc * S5_T, D_MODEL)


def _mix_out_kernel(x_ref, mod_ref, g_ref, pa_ref, yb_ref, yc_ref, wb_ref, wc_ref, wgate_ref, bgate_ref,
                    wout_ref, o_ref):
    x = x_ref[...]
    h = _mod_norm(x, g_ref[...], mod_ref[0, 4:5, :], mod_ref[0, 3:4, :]).astype(BF)

    def gate(k):
        lo = k * D_MODEL
        return jax.nn.sigmoid(_dot(h, wgate_ref[:, lo:lo + D_MODEL]) + bgate_ref[:, lo:lo + D_MODEL])

    merged = gate(0) * pa_ref[...]
    merged = merged + gate(1) * _dot(yb_ref[...], wb_ref[...])
    merged = merged + gate(2) * _dot(yc_ref[...], wc_ref[...])
    y = _dot(merged.astype(BF), wout_ref[...])
    o_ref[...] = x + mod_ref[0, 5:6, :] * y


def _mix_out(x, mod, norm_g, pa, yb, yc, w_b, w_c, w_gate, b_gate, w_out, *, layer, seq_tokens):
    n = x.shape[0]
    tm = FFN_TM
    tok = lambda c: pl.BlockSpec((tm, c), lambda i: (i, 0))
    return pl.pallas_call(
        _mix_out_kernel,
        grid=(n // tm,),
        in_specs=[
            tok(D_MODEL),
            _mod_spec(mod.shape[0], seq_tokens // tm),
            _const_spec((None, 1, D_MODEL), (3 * layer + 1, 0, 0)),
            tok(D_MODEL), tok(CONV_WIDTH), tok(SG_WIDTH),
            _const_spec((None, CONV_WIDTH, D_MODEL), (layer, 0, 0)),
            _const_spec((None, SG_WIDTH, D_MODEL), (layer, 0, 0)),
            _const_spec((None, D_MODEL, 3 * D_MODEL), (layer, 0, 0)),
            _const_spec((None, 1, 3 * D_MODEL), (layer, 0, 0)),
            _const_spec((None, D_MODEL, D_MODEL), (layer, 0, 0)),
        ],
        out_specs=tok(D_MODEL),
        out_shape=jax.ShapeDtypeStruct((n, D_MODEL), F32),
        compiler_params=_params(1),
        name="mix_out",
    )(x, mod, norm_g, pa, yb, yc, w_b, w_c, w_gate, b_gate, w_out)


def _grid_pos_embed(n_tokens, dim):
    rows = n_tokens // GRID_W
    rr, cc = jnp.meshgrid(jnp.arange(rows, dtype=F32), jnp.arange(GRID_W, dtype=F32), indexing='ij')
    quarter = dim // 4
    omega = 1.0 / (10000.0 ** (jnp.arange(quarter, dtype=F32) / quarter))

    def emb(p):
        ang = p.reshape(-1)[:, None] * omega[None, :]
        return jnp.concatenate([jnp.sin(ang), jnp.cos(ang)], axis=-1)

    return jnp.concatenate([emb(rr), emb(cc)], axis=-1)


def kernel(x_prompt, x_sample, state_ssm, c, c_ctx, w_mod, b_mod, norm_g, ffn_w1, ffn_w2, w_in, w_gate, b_gate,
           s5_a_re, s5_a_im, s5_log_dt, s5_b_re, s5_b_im, s5_c_re, s5_c_im, s5_d, s5_w_glu, w_br_a, conv_w,
           conv_b, conv_ln_g, conv_ln_b, w_br_b, sg_ln_g, sg_ln_b, sg_w, sg_b, w_br_c, w_out, final_g):
    batch, seq, _ = x_prompt.shape
    dec_batch, dec_seq, _ = x_sample.shape
    assert (batch * seq) % TOK_BLOCK == 0 and TOK_BLOCK % seq == 0 and dec_seq % TOK_BLOCK == 0
    assert seq % SG_CHUNK == 0 and seq % S5_T == 0 and dec_seq % S5_T == 0
    ctx_chunks, smp_chunks = seq // S5_T, dec_seq // S5_T
    assert S5_CB % ctx_chunks == 0 and S5_CB % smp_chunks == 0
    assert ctx_chunks & (ctx_chunks - 1) == 0 and smp_chunks & (smp_chunks - 1) == 0
    assert 1 + dec_batch <= MOD_ROWS

    cond = jnp.zeros((MOD_ROWS, D_MODEL), F32).at[0].set(c_ctx).at[1:1 + dec_batch].set(c)
    [mod_all], prep_out = _call([_modulation_stage(cond, w_mod, b_mod),
                                 _s5_prep_stage(s5_a_re, s5_a_im, s5_log_dt, s5_b_re, s5_b_im, s5_c_re, s5_c_im)],
                                "mod_prep")
    mod_all = mod_all.reshape(DEPTH, MOD_ROWS, N_MOD, D_MODEL)
    s5_ops = dict(zip(S5_OPS, prep_out))
    pos = _grid_pos_embed(dec_seq, D_MODEL)

    ffn_w = {(0, 0): (ffn_w1[0, 0].astype(BF), ffn_w2[0, 0].astype(BF))}
    mixer_w = dict(w_in=w_in, w_gate=w_gate, w_out=w_out, w_a=w_br_a, w_b=w_br_b, w_c=w_br_c)
    w_glu_t = s5_w_glu.transpose(0, 2, 1).astype(BF)
    norm_rows = norm_g.reshape(DEPTH * 3, 1, D_MODEL)
    rows = lambda a: a.reshape(DEPTH, 1, -1)
    sgw_cat = sg_w.transpose(0, 2, 1, 3).reshape(DEPTH, SG_CHUNK, SG_HEADS * SG_CHUNK).astype(BF)
    sgb_full = jnp.repeat(sg_b.transpose(0, 2, 1), SG_HEAD_DIM, axis=2)
    s5_d_col = s5_d.reshape(DEPTH, S5_WIDTH, 1)

    conv_w8 = jnp.broadcast_to(conv_w[:, :, None, :], (DEPTH, CONV_K, SUBLANES, CONV_WIDTH))

    ctx = dict(n_seq=batch, seq_len=seq, mod_rows=slice(0, 1), mod_tokens=batch * seq)
    smp = dict(n_seq=dec_batch, seq_len=dec_seq, mod_rows=slice(1, 1 + dec_batch), mod_tokens=dec_seq)
    x_ctx = x_prompt.reshape(batch * seq, D_MODEL)
    x_smp = x_sample.reshape(dec_batch * dec_seq, D_MODEL)
    ffn_steps = x_ctx.shape[0] // FFN_TM
    assert x_smp.shape[0] // FFN_TM == ffn_steps and S5_GROUPS % ffn_steps == 0

    assert DEPTH * S5_GROUPS // S5_PREP_GB == ffn_steps

    def ffn_stage(gr, x, l, which, **kw):
        return _ffn_stage(x, mod_all[l, gr["mod_rows"]], norm_rows, *ffn_w[l, which], layer=l, which=which,
                          seq_tokens=gr["mod_tokens"], **kw)

    def cast_stages(l, which):
        if l >= DEPTH:
            return []
        return [_cast_stage(ffn_w1, (l, which), ffn_steps), _cast_stage(ffn_w2, (l, which), ffn_steps)]

    def keep_casts(l, which, outs):
        if outs:
            ffn_w[l, which] = (outs[0][0], outs[1][0])

    def branches_in(gr, x, l):
        mod = mod_all[l, gr["mod_rows"]]
        yb, yc = _mix_in(x, mod, norm_rows, mixer_w["w_in"], conv_w8, rows(conv_b), rows(conv_ln_g), rows(conv_ln_b),
                         rows(sg_ln_g), rows(sg_ln_b), sgw_cat, sgb_full, layer=l, seq_len=gr["seq_len"])
        zt = _s5_in(x, mod, norm_rows, mixer_w["w_in"], layer=l, seq_chunks=gr["seq_len"] // S5_T)
        return yb, yc, zt

    def s5_stage(gr, zt, l, **kw):
        return _s5_core_stage(zt, s5_ops, layer=l, seq_chunks=gr["seq_len"] // S5_T, steps=ffn_steps, **kw)

    def branches_out(gr, x, l, yb, yc, zt, yt):
        pa = _s5_out(yt, zt, s5_d_col, w_glu_t, mixer_w["w_a"], layer=l)
        return _mix_out(x, mod_all[l, gr["mod_rows"]], norm_rows, pa, yb, yc, mixer_w["w_b"], mixer_w["w_c"],
                        mixer_w["w_gate"], rows(b_gate), mixer_w["w_out"], layer=l, seq_tokens=gr["mod_tokens"])

    ctx_states = []
    for l in range(DEPTH):
        riders = []
        if l == 0:
            riders = cast_stages(0, 1) + [_cast_stage(w.reshape(-1, w.shape[-1]), (), ffn_steps)
                                          for w in mixer_w.values()]
        [x_ctx], *outs = _call([ffn_stage(ctx, x_ctx, l, 0)] + riders, "ffn")
        if l == 0:
            keep_casts(0, 1, outs[:2])
            mixer_w = {k: o[0].reshape(mixer_w[k].shape) for k, o in zip(mixer_w, outs[2:])}
        yb_c, yc_c, zt_c = branches_in(ctx, x_ctx, l)
        [x_smp], [yt_c, hf], *outs = _call([ffn_stage(smp, x_smp, l, 0, pos=pos if l == 0 else None),
                                             s5_stage(ctx, zt_c, l, want_final=True)] + cast_stages(l + 1, 0),
                                            "ffn_s5")
        keep_casts(l + 1, 0, outs)
        hf = hf[:, :batch, :].reshape(S5_GROUPS, batch, 2, 2, S5_STATE)
        ctx_states.append(hf.transpose(1, 3, 0, 4, 2))
        x_ctx = branches_out(ctx, x_ctx, l, yb_c, yc_c, zt_c, yt_c)
        yb_s, yc_s, zt_s = branches_in(smp, x_smp, l)
        h0 = state_ssm[:, l].transpose(2, 0, 4, 1, 3).reshape(S5_GROUPS, dec_batch, S5_NS)
        final = dict(final_g=final_g) if l == DEPTH - 1 else {}
        [x_ctx], [yt_s], *outs = _call([ffn_stage(ctx, x_ctx, l, 1, **final), s5_stage(smp, zt_s, l, h0=h0)]
                                       + cast_stages(l + 1, 1), "ffn_s5")
        keep_casts(l + 1, 1, outs)
        x_smp = branches_out(smp, x_smp, l, yb_s, yc_s, zt_s, yt_s)
        [x_smp], = _call([ffn_stage(smp, x_smp, l, 1, **final)], "ffn")
    y_prompt = x_ctx.reshape(batch, seq, D_MODEL)
    y_sample = x_smp.reshape(dec_batch, dec_seq, D_MODEL)
    new_state_ssm = jnp.stack(ctx_states, axis=1)
    return (y_prompt, y_sample, new_state_ssm)
```

```python
import functools
import inspect

import jax
import jax.numpy as jnp
from jax import lax
from jax.experimental import pallas as pl
from jax.experimental.pallas import tpu as pltpu

D_MODEL = 1024
DEPTH = 2
GRID_W = 64
D_FF = 2816
S5_WIDTH = 512
S5_GROUP = 16
S5_GROUPS = 32
S5_STATE = 64
CONV_WIDTH = 256
CONV_K = 31
SG_WIDTH = 256
SG_CHUNK = 128
SG_HEADS = 4
SG_HEAD_DIM = SG_WIDTH // SG_HEADS
BRANCH_COLS = 2 * CONV_WIDTH
IN_COLS = S5_WIDTH + 2 * BRANCH_COLS
N_MOD = 9
EPS = 1e-6

LANES = 128
SUBLANES = 8
S5_T = 16
S5_CL = S5_T * S5_GROUP
S5_NS = 4 * S5_STATE
S5_PREP_GB = 4
S5_POW_ROWS = SUBLANES * (S5_T // SUBLANES + 1)
S5_CB = 128
MOD_ROWS = 16
MOD_TN = 1152
TOK_BLOCK = 1024
FFN_TM = 512
FFN_SUBTILES = 4
CONV_RC = 32
CONV_PAD = 16
CONV_SHIFT_ROWS = SUBLANES * ((CONV_PAD - CONV_K // 2 + CONV_K - 1) // SUBLANES)
VMEM_LIMIT = 56 * 1024 * 1024

BF = jnp.bfloat16
F32 = jnp.float32
HIGHEST = lax.Precision.HIGHEST


def _dot(a, b):
    return jnp.dot(a, b, preferred_element_type=F32)


def _dot_exact(a, b):
    return jnp.dot(a, b, preferred_element_type=F32, precision=HIGHEST)


def _const_spec(block, index):
    return pl.BlockSpec(block, lambda *_: index, pipeline_mode=pl.Buffered(1))


def _params(n_axes=1):
    return pltpu.CompilerParams(dimension_semantics=("arbitrary",) * n_axes,
                                vmem_limit_bytes=VMEM_LIMIT)


def _call(stages, name):
    grid = stages[0]["grid"]
    assert all(s["grid"] == grid for s in stages)
    n_in = [len(s["in_specs"]) for s in stages]
    n_out = [len(s["out_specs"]) for s in stages]
    total_in, total_out = sum(n_in), sum(n_out)

    def body(*refs):
        bound, i, o = [], 0, 0
        for s, ni, no in zip(stages, n_in, n_out):
            bound.append((s["kernel"], refs[i:i + ni] + refs[total_in + o:total_in + o + no]))
            i, o = i + ni, o + no
        live = [run for run in (kern(*r) for kern, r in reversed(bound)) if inspect.isgenerator(run)]
        while live:
            for gen in list(live):
                if next(gen, "done") == "done":
                    live.remove(gen)

    outs = pl.pallas_call(
        body,
        grid=grid,
        in_specs=[spec for s in stages for spec in s["in_specs"]],
        out_specs=[spec for s in stages for spec in s["out_specs"]],
        out_shape=[shape for s in stages for shape in s["out_shape"]],
        compiler_params=_params(len(grid)),
        name=name,
    )(*[a for s in stages for a in s["args"]])
    split, o = [], 0
    for no in n_out:
        split.append(list(outs[o:o + no]))
        o += no
    return split


def _mod_norm(x, g, sc, sh):
    var = jnp.mean(x * x, axis=-1, keepdims=True)
    return (x * lax.rsqrt(var + EPS) * g) * (1.0 + sc) + sh


def _layernorm(x, g, b):
    mu = jnp.mean(x, axis=-1, keepdims=True)
    xc = x - mu
    var = jnp.mean(xc * xc, axis=-1, keepdims=True)
    return xc * lax.rsqrt(var + EPS) * g + b


def _mod_spec(n_mod, blocks_per_mod, rows=1):
    if n_mod == 1:
        return pl.BlockSpec((1, N_MOD, D_MODEL), lambda i: (0, 0, 0))
    return pl.BlockSpec((rows, N_MOD, D_MODEL), lambda i: (i // blocks_per_mod, 0, 0))


def _mod_kernel(c_ref, w_ref, b_ref, o_ref):
    c = c_ref[...]
    a = (c * jax.nn.sigmoid(c)).astype(BF)
    o_ref[...] = _dot(a, w_ref[...].astype(BF)) + b_ref[...]


def _modulation_stage(cond, w_mod, b_mod):
    n = N_MOD * D_MODEL
    tiles = n // MOD_TN
    tile = lambda i: (i // tiles, 0, i % tiles)
    return dict(
        kernel=_mod_kernel, grid=(DEPTH * tiles,), args=[cond, w_mod, b_mod.reshape(DEPTH, 1, n)],
        in_specs=[pl.BlockSpec((MOD_ROWS, D_MODEL), lambda i: (0, 0)),
                  pl.BlockSpec((None, D_MODEL, MOD_TN), tile),
                  pl.BlockSpec((None, 1, MOD_TN), tile)],
        out_specs=[pl.BlockSpec((None, MOD_ROWS, MOD_TN), tile)],
        out_shape=[jax.ShapeDtypeStruct((DEPTH, MOD_ROWS, n), F32)])


def _ffn_kernel(*refs, mod_base, add_pos, final):
    refs = list(refs)
    x_ref = refs.pop(0)
    pos_ref = refs.pop(0) if add_pos else None
    mod_ref, g_ref, w1g_ref, w1u_ref, w2_ref = refs[:5]
    refs = refs[5:]
    fg_ref = refs.pop(0) if final else None
    o_ref = refs.pop(0)

    sh = mod_ref[0, mod_base:mod_base + 1, :]
    sc = mod_ref[0, mod_base + 1:mod_base + 2, :]
    gt = mod_ref[0, mod_base + 2:mod_base + 3, :]
    rows = x_ref.shape[0] // FFN_SUBTILES
    sub = [slice(r * rows, (r + 1) * rows) for r in range(FFN_SUBTILES)]
    xs, hs, gs, accs = [], [], [], []
    for r in sub:
        x = x_ref[r, :] + pos_ref[r, :] if add_pos else x_ref[r, :]
        xs.append(x)
        hs.append(_mod_norm(x, g_ref[...], sc, sh).astype(BF))
        gs.append(_dot(hs[-1], w1g_ref[...]))
    yield
    for h, g in zip(hs, gs):
        u = _dot(h, w1u_ref[...])
        accs.append((g * jax.nn.sigmoid(g) * u).astype(BF))
    yield
    for r, x, a in zip(sub, xs, accs):
        y = _dot(a, w2_ref[...])
        xn = x + (0.5 * gt) * y
        if final:
            var = jnp.mean(xn * xn, axis=-1, keepdims=True)
            xn = xn * lax.rsqrt(var + EPS) * fg_ref[...]
        o_ref[r, :] = xn
    yield


def _cast_kernel(x_ref, o_ref):
    o_ref[...] = x_ref[...].astype(o_ref.dtype)


def _cast_stage(w, index, steps):
    rows, cols = w.shape[-2:]
    lead = (None,) * len(index)
    return dict(kernel=_cast_kernel, grid=(steps,), args=[w],
                in_specs=[pl.BlockSpec(lead + (rows // steps, cols), lambda i: tuple(index) + (i, 0))],
                out_specs=[pl.BlockSpec((rows // steps, cols), lambda i: (i, 0))],
                out_shape=[jax.ShapeDtypeStruct((rows, cols), BF)])


def _ffn_stage(x, mod, norm_g, w1, w2, *, layer, which, seq_tokens, pos=None, final_g=None):
    n = x.shape[0]
    tm = FFN_TM
    in_specs = [pl.BlockSpec((tm, D_MODEL), lambda i: (i, 0))]
    args = [x]
    if pos is not None:
        pos_blocks = pos.shape[0] // tm
        in_specs.append(pl.BlockSpec((tm, D_MODEL), lambda i: (i % pos_blocks, 0)))
        args.append(pos)
    in_specs += [
        _mod_spec(mod.shape[0], seq_tokens // tm),
        _const_spec((None, 1, D_MODEL), (3 * layer + 2 * which, 0, 0)),
        _const_spec((D_MODEL, D_FF), (0, 0)),
        _const_spec((D_MODEL, D_FF), (0, 1)),
        _const_spec((D_FF, D_MODEL), (0, 0)),
    ]
    args += [mod, norm_g, w1, w1, w2]
    if final_g is not None:
        in_specs.append(_const_spec((1, D_MODEL), (0, 0)))
        args.append(final_g.reshape(1, D_MODEL))
    kern = functools.partial(_ffn_kernel, mod_base=6 * which, add_pos=pos is not None,
                             final=final_g is not None)
    return dict(kernel=kern, grid=(n // tm,), in_specs=in_specs, args=args,
                out_specs=[pl.BlockSpec((tm, D_MODEL), lambda i: (i, 0))],
                out_shape=[jax.ShapeDtypeStruct((n, D_MODEL), F32)])


def _mix_in_kernel(x_ref, mod_ref, g_ref, winb_ref, winc_ref, cw_ref, cb_ref, clg_ref, clb_ref,
                   sglg_ref, sglb_ref, sgw_ref, sgb_ref, yb_ref, yc_ref, pad_ref, conv_ref, *, seq_len):
    n_seq = TOK_BLOCK // seq_len
    x = x_ref[...]
    h = _mod_norm(x, g_ref[...], mod_ref[0, 4:5, :], mod_ref[0, 3:4, :]).astype(BF)

    zb = _dot(h, winb_ref[...])
    gl = zb[:, :CONV_WIDTH] * jax.nn.sigmoid(zb[:, CONV_WIDTH:])
    zeros = jnp.zeros((CONV_PAD, CONV_WIDTH), F32)
    shifted_rows = seq_len + CONV_SHIFT_ROWS
    padded_rows = seq_len + 2 * CONV_PAD
    for s in range(n_seq):
        padded = jnp.concatenate([zeros, gl[s * seq_len:(s + 1) * seq_len, :], zeros], axis=0)
        pad_ref[0, s] = padded[:shifted_rows, :]
        for r in range(1, SUBLANES):
            pad_ref[r, s] = pltpu.roll(padded, padded_rows - r, 0)[:shifted_rows, :]
    first = CONV_PAD - CONV_K // 2
    tiles = CONV_RC // SUBLANES
    n_q = CONV_SHIFT_ROWS // SUBLANES + 1
    for s in range(n_seq):
        def conv_rows(r, carry, s=s):
            r0 = pl.multiple_of(r * CONV_RC, CONV_RC)
            accs = [None] * n_q
            for shift in range(SUBLANES):
                span = pad_ref[shift, s, pl.ds(r0, CONV_RC + CONV_SHIFT_ROWS), :]
                span = span.reshape(tiles + n_q - 1, SUBLANES, CONV_WIDTH)
                for q in range(n_q):
                    k = SUBLANES * q + shift - first
                    if 0 <= k < CONV_K:
                        term = cw_ref[k] * span[q:q + tiles]
                        accs[q] = term if accs[q] is None else accs[q] + term
            acc = (accs[0] + accs[1]) + (accs[2] + accs[3])
            conv_ref[pl.ds(pl.multiple_of(s * seq_len + r0, CONV_RC), CONV_RC), :] = acc.reshape(CONV_RC, CONV_WIDTH)
            return carry
        lax.fori_loop(0, seq_len // CONV_RC, conv_rows, 0)
    y = _layernorm(conv_ref[...] + cb_ref[...], clg_ref[...], clb_ref[...])
    yb_ref[...] = (y * jax.nn.sigmoid(y)).astype(BF)

    zc = jax.nn.gelu(_dot(h, winc_ref[...]))
    u = zc[:, :SG_WIDTH]
    v = _layernorm(zc[:, SG_WIDTH:], sglg_ref[...], sglb_ref[...]).astype(BF)
    head = lax.broadcasted_iota(jnp.int32, (SG_CHUNK, SG_WIDTH), 1) // SG_HEAD_DIM
    zero = jnp.zeros((SG_CHUNK, SG_WIDTH), BF)
    for n in range(TOK_BLOCK // SG_CHUNK):
        vn = v[n * SG_CHUNK:(n + 1) * SG_CHUNK, :]
        vcat = jnp.concatenate([jnp.where(head == hh, vn, zero) for hh in range(SG_HEADS)], axis=0)
        sgate = _dot(sgw_ref[...], vcat) + sgb_ref[...]
        yc_ref[n * SG_CHUNK:(n + 1) * SG_CHUNK, :] = (u[n * SG_CHUNK:(n + 1) * SG_CHUNK, :] * sgate).astype(BF)


def _mix_in(x, mod, norm_g, w_in, conv_w, conv_b, conv_ln_g, conv_ln_b, sg_ln_g, sg_ln_b, sgw_cat, sgb_full,
            *, layer, seq_len):
    n = x.shape[0]
    n_seq = TOK_BLOCK // seq_len
    vec = lambda width: _const_spec((None, 1, width), (layer, 0, 0))
    kern = functools.partial(_mix_in_kernel, seq_len=seq_len)
    return pl.pallas_call(
        kern,
        grid=(n // TOK_BLOCK,),
        in_specs=[
            pl.BlockSpec((TOK_BLOCK, D_MODEL), lambda i: (i, 0)),
            _mod_spec(mod.shape[0], max(seq_len // TOK_BLOCK, 1)),
            _const_spec((None, 1, D_MODEL), (3 * layer + 1, 0, 0)),
            _const_spec((None, D_MODEL, BRANCH_COLS), (layer, 0, S5_WIDTH // BRANCH_COLS)),
            _const_spec((None, D_MODEL, BRANCH_COLS), (layer, 0, S5_WIDTH // BRANCH_COLS + 1)),
            _const_spec((None, CONV_K, SUBLANES, CONV_WIDTH), (layer, 0, 0, 0)),
            vec(CONV_WIDTH), vec(CONV_WIDTH), vec(CONV_WIDTH), vec(SG_WIDTH), vec(SG_WIDTH),
            _const_spec((None, SG_CHUNK, SG_HEADS * SG_CHUNK), (layer, 0, 0)),
            _const_spec((None, SG_CHUNK, SG_WIDTH), (layer, 0, 0)),
        ],
        out_specs=[
            pl.BlockSpec((TOK_BLOCK, CONV_WIDTH), lambda i: (i, 0)),
            pl.BlockSpec((TOK_BLOCK, SG_WIDTH), lambda i: (i, 0)),
        ],
        out_shape=[
            jax.ShapeDtypeStruct((n, CONV_WIDTH), BF),
            jax.ShapeDtypeStruct((n, SG_WIDTH), BF),
        ],
        scratch_shapes=[pltpu.VMEM((SUBLANES, n_seq, seq_len + CONV_SHIFT_ROWS, CONV_WIDTH), F32),
                        pltpu.VMEM((TOK_BLOCK, CONV_WIDTH), F32)],
        compiler_params=_params(1),
        name="mix_in",
    )(x, mod, norm_g, w_in, w_in, conv_w, conv_b, conv_ln_g, conv_ln_b, sg_ln_g, sg_ln_b, sgw_cat, sgb_full)


def _s5_prep_kernel(arow_re_ref, arow_im_ref, dtrow_ref, b4_re_ref, b4_im_ref, c4_re_ref, c4_im_ref,
                    mt_ref, win_ref, woutt_ref, at_re_ref, at_im_ref):
    T, H = S5_T, S5_GROUP
    lanes = 2 * S5_CL
    lane = lax.broadcasted_iota(jnp.int32, (1, S5_NS), 1)
    lane_im = lane >= 2 * S5_STATE
    lane_bwd = (lane // S5_STATE) % 2 == 1
    e_col = lax.broadcasted_iota(jnp.int32, (S5_POW_ROWS, 1), 0).astype(F32)
    zeros = jnp.zeros((H, S5_NS), F32)
    for g in range(S5_PREP_GB):
        a_re, a_im = arow_re_ref[g], arow_im_ref[g]
        dt = jnp.exp(dtrow_ref[g])
        mag = jnp.exp(a_re * dt * e_col)
        ang = a_im * dt * e_col
        pw_re, pw_im = mag * jnp.cos(ang), mag * jnp.sin(ang)
        den = a_re * a_re + a_im * a_im
        nr, ni = pw_re[1:2, :] - 1.0, pw_im[1:2, :]
        q_re = (nr * a_re + ni * a_im) / den
        q_im = (ni * a_re - nr * a_im) / den
        b_re, b_im = b4_re_ref[g], b4_im_ref[g]
        bb_re = q_re * b_re - q_im * b_im
        bb_im = q_re * b_im + q_im * b_re
        c_re, c_im = c4_re_ref[g], c4_im_ref[g]

        def power(e_f, e_b, pw_re=pw_re, pw_im=pw_im):
            return (jnp.where(lane_bwd, pw_re[e_b:e_b + 1, :], pw_re[e_f:e_f + 1, :]),
                    jnp.where(lane_bwd, pw_im[e_b:e_b + 1, :], pw_im[e_f:e_f + 1, :]))

        def times_b(e_f, e_b, bb_re=bb_re, bb_im=bb_im, power=power):
            p_re, p_im = power(e_f, e_b)
            return jnp.where(lane_im, p_re * bb_im + p_im * bb_re, p_re * bb_re - p_im * bb_im)

        def times_c(e_f, e_b, c_re=c_re, c_im=c_im, power=power):
            p_re, p_im = power(e_f, e_b)
            return jnp.where(lane_im, -(c_re * p_im + c_im * p_re), c_re * p_re - c_im * p_im)

        win_ref[g] = jnp.concatenate([times_b(T - 1 - j, j) for j in range(T)], axis=0).astype(BF)
        woutt_ref[g] = jnp.concatenate([times_c(i + 1, T - i) for i in range(T)], axis=0).astype(BF)
        blocks = []
        for b in range(2 * T - 1):
            m = T - 1 - b
            w = times_b(abs(m), abs(m))
            if m > 0:
                w = jnp.where(lane_bwd, 0.0, w)
            elif m < 0:
                w = jnp.where(lane_bwd, w, 0.0)
            blocks.append(w)
        ystack = jnp.concatenate(blocks + [zeros], axis=0)
        c_signed = jnp.where(lane_im, -c_im, c_re)
        krev = lax.dot_general(c_signed, ystack, (((1,), (1,)), ((), ())), preferred_element_type=F32,
                               precision=HIGHEST)
        mt_ref[g] = jnp.concatenate(
            [pltpu.roll(krev, (lanes - H * (T - 1 - i)) % lanes, 1)[:, :S5_CL] for i in range(T)],
            axis=0).astype(BF)
        at_re_ref[g] = pw_re[T:T + 1, :]
        at_im_ref[g] = pw_im[T:T + 1, :]


S5_OPS = ("mt", "win", "woutt", "at_re", "at_im")


def _s5_prep_stage(a_re, a_im, log_dt, b_re, b_im, c_re, c_im):
    G, P, H = S5_GROUPS, S5_STATE, S5_GROUP
    n = DEPTH * G
    rep = lambda a: jnp.broadcast_to(a.transpose(0, 2, 1, 3)[:, :, None, :, :],
                                     (DEPTH, G, 2, 2, P)).reshape(n, 1, S5_NS)
    dt4 = rep(jnp.broadcast_to(log_dt[..., None], (DEPTH, 2, G, P)))
    c4 = lambda c: jnp.broadcast_to(c.transpose(0, 2, 3, 1, 4)[:, :, :, None, :, :],
                                    (DEPTH, G, H, 2, 2, P)).reshape(n, H, S5_NS)
    b4 = lambda b: c4(b.transpose(0, 1, 2, 4, 3))
    blk = lambda r, c: pl.BlockSpec((S5_PREP_GB, r, c), lambda i: (i, 0, 0))
    op = jax.ShapeDtypeStruct((n, S5_CL, S5_CL), BF)
    row = jax.ShapeDtypeStruct((n, 1, S5_NS), F32)
    return dict(
        kernel=_s5_prep_kernel, grid=(n // S5_PREP_GB,),
        in_specs=[blk(1, S5_NS), blk(1, S5_NS), blk(1, S5_NS), blk(H, S5_NS), blk(H, S5_NS), blk(H, S5_NS),
                  blk(H, S5_NS)],
        args=[rep(a_re), rep(a_im), dt4, b4(b_re), b4(b_im), c4(c_re), c4(c_im)],
        out_specs=[blk(S5_CL, S5_CL), blk(S5_CL, S5_NS), blk(S5_CL, S5_NS), blk(1, S5_NS), blk(1, S5_NS)],
        out_shape=[op, op, op, row, row])


def _token_slab_copies(hbm_ref, buf_ref, sem_ref, block, slot, to_hbm):
    copies = []
    for j in range(S5_T):
        hbm = hbm_ref.at[pl.ds(block * S5_CB, S5_CB), j, :]
        vmem = buf_ref.at[slot, j]
        src, dst = (vmem, hbm) if to_hbm else (hbm, vmem)
        copies.append(pltpu.make_async_copy(src, dst, sem_ref.at[slot, j]))
    return copies


def _s5_in_kernel(x_hbm, mod_ref, g_ref, w_ref, zt_ref, xbuf, sem, *, mod_rows):
    g = g_ref[...]
    rows = S5_CB // mod_rows
    i = pl.program_id(0)
    slot = i % 2
    fetch = functools.partial(_token_slab_copies, x_hbm, xbuf, sem, to_hbm=False)

    @pl.when(i == 0)
    def _():
        for c in fetch(0, 0):
            c.start()

    @pl.when(i + 1 < pl.num_programs(0))
    def _():
        for c in fetch(i + 1, 1 - slot):
            c.start()

    for c in fetch(i, slot):
        c.wait()

    def normed(j):
        x = xbuf[slot, j]
        parts = [_mod_norm(x[m * rows:(m + 1) * rows, :], g, mod_ref[m, 4:5, :], mod_ref[m, 3:4, :])
                 for m in range(mod_rows)]
        return parts[0] if mod_rows == 1 else jnp.concatenate(parts, axis=0)

    wt = w_ref[...].T
    for j in range(0, S5_T, 2):
        h = jnp.concatenate([normed(j), normed(j + 1)], axis=0).astype(BF)
        zt = lax.dot_general(wt, h, (((1,), (1,)), ((), ())), preferred_element_type=F32)
        zt_ref[j] = zt[:, :S5_CB]
        zt_ref[j + 1] = zt[:, S5_CB:]


def _s5_in(x, mod, norm_g, w_in, *, layer, seq_chunks):
    nc = x.shape[0] // S5_T
    xv = x.reshape(nc, S5_T, D_MODEL)
    mod_rows = 1 if mod.shape[0] == 1 else S5_CB // seq_chunks
    in_specs = [
        pl.BlockSpec(memory_space=pl.ANY),
        _mod_spec(mod.shape[0], 1, rows=mod_rows),
        _const_spec((None, 1, D_MODEL), (3 * layer + 1, 0, 0)),
        _const_spec((None, D_MODEL, S5_WIDTH), (layer, 0, 0)),
    ]
    return pl.pallas_call(
        functools.partial(_s5_in_kernel, mod_rows=mod_rows),
        grid=(nc // S5_CB,),
        in_specs=in_specs,
        out_specs=pl.BlockSpec((S5_T, S5_WIDTH, S5_CB), lambda i: (0, 0, i)),
        out_shape=jax.ShapeDtypeStruct((S5_T, S5_WIDTH, nc), F32),
        scratch_shapes=[pltpu.VMEM((2, S5_T, S5_CB, D_MODEL), F32), pltpu.SemaphoreType.DMA((2, S5_T))],
        compiler_params=_params(1),
        name="s5_in",
    )(xv, mod, norm_g, w_in)


def _s5_core_kernel(*refs, groups, seq_chunks, has_h0, want_final):
    refs = list(refs)
    zt_ref, mt_ref, win_ref, woutt_ref, at_re_ref, at_im_ref = refs[:6]
    refs = refs[6:]
    h0_ref = refs.pop(0) if has_h0 else None
    yt_ref = refs.pop(0)
    hf_ref = refs.pop(0) if want_final else None
    T, K = S5_T, seq_chunks
    half = S5_NS // 2
    nc = zt_ref.shape[2]
    n_seq = nc // K
    n_steps = K.bit_length() - 1
    fwd = lax.broadcasted_iota(jnp.int32, (nc, half), 1) < S5_STATE
    bwd = jnp.logical_not(fwd)
    kpos = lax.broadcasted_iota(jnp.int32, (nc, half), 0) % K
    first = (fwd & (kpos == 0)) | (bwd & (kpos == K - 1))
    valid = [(fwd & (kpos >= (1 << t))) | (bwd & (kpos < K - (1 << t))) for t in range(n_steps)]
    shift = lambda x, s: jnp.where(fwd, pltpu.roll(x, s, 0), pltpu.roll(x, nc - s, 0))
    if has_h0:
        chunk = lax.broadcasted_iota(jnp.int32, (nc, n_seq), 0)
        seq = lax.broadcasted_iota(jnp.int32, (nc, n_seq), 1)
        place_f = (chunk == seq * K).astype(F32)
        place_b = (chunk == seq * K + (K - 1)).astype(F32)
        fwd_full = jnp.concatenate([fwd, fwd], axis=1)
    if want_final:
        seq = lax.broadcasted_iota(jnp.int32, (LANES, nc), 0)
        chunk = lax.broadcasted_iota(jnp.int32, (LANES, nc), 1)
        pick_f = (chunk == seq * K + (K - 1)).astype(F32)
        pick_b = (chunk == seq * K).astype(F32)
        fwd_row = lax.broadcasted_iota(jnp.int32, (1, S5_NS), 1) % half < S5_STATE

    xts, vs, h0s = [], [], []
    for g in range(groups):
        r0 = g * S5_GROUP
        xt = jnp.concatenate([zt_ref[j, r0:r0 + S5_GROUP, :] for j in range(T)], axis=0).astype(BF)
        xts.append(xt)
        vs.append(lax.dot_general(xt, win_ref[g], (((0,), (0,)), ((), ())), preferred_element_type=F32))
        if has_h0:
            h0s.append(jnp.where(fwd_full, _dot_exact(place_f, h0_ref[g]), _dot_exact(place_b, h0_ref[g])))
    yield
    s_ins, fins = [], []
    for g in range(groups):
        vr, vi = vs[g][:, :half], vs[g][:, half:]
        cr, ci = at_re_ref[g, :, :half], at_im_ref[g, :, :half]
        ar, ai = cr, ci
        hr, hi = (h0s[g][:, :half], h0s[g][:, half:]) if has_h0 else (0.0, 0.0)
        sr = jnp.where(first, hr, shift(vr, 1))
        si = jnp.where(first, hi, shift(vi, 1))
        for t in range(n_steps):
            rr = jnp.where(valid[t], shift(sr, 1 << t), 0.0)
            ri = jnp.where(valid[t], shift(si, 1 << t), 0.0)
            sr, si = sr + (cr * rr - ci * ri), si + (cr * ri + ci * rr)
            cr, ci = cr * cr - ci * ci, 2.0 * (cr * ci)
        s_ins.append(jnp.concatenate([sr, si], axis=1).astype(BF))
        if want_final:
            fins.append(jnp.concatenate([ar * sr - ai * si + vr, ar * si + ai * sr + vi], axis=1))
    yield
    for g in range(groups):
        r0 = g * S5_GROUP
        y = _dot(mt_ref[g], xts[g]) + lax.dot_general(woutt_ref[g], s_ins[g], (((1,), (1,)), ((), ())),
                                                      preferred_element_type=F32)
        for i in range(T):
            yt_ref[i, r0:r0 + S5_GROUP, :] = y[i * S5_GROUP:(i + 1) * S5_GROUP, :]
        if want_final:
            hf_ref[g] = jnp.where(fwd_row, _dot_exact(pick_f, fins[g]), _dot_exact(pick_b, fins[g]))


def _s5_core_stage(zt, ops, *, layer, seq_chunks, steps, h0=None, want_final=False):
    nc = zt.shape[2]
    gb = S5_GROUPS // steps
    rows = gb * S5_GROUP
    gspec = lambda r, c: pl.BlockSpec((gb, r, c), lambda i: (layer * steps + i, 0, 0))
    slab = pl.BlockSpec((S5_T, rows, nc), lambda i: (0, i, 0))
    in_specs = [slab, gspec(S5_CL, S5_CL), gspec(S5_CL, S5_NS), gspec(S5_CL, S5_NS),
                gspec(1, S5_NS), gspec(1, S5_NS)]
    args = [zt, ops["mt"], ops["win"], ops["woutt"], ops["at_re"], ops["at_im"]]
    if h0 is not None:
        in_specs.append(pl.BlockSpec((gb, h0.shape[1], S5_NS), lambda i: (i, 0, 0)))
        args.append(h0)
    out_specs = [slab]
    out_shape = [jax.ShapeDtypeStruct(zt.shape, F32)]
    if want_final:
        assert nc // seq_chunks <= LANES
        out_specs.append(pl.BlockSpec((gb, LANES, S5_NS), lambda i: (i, 0, 0)))
        out_shape.append(jax.ShapeDtypeStruct((S5_GROUPS, LANES, S5_NS), F32))
    kern = functools.partial(_s5_core_kernel, groups=gb, seq_chunks=seq_chunks, has_h0=h0 is not None,
                             want_final=want_final)
    return dict(kernel=kern, grid=(steps,), in_specs=in_specs, args=args, out_specs=out_specs, out_shape=out_shape)


def _s5_out_kernel(yt_ref, zt_ref, d_ref, wglut_ref, wa_ref, o_hbm, obuf, sem):
    i = pl.program_id(0)
    last = pl.num_programs(0) - 1
    slot = i % 2
    store = functools.partial(_token_slab_copies, o_hbm, obuf, sem, to_hbm=True)

    @pl.when(i >= 2)
    def _():
        for c in store(i - 2, slot):
            c.wait()

    d = d_ref[...]
    for j in range(0, S5_T, 2):
        pre = jnp.concatenate([yt_ref[j] + d * zt_ref[j], yt_ref[j + 1] + d * zt_ref[j + 1]], axis=1)
        ya = jax.nn.gelu(pre)
        ya = ya * jax.nn.sigmoid(_dot(wglut_ref[...], ya.astype(BF)))
        pa = lax.dot_general(ya.astype(BF), wa_ref[...], (((0,), (0,)), ((), ())), preferred_element_type=F32)
        obuf[slot, j] = pa[:S5_CB, :]
        obuf[slot, j + 1] = pa[S5_CB:, :]
    for c in store(i, slot):
        c.start()

    @pl.when(i == last)
    def _():
        for c in store(i, slot):
            c.wait()

    @pl.when((i == last) & (i >= 1))
    def _():
        for c in store(i - 1, 1 - slot):
            c.wait()


def _s5_out(yt, zt, s5_d_col, w_glu_t, w_br_a, *, layer):
    nc = yt.shape[2]
    slab = pl.BlockSpec((S5_T, S5_WIDTH, S5_CB), lambda i: (0, 0, i))
    out = pl.pallas_call(
        _s5_out_kernel,
        grid=(nc // S5_CB,),
        in_specs=[slab, slab,
                  _const_spec((None, S5_WIDTH, 1), (layer, 0, 0)),
                  _const_spec((None, S5_WIDTH, S5_WIDTH), (layer, 0, 0)),
                  _const_spec((None, S5_WIDTH, D_MODEL), (layer, 0, 0))],
        out_specs=pl.BlockSpec(memory_space=pl.ANY),
        out_shape=jax.ShapeDtypeStruct((nc, S5_T, D_MODEL), F32),
        scratch_shapes=[pltpu.VMEM((2, S5_T, S5_CB, D_MODEL), F32), pltpu.SemaphoreType.DMA((2, S5_T))],
        compiler_params=_params(1),
        name="s5_out",
    )(yt, zt, s5_d_col, w_glu_t, w_br_a)
    return out.reshape(nc * S5_T, D_MODEL)


def _mix_out_kernel(x_ref, mod_ref, g_ref, pa_ref, yb_ref, yc_ref, wb_ref, wc_ref, wgate_ref, bgate_ref,
                    wout_ref, o_ref):
    x = x_ref[...]
    h = _mod_norm(x, g_ref[...], mod_ref[0, 4:5, :], mod_ref[0, 3:4, :]).astype(BF)

    def gate(k):
        lo = k * D_MODEL
        return jax.nn.sigmoid(_dot(h, wgate_ref[:, lo:lo + D_MODEL]) + bgate_ref[:, lo:lo + D_MODEL])

    merged = gate(0) * pa_ref[...]
    merged = merged + gate(1) * _dot(yb_ref[...], wb_ref[...])
    merged = merged + gate(2) * _dot(yc_ref[...], wc_ref[...])
    y = _dot(merged.astype(BF), wout_ref[...])
    o_ref[...] = x + mod_ref[0, 5:6, :] * y


def _mix_out(x, mod, norm_g, pa, yb, yc, w_b, w_c, w_gate, b_gate, w_out, *, layer, seq_tokens):
    n = x.shape[0]
    tm = FFN_TM
    tok = lambda c: pl.BlockSpec((tm, c), lambda i: (i, 0))
    return pl.pallas_call(
        _mix_out_kernel,
        grid=(n // tm,),
        in_specs=[
            tok(D_MODEL),
            _mod_spec(mod.shape[0], seq_tokens // tm),
            _const_spec((None, 1, D_MODEL), (3 * layer + 1, 0, 0)),
            tok(D_MODEL), tok(CONV_WIDTH), tok(SG_WIDTH),
            _const_spec((None, CONV_WIDTH, D_MODEL), (layer, 0, 0)),
            _const_spec((None, SG_WIDTH, D_MODEL), (layer, 0, 0)),
            _const_spec((None, D_MODEL, 3 * D_MODEL), (layer, 0, 0)),
            _const_spec((None, 1, 3 * D_MODEL), (layer, 0, 0)),
            _const_spec((None, D_MODEL, D_MODEL), (layer, 0, 0)),
        ],
        out_specs=tok(D_MODEL),
        out_shape=jax.ShapeDtypeStruct((n, D_MODEL), F32),
        compiler_params=_params(1),
        name="mix_out",
    )(x, mod, norm_g, pa, yb, yc, w_b, w_c, w_gate, b_gate, w_out)


def _grid_pos_embed(n_tokens, dim):
    rows = n_tokens // GRID_W
    rr, cc = jnp.meshgrid(jnp.arange(rows, dtype=F32), jnp.arange(GRID_W, dtype=F32), indexing='ij')
    quarter = dim // 4
    omega = 1.0 / (10000.0 ** (jnp.arange(quarter, dtype=F32) / quarter))

    def emb(p):
        ang = p.reshape(-1)[:, None] * omega[None, :]
        return jnp.concatenate([jnp.sin(ang), jnp.cos(ang)], axis=-1)

    return jnp.concatenate([emb(rr), emb(cc)], axis=-1)


def kernel(x_prompt, x_sample, state_ssm, c, c_ctx, w_mod, b_mod, norm_g, ffn_w1, ffn_w2, w_in, w_gate, b_gate,
           s5_a_re, s5_a_im, s5_log_dt, s5_b_re, s5_b_im, s5_c_re, s5_c_im, s5_d, s5_w_glu, w_br_a, conv_w,
           conv_b, conv_ln_g, conv_ln_b, w_br_b, sg_ln_g, sg_ln_b, sg_w, sg_b, w_br_c, w_out, final_g):
    batch, seq, _ = x_prompt.shape
    dec_batch, dec_seq, _ = x_sample.shape
    assert (batch * seq) % TOK_BLOCK == 0 and TOK_BLOCK % seq == 0 and dec_seq % TOK_BLOCK == 0
    assert seq % SG_CHUNK == 0 and seq % S5_T == 0 and dec_seq % S5_T == 0
    ctx_chunks, smp_chunks = seq // S5_T, dec_seq // S5_T
    assert S5_CB % ctx_chunks == 0 and S5_CB % smp_chunks == 0
    assert ctx_chunks & (ctx_chunks - 1) == 0 and smp_chunks & (smp_chunks - 1) == 0
    assert 1 + dec_batch <= MOD_ROWS

    cond = jnp.zeros((MOD_ROWS, D_MODEL), F32).at[0].set(c_ctx).at[1:1 + dec_batch].set(c)
    mod_stage = _modulation_stage(cond, w_mod, b_mod)
    [mod_all], prep_out, [w1_first], [w2_first] = _call(
        [mod_stage, _s5_prep_stage(s5_a_re, s5_a_im, s5_log_dt, s5_b_re, s5_b_im, s5_c_re, s5_c_im),
         _cast_stage(ffn_w1, (0, 0), mod_stage["grid"][0]), _cast_stage(ffn_w2, (0, 0), mod_stage["grid"][0])],
        "mod_prep")
    mod_all = mod_all.reshape(DEPTH, MOD_ROWS, N_MOD, D_MODEL)
    s5_ops = dict(zip(S5_OPS, prep_out))
    pos = _grid_pos_embed(dec_seq, D_MODEL)

    ffn_w = {(0, 0): (w1_first, w2_first)}
    mixer_w = dict(w_in=w_in, w_gate=w_gate, w_out=w_out, w_a=w_br_a, w_b=w_br_b, w_c=w_br_c)
    w_glu_t = s5_w_glu.transpose(0, 2, 1).astype(BF)
    norm_rows = norm_g.reshape(DEPTH * 3, 1, D_MODEL)
    rows = lambda a: a.reshape(DEPTH, 1, -1)
    sgw_cat = sg_w.transpose(0, 2, 1, 3).reshape(DEPTH, SG_CHUNK, SG_HEADS * SG_CHUNK).astype(BF)
    sgb_full = jnp.repeat(sg_b.transpose(0, 2, 1), SG_HEAD_DIM, axis=2)
    s5_d_col = s5_d.reshape(DEPTH, S5_WIDTH, 1)

    conv_w8 = jnp.broadcast_to(conv_w[:, :, None, :], (DEPTH, CONV_K, SUBLANES, CONV_WIDTH))

    ctx = dict(n_seq=batch, seq_len=seq, mod_rows=slice(0, 1), mod_tokens=batch * seq)
    smp = dict(n_seq=dec_batch, seq_len=dec_seq, mod_rows=slice(1, 1 + dec_batch), mod_tokens=dec_seq)
    x_ctx = x_prompt.reshape(batch * seq, D_MODEL)
    x_smp = x_sample.reshape(dec_batch * dec_seq, D_MODEL)
    ffn_steps = x_ctx.shape[0] // FFN_TM
    assert x_smp.shape[0] // FFN_TM == ffn_steps and S5_GROUPS % ffn_steps == 0

    assert DEPTH * S5_GROUPS // S5_PREP_GB == ffn_steps

    def ffn_stage(gr, x, l, which, **kw):
        return _ffn_stage(x, mod_all[l, gr["mod_rows"]], norm_rows, *ffn_w[l, which], layer=l, which=which,
                          seq_tokens=gr["mod_tokens"], **kw)

    def cast_stages(l, which):
        if l >= DEPTH:
            return []
        return [_cast_stage(ffn_w1, (l, which), ffn_steps), _cast_stage(ffn_w2, (l, which), ffn_steps)]

    def keep_casts(l, which, outs):
        if outs:
            ffn_w[l, which] = (outs[0][0], outs[1][0])

    def branches_in(gr, x, l):
        mod = mod_all[l, gr["mod_rows"]]
        yb, yc = _mix_in(x, mod, norm_rows, mixer_w["w_in"], conv_w8, rows(conv_b), rows(conv_ln_g), rows(conv_ln_b),
                         rows(sg_ln_g), rows(sg_ln_b), sgw_cat, sgb_full, layer=l, seq_len=gr["seq_len"])
        zt = _s5_in(x, mod, norm_rows, mixer_w["w_in"], layer=l, seq_chunks=gr["seq_len"] // S5_T)
        return yb, yc, zt

    def s5_stage(gr, zt, l, **kw):
        return _s5_core_stage(zt, s5_ops, layer=l, seq_chunks=gr["seq_len"] // S5_T, steps=ffn_steps, **kw)

    def branches_out(gr, x, l, yb, yc, zt, yt):
        pa = _s5_out(yt, zt, s5_d_col, w_glu_t, mixer_w["w_a"], layer=l)
        return _mix_out(x, mod_all[l, gr["mod_rows"]], norm_rows, pa, yb, yc, mixer_w["w_b"], mixer_w["w_c"],
                        mixer_w["w_gate"], rows(b_gate), mixer_w["w_out"], layer=l, seq_tokens=gr["mod_tokens"])

    ctx_states = []
    for l in range(DEPTH):
        riders = []
        if l == 0:
            riders = cast_stages(0, 1) + [_cast_stage(w.reshape(-1, w.shape[-1]), (), ffn_steps)
                                          for w in mixer_w.values()]
        [x_ctx], *outs = _call([ffn_stage(ctx, x_ctx, l, 0)] + riders, "ffn")
        if l == 0:
            keep_casts(0, 1, outs[:2])
            mixer_w = {k: o[0].reshape(mixer_w[k].shape) for k, o in zip(mixer_w, outs[2:])}
        yb_c, yc_c, zt_c = branches_in(ctx, x_ctx, l)
        [x_smp], [yt_c, hf], *outs = _call([ffn_stage(smp, x_smp, l, 0, pos=pos if l == 0 else None),
                                             s5_stage(ctx, zt_c, l, want_final=True)] + cast_stages(l + 1, 0),
                                            "ffn_s5")
        keep_casts(l + 1, 0, outs)
        hf = hf[:, :batch, :].reshape(S5_GROUPS, batch, 2, 2, S5_STATE)
        ctx_states.append(hf.transpose(1, 3, 0, 4, 2))
        x_ctx = branches_out(ctx, x_ctx, l, yb_c, yc_c, zt_c, yt_c)
        yb_s, yc_s, zt_s = branches_in(smp, x_smp, l)
        h0 = state_ssm[:, l].transpose(2, 0, 4, 1, 3).reshape(S5_GROUPS, dec_batch, S5_NS)
        final = dict(final_g=final_g) if l == DEPTH - 1 else {}
        [x_ctx], [yt_s], *outs = _call([ffn_stage(ctx, x_ctx, l, 1, **final), s5_stage(smp, zt_s, l, h0=h0)]
                                       + cast_stages(l + 1, 1), "ffn_s5")
        keep_casts(l + 1, 1, outs)
        x_smp = branches_out(smp, x_smp, l, yb_s, yc_s, zt_s, yt_s)
        [x_smp], = _call([ffn_stage(smp, x_smp, l, 1, **final)], "ffn")
    y_prompt = x_ctx.reshape(batch, seq, D_MODEL)
    y_sample = x_smp.reshape(dec_batch, dec_seq, D_MODEL)
    new_state_ssm = jnp.stack(ctx_states, axis=1)
    return (y_prompt, y_sample, new_state_ssm)
```

```python
import functools
import inspect

import jax
import jax.numpy as jnp
from jax import lax
from jax.experimental import pallas as pl
from jax.experimental.pallas import tpu as pltpu

D_MODEL = 1024
DEPTH = 2
GRID_W = 64
D_FF = 2816
S5_WIDTH = 512
S5_GROUP = 16
S5_GROUPS = 32
S5_STATE = 64
CONV_WIDTH = 256
CONV_K = 31
SG_WIDTH = 256
SG_CHUNK = 128
SG_HEADS = 4
SG_HEAD_DIM = SG_WIDTH // SG_HEADS
BRANCH_COLS = 2 * CONV_WIDTH
IN_COLS = S5_WIDTH + 2 * BRANCH_COLS
N_MOD = 9
EPS = 1e-6

LANES = 128
SUBLANES = 8
S5_T = 16
S5_CL = S5_T * S5_GROUP
S5_NS = 4 * S5_STATE
S5_PREP_GB = 4
S5_POW_ROWS = SUBLANES * (S5_T // SUBLANES + 1)
S5_CB = 128
MOD_ROWS = 16
MOD_TN = 1152
TOK_BLOCK = 1024
FFN_TM = 512
FFN_SUBTILES = 4
CONV_RC = 32
CONV_PAD = 16
CONV_SHIFT_ROWS = SUBLANES * ((CONV_PAD - CONV_K // 2 + CONV_K - 1) // SUBLANES)
VMEM_LIMIT = 56 * 1024 * 1024

BF = jnp.bfloat16
F32 = jnp.float32
HIGHEST = lax.Precision.HIGHEST


def _dot(a, b):
    return jnp.dot(a, b, preferred_element_type=F32)


def _dot_exact(a, b):
    return jnp.dot(a, b, preferred_element_type=F32, precision=HIGHEST)


def _const_spec(block, index):
    return pl.BlockSpec(block, lambda *_: index, pipeline_mode=pl.Buffered(1))


def _params(n_axes=1):
    return pltpu.CompilerParams(dimension_semantics=("arbitrary",) * n_axes,
                                vmem_limit_bytes=VMEM_LIMIT)


def _call(stages, name):
    grid = stages[0]["grid"]
    assert all(s["grid"] == grid for s in stages)
    n_in = [len(s["in_specs"]) for s in stages]
    n_out = [len(s["out_specs"]) for s in stages]
    total_in, total_out = sum(n_in), sum(n_out)

    def body(*refs):
        bound, i, o = [], 0, 0
        for s, ni, no in zip(stages, n_in, n_out):
            bound.append((s["kernel"], refs[i:i + ni] + refs[total_in + o:total_in + o + no]))
            i, o = i + ni, o + no
        live = [run for run in (kern(*r) for kern, r in reversed(bound)) if inspect.isgenerator(run)]
        while live:
            for gen in list(live):
                if next(gen, "done") == "done":
                    live.remove(gen)

    outs = pl.pallas_call(
        body,
        grid=grid,
        in_specs=[spec for s in stages for spec in s["in_specs"]],
        out_specs=[spec for s in stages for spec in s["out_specs"]],
        out_shape=[shape for s in stages for shape in s["out_shape"]],
        compiler_params=_params(len(grid)),
        name=name,
    )(*[a for s in stages for a in s["args"]])
    split, o = [], 0
    for no in n_out:
        split.append(list(outs[o:o + no]))
        o += no
    return split


def _mod_norm(x, g, sc, sh):
    var = jnp.mean(x * x, axis=-1, keepdims=True)
    return (x * lax.rsqrt(var + EPS) * g) * (1.0 + sc) + sh


def _layernorm(x, g, b):
    mu = jnp.mean(x, axis=-1, keepdims=True)
    xc = x - mu
    var = jnp.mean(xc * xc, axis=-1, keepdims=True)
    return xc * lax.rsqrt(var + EPS) * g + b


def _mod_spec(n_mod, blocks_per_mod, rows=1):
    if n_mod == 1:
        return pl.BlockSpec((1, N_MOD, D_MODEL), lambda i: (0, 0, 0))
    return pl.BlockSpec((rows, N_MOD, D_MODEL), lambda i: (i // blocks_per_mod, 0, 0))


def _mod_kernel(c_ref, w_ref, b_ref, o_ref):
    c = c_ref[...]
    a = (c * jax.nn.sigmoid(c)).astype(BF)
    o_ref[...] = _dot(a, w_ref[...].astype(BF)) + b_ref[...]


def _modulation_stage(cond, w_mod, b_mod):
    n = N_MOD * D_MODEL
    tiles = n // MOD_TN
    tile = lambda i: (i // tiles, 0, i % tiles)
    return dict(
        kernel=_mod_kernel, grid=(DEPTH * tiles,), args=[cond, w_mod, b_mod.reshape(DEPTH, 1, n)],
        in_specs=[pl.BlockSpec((MOD_ROWS, D_MODEL), lambda i: (0, 0)),
                  pl.BlockSpec((None, D_MODEL, MOD_TN), tile),
                  pl.BlockSpec((None, 1, MOD_TN), tile)],
        out_specs=[pl.BlockSpec((None, MOD_ROWS, MOD_TN), tile)],
        out_shape=[jax.ShapeDtypeStruct((DEPTH, MOD_ROWS, n), F32)])


def _ffn_kernel(*refs, mod_base, add_pos, final):
    refs = list(refs)
    x_ref = refs.pop(0)
    pos_ref = refs.pop(0) if add_pos else None
    mod_ref, g_ref, w1g_ref, w1u_ref, w2_ref = refs[:5]
    refs = refs[5:]
    fg_ref = refs.pop(0) if final else None
    o_ref = refs.pop(0)

    sh = mod_ref[0, mod_base:mod_base + 1, :]
    sc = mod_ref[0, mod_base + 1:mod_base + 2, :]
    gt = mod_ref[0, mod_base + 2:mod_base + 3, :]
    rows = x_ref.shape[0] // FFN_SUBTILES
    sub = [slice(r * rows, (r + 1) * rows) for r in range(FFN_SUBTILES)]
    xs, hs, gs, accs = [], [], [], []
    for r in sub:
        x = x_ref[r, :] + pos_ref[r, :] if add_pos else x_ref[r, :]
        xs.append(x)
        hs.append(_mod_norm(x, g_ref[...], sc, sh).astype(BF))
        gs.append(_dot(hs[-1], w1g_ref[...]))
    yield
    for h, g in zip(hs, gs):
        u = _dot(h, w1u_ref[...])
        accs.append((g * jax.nn.sigmoid(g) * u).astype(BF))
    yield
    for r, x, a in zip(sub, xs, accs):
        y = _dot(a, w2_ref[...])
        xn = x + (0.5 * gt) * y
        if final:
            var = jnp.mean(xn * xn, axis=-1, keepdims=True)
            xn = xn * lax.rsqrt(var + EPS) * fg_ref[...]
        o_ref[r, :] = xn
    yield


def _cast_kernel(x_ref, o_ref):
    o_ref[...] = x_ref[...].astype(o_ref.dtype)


def _cast_stage(w, index, steps):
    rows, cols = w.shape[-2:]
    lead = (None,) * len(index)
    return dict(kernel=_cast_kernel, grid=(steps,), args=[w],
                in_specs=[pl.BlockSpec(lead + (rows // steps, cols), lambda i: tuple(index) + (i, 0))],
                out_specs=[pl.BlockSpec((rows // steps, cols), lambda i: (i, 0))],
                out_shape=[jax.ShapeDtypeStruct((rows, cols), BF)])


def _ffn_stage(x, mod, norm_g, w1, w2, *, layer, which, seq_tokens, pos=None, final_g=None):
    n = x.shape[0]
    tm = FFN_TM
    in_specs = [pl.BlockSpec((tm, D_MODEL), lambda i: (i, 0))]
    args = [x]
    if pos is not None:
        pos_blocks = pos.shape[0] // tm
        in_specs.append(pl.BlockSpec((tm, D_MODEL), lambda i: (i % pos_blocks, 0)))
        args.append(pos)
    in_specs += [
        _mod_spec(mod.shape[0], seq_tokens // tm),
        _const_spec((None, 1, D_MODEL), (3 * layer + 2 * which, 0, 0)),
        _const_spec((D_MODEL, D_FF), (0, 0)),
        _const_spec((D_MODEL, D_FF), (0, 1)),
        _const_spec((D_FF, D_MODEL), (0, 0)),
    ]
    args += [mod, norm_g, w1, w1, w2]
    if final_g is not None:
        in_specs.append(_const_spec((1, D_MODEL), (0, 0)))
        args.append(final_g.reshape(1, D_MODEL))
    kern = functools.partial(_ffn_kernel, mod_base=6 * which, add_pos=pos is not None,
                             final=final_g is not None)
    return dict(kernel=kern, grid=(n // tm,), in_specs=in_specs, args=args,
                out_specs=[pl.BlockSpec((tm, D_MODEL), lambda i: (i, 0))],
                out_shape=[jax.ShapeDtypeStruct((n, D_MODEL), F32)])


def _mix_in_kernel(x_ref, mod_ref, g_ref, winb_ref, winc_ref, cw_ref, cb_ref, clg_ref, clb_ref,
                   sglg_ref, sglb_ref, sgw_ref, sgb_ref, yb_ref, yc_ref, pad_ref, conv_ref, *, seq_len):
    n_seq = TOK_BLOCK // seq_len
    x = x_ref[...]
    h = _mod_norm(x, g_ref[...], mod_ref[0, 4:5, :], mod_ref[0, 3:4, :]).astype(BF)

    zb = _dot(h, winb_ref[...])
    gl = zb[:, :CONV_WIDTH] * jax.nn.sigmoid(zb[:, CONV_WIDTH:])
    zeros = jnp.zeros((CONV_PAD, CONV_WIDTH), F32)
    shifted_rows = seq_len + CONV_SHIFT_ROWS
    padded_rows = seq_len + 2 * CONV_PAD
    for s in range(n_seq):
        padded = jnp.concatenate([zeros, gl[s * seq_len:(s + 1) * seq_len, :], zeros], axis=0)
        pad_ref[0, s] = padded[:shifted_rows, :]
        for r in range(1, SUBLANES):
            pad_ref[r, s] = pltpu.roll(padded, padded_rows - r, 0)[:shifted_rows, :]
    first = CONV_PAD - CONV_K // 2
    tiles = CONV_RC // SUBLANES
    n_q = CONV_SHIFT_ROWS // SUBLANES + 1
    for s in range(n_seq):
        def conv_rows(r, carry, s=s):
            r0 = pl.multiple_of(r * CONV_RC, CONV_RC)
            accs = [None] * n_q
            for shift in range(SUBLANES):
                span = pad_ref[shift, s, pl.ds(r0, CONV_RC + CONV_SHIFT_ROWS), :]
                span = span.reshape(tiles + n_q - 1, SUBLANES, CONV_WIDTH)
                for q in range(n_q):
                    k = SUBLANES * q + shift - first
                    if 0 <= k < CONV_K:
                        term = cw_ref[k] * span[q:q + tiles]
                        accs[q] = term if accs[q] is None else accs[q] + term
            acc = (accs[0] + accs[1]) + (accs[2] + accs[3])
            conv_ref[pl.ds(pl.multiple_of(s * seq_len + r0, CONV_RC), CONV_RC), :] = acc.reshape(CONV_RC, CONV_WIDTH)
            return carry
        lax.fori_loop(0, seq_len // CONV_RC, conv_rows, 0)
    y = _layernorm(conv_ref[...] + cb_ref[...], clg_ref[...], clb_ref[...])
    yb_ref[...] = (y * jax.nn.sigmoid(y)).astype(BF)

    zc = jax.nn.gelu(_dot(h, winc_ref[...]))
    u = zc[:, :SG_WIDTH]
    v = _layernorm(zc[:, SG_WIDTH:], sglg_ref[...], sglb_ref[...]).astype(BF)
    head = lax.broadcasted_iota(jnp.int32, (SG_CHUNK, SG_WIDTH), 1) // SG_HEAD_DIM
    zero = jnp.zeros((SG_CHUNK, SG_WIDTH), BF)
    for n in range(TOK_BLOCK // SG_CHUNK):
        vn = v[n * SG_CHUNK:(n + 1) * SG_CHUNK, :]
        vcat = jnp.concatenate([jnp.where(head == hh, vn, zero) for hh in range(SG_HEADS)], axis=0)
        sgate = _dot(sgw_ref[...], vcat) + sgb_ref[...]
        yc_ref[n * SG_CHUNK:(n + 1) * SG_CHUNK, :] = (u[n * SG_CHUNK:(n + 1) * SG_CHUNK, :] * sgate).astype(BF)


def _mix_in(x, mod, norm_g, w_in, conv_w, conv_b, conv_ln_g, conv_ln_b, sg_ln_g, sg_ln_b, sgw_cat, sgb_full,
            *, layer, seq_len):
    n = x.shape[0]
    n_seq = TOK_BLOCK // seq_len
    vec = lambda width: _const_spec((None, 1, width), (layer, 0, 0))
    kern = functools.partial(_mix_in_kernel, seq_len=seq_len)
    return pl.pallas_call(
        kern,
        grid=(n // TOK_BLOCK,),
        in_specs=[
            pl.BlockSpec((TOK_BLOCK, D_MODEL), lambda i: (i, 0)),
            _mod_spec(mod.shape[0], max(seq_len // TOK_BLOCK, 1)),
            _const_spec((None, 1, D_MODEL), (3 * layer + 1, 0, 0)),
            _const_spec((None, D_MODEL, BRANCH_COLS), (layer, 0, S5_WIDTH // BRANCH_COLS)),
            _const_spec((None, D_MODEL, BRANCH_COLS), (layer, 0, S5_WIDTH // BRANCH_COLS + 1)),
            _const_spec((None, CONV_K, SUBLANES, CONV_WIDTH), (layer, 0, 0, 0)),
            vec(CONV_WIDTH), vec(CONV_WIDTH), vec(CONV_WIDTH), vec(SG_WIDTH), vec(SG_WIDTH),
            _const_spec((None, SG_CHUNK, SG_HEADS * SG_CHUNK), (layer, 0, 0)),
            _const_spec((None, SG_CHUNK, SG_WIDTH), (layer, 0, 0)),
        ],
        out_specs=[
            pl.BlockSpec((TOK_BLOCK, CONV_WIDTH), lambda i: (i, 0)),
            pl.BlockSpec((TOK_BLOCK, SG_WIDTH), lambda i: (i, 0)),
        ],
        out_shape=[
            jax.ShapeDtypeStruct((n, CONV_WIDTH), BF),
            jax.ShapeDtypeStruct((n, SG_WIDTH), BF),
        ],
        scratch_shapes=[pltpu.VMEM((SUBLANES, n_seq, seq_len + CONV_SHIFT_ROWS, CONV_WIDTH), F32),
                        pltpu.VMEM((TOK_BLOCK, CONV_WIDTH), F32)],
        compiler_params=_params(1),
        name="mix_in",
    )(x, mod, norm_g, w_in, w_in, conv_w, conv_b, conv_ln_g, conv_ln_b, sg_ln_g, sg_ln_b, sgw_cat, sgb_full)


def _s5_prep_kernel(arow_re_ref, arow_im_ref, dtrow_ref, b4_re_ref, b4_im_ref, c4_re_ref, c4_im_ref,
                    mt_ref, win_ref, woutt_ref, at_re_ref, at_im_ref):
    T, H = S5_T, S5_GROUP
    lanes = 2 * S5_CL
    lane = lax.broadcasted_iota(jnp.int32, (1, S5_NS), 1)
    lane_im = lane >= 2 * S5_STATE
    lane_bwd = (lane // S5_STATE) % 2 == 1
    e_col = lax.broadcasted_iota(jnp.int32, (S5_POW_ROWS, 1), 0).astype(F32)
    zeros = jnp.zeros((H, S5_NS), F32)
    for g in range(S5_PREP_GB):
        a_re, a_im = arow_re_ref[g], arow_im_ref[g]
        dt = jnp.exp(dtrow_ref[g])
        mag = jnp.exp(a_re * dt * e_col)
        ang = a_im * dt * e_col
        pw_re, pw_im = mag * jnp.cos(ang), mag * jnp.sin(ang)
        den = a_re * a_re + a_im * a_im
        nr, ni = pw_re[1:2, :] - 1.0, pw_im[1:2, :]
        q_re = (nr * a_re + ni * a_im) / den
        q_im = (ni * a_re - nr * a_im) / den
        b_re, b_im = b4_re_ref[g], b4_im_ref[g]
        bb_re = q_re * b_re - q_im * b_im
        bb_im = q_re * b_im + q_im * b_re
        c_re, c_im = c4_re_ref[g], c4_im_ref[g]

        def power(e_f, e_b, pw_re=pw_re, pw_im=pw_im):
            return (jnp.where(lane_bwd, pw_re[e_b:e_b + 1, :], pw_re[e_f:e_f + 1, :]),
                    jnp.where(lane_bwd, pw_im[e_b:e_b + 1, :], pw_im[e_f:e_f + 1, :]))

        def times_b(e_f, e_b, bb_re=bb_re, bb_im=bb_im, power=power):
            p_re, p_im = power(e_f, e_b)
            return jnp.where(lane_im, p_re * bb_im + p_im * bb_re, p_re * bb_re - p_im * bb_im)

        def times_c(e_f, e_b, c_re=c_re, c_im=c_im, power=power):
            p_re, p_im = power(e_f, e_b)
            return jnp.where(lane_im, -(c_re * p_im + c_im * p_re), c_re * p_re - c_im * p_im)

        win_ref[g] = jnp.concatenate([times_b(T - 1 - j, j) for j in range(T)], axis=0).astype(BF)
        woutt_ref[g] = jnp.concatenate([times_c(i + 1, T - i) for i in range(T)], axis=0).astype(BF)
        blocks = []
        for b in range(2 * T - 1):
            m = T - 1 - b
            w = times_b(abs(m), abs(m))
            if m > 0:
                w = jnp.where(lane_bwd, 0.0, w)
            elif m < 0:
                w = jnp.where(lane_bwd, w, 0.0)
            blocks.append(w)
        ystack = jnp.concatenate(blocks + [zeros], axis=0)
        c_signed = jnp.where(lane_im, -c_im, c_re)
        krev = lax.dot_general(c_signed, ystack, (((1,), (1,)), ((), ())), preferred_element_type=F32,
                               precision=HIGHEST)
        mt_ref[g] = jnp.concatenate(
            [pltpu.roll(krev, (lanes - H * (T - 1 - i)) % lanes, 1)[:, :S5_CL] for i in range(T)],
            axis=0).astype(BF)
        at_re_ref[g] = pw_re[T:T + 1, :]
        at_im_ref[g] = pw_im[T:T + 1, :]


S5_OPS = ("mt", "win", "woutt", "at_re", "at_im")


def _s5_prep_stage(a_re, a_im, log_dt, b_re, b_im, c_re, c_im):
    G, P, H = S5_GROUPS, S5_STATE, S5_GROUP
    n = DEPTH * G
    rep = lambda a: jnp.broadcast_to(a.transpose(0, 2, 1, 3)[:, :, None, :, :],
                                     (DEPTH, G, 2, 2, P)).reshape(n, 1, S5_NS)
    dt4 = rep(jnp.broadcast_to(log_dt[..., None], (DEPTH, 2, G, P)))
    c4 = lambda c: jnp.broadcast_to(c.transpose(0, 2, 3, 1, 4)[:, :, :, None, :, :],
                                    (DEPTH, G, H, 2, 2, P)).reshape(n, H, S5_NS)
    b4 = lambda b: c4(b.transpose(0, 1, 2, 4, 3))
    blk = lambda r, c: pl.BlockSpec((S5_PREP_GB, r, c), lambda i: (i, 0, 0))
    op = jax.ShapeDtypeStruct((n, S5_CL, S5_CL), BF)
    row = jax.ShapeDtypeStruct((n, 1, S5_NS), F32)
    return dict(
        kernel=_s5_prep_kernel, grid=(n // S5_PREP_GB,),
        in_specs=[blk(1, S5_NS), blk(1, S5_NS), blk(1, S5_NS), blk(H, S5_NS), blk(H, S5_NS), blk(H, S5_NS),
                  blk(H, S5_NS)],
        args=[rep(a_re), rep(a_im), dt4, b4(b_re), b4(b_im), c4(c_re), c4(c_im)],
        out_specs=[blk(S5_CL, S5_CL), blk(S5_CL, S5_NS), blk(S5_CL, S5_NS), blk(1, S5_NS), blk(1, S5_NS)],
        out_shape=[op, op, op, row, row])


def _token_slab_copies(hbm_ref, buf_ref, sem_ref, block, slot, to_hbm):
    copies = []
    for j in range(S5_T):
        hbm = hbm_ref.at[pl.ds(block * S5_CB, S5_CB), j, :]
        vmem = buf_ref.at[slot, j]
        src, dst = (vmem, hbm) if to_hbm else (hbm, vmem)
        copies.append(pltpu.make_async_copy(src, dst, sem_ref.at[slot, j]))
    return copies


def _s5_in_kernel(x_hbm, mod_ref, g_ref, w_ref, zt_ref, xbuf, sem, *, mod_rows):
    g = g_ref[...]
    rows = S5_CB // mod_rows
    i = pl.program_id(0)
    slot = i % 2
    fetch = functools.partial(_token_slab_copies, x_hbm, xbuf, sem, to_hbm=False)

    @pl.when(i == 0)
    def _():
        for c in fetch(0, 0):
            c.start()

    @pl.when(i + 1 < pl.num_programs(0))
    def _():
        for c in fetch(i + 1, 1 - slot):
            c.start()

    for c in fetch(i, slot):
        c.wait()

    def normed(j):
        x = xbuf[slot, j]
        parts = [_mod_norm(x[m * rows:(m + 1) * rows, :], g, mod_ref[m, 4:5, :], mod_ref[m, 3:4, :])
                 for m in range(mod_rows)]
        return parts[0] if mod_rows == 1 else jnp.concatenate(parts, axis=0)

    wt = w_ref[...].T
    for j in range(0, S5_T, 2):
        h = jnp.concatenate([normed(j), normed(j + 1)], axis=0).astype(BF)
        zt = lax.dot_general(wt, h, (((1,), (1,)), ((), ())), preferred_element_type=F32)
        zt_ref[j] = zt[:, :S5_CB]
        zt_ref[j + 1] = zt[:, S5_CB:]


def _s5_in(x, mod, norm_g, w_in, *, layer, seq_chunks):
    nc = x.shape[0] // S5_T
    xv = x.reshape(nc, S5_T, D_MODEL)
    mod_rows = 1 if mod.shape[0] == 1 else S5_CB // seq_chunks
    in_specs = [
        pl.BlockSpec(memory_space=pl.ANY),
        _mod_spec(mod.shape[0], 1, rows=mod_rows),
        _const_spec((None, 1, D_MODEL), (3 * layer + 1, 0, 0)),
        _const_spec((None, D_MODEL, S5_WIDTH), (layer, 0, 0)),
    ]
    return pl.pallas_call(
        functools.partial(_s5_in_kernel, mod_rows=mod_rows),
        grid=(nc // S5_CB,),
        in_specs=in_specs,
        out_specs=pl.BlockSpec((S5_T, S5_WIDTH, S5_CB), lambda i: (0, 0, i)),
        out_shape=jax.ShapeDtypeStruct((S5_T, S5_WIDTH, nc), F32),
        scratch_shapes=[pltpu.VMEM((2, S5_T, S5_CB, D_MODEL), F32), pltpu.SemaphoreType.DMA((2, S5_T))],
        compiler_params=_params(1),
        name="s5_in",
    )(xv, mod, norm_g, w_in)


def _s5_core_kernel(*refs, groups, seq_chunks, has_h0, want_final):
    refs = list(refs)
    zt_ref, mt_ref, win_ref, woutt_ref, at_re_ref, at_im_ref = refs[:6]
    refs = refs[6:]
    h0_ref = refs.pop(0) if has_h0 else None
    yt_ref = refs.pop(0)
    hf_ref = refs.pop(0) if want_final else None
    T, K = S5_T, seq_chunks
    half = S5_NS // 2
    nc = zt_ref.shape[2]
    n_seq = nc // K
    n_steps = K.bit_length() - 1
    fwd = lax.broadcasted_iota(jnp.int32, (nc, half), 1) < S5_STATE
    bwd = jnp.logical_not(fwd)
    kpos = lax.broadcasted_iota(jnp.int32, (nc, half), 0) % K
    first = (fwd & (kpos == 0)) | (bwd & (kpos == K - 1))
    valid = [(fwd & (kpos >= (1 << t))) | (bwd & (kpos < K - (1 << t))) for t in range(n_steps)]
    shift = lambda x, s: jnp.where(fwd, pltpu.roll(x, s, 0), pltpu.roll(x, nc - s, 0))
    if want_final:
        fwd_row = lax.broadcasted_iota(jnp.int32, (1, S5_NS), 1) % half < S5_STATE

    xts, vs = [], []
    for g in range(groups):
        r0 = g * S5_GROUP
        xt = jnp.concatenate([zt_ref[j, r0:r0 + S5_GROUP, :] for j in range(T)], axis=0).astype(BF)
        xts.append(xt)
        vs.append(lax.dot_general(xt, win_ref[g], (((0,), (0,)), ((), ())), preferred_element_type=F32))
    yield
    s_ins, fins = [], []
    for g in range(groups):
        vr, vi = vs[g][:, :half], vs[g][:, half:]
        cr, ci = at_re_ref[g, :, :half], at_im_ref[g, :, :half]
        ar, ai = cr, ci
        if has_h0:
            h0 = jnp.concatenate([jnp.broadcast_to(h0_ref[g, s:s + 1, :], (K, S5_NS)) for s in range(n_seq)], axis=0)
            hr, hi = h0[:, :half], h0[:, half:]
        else:
            hr = hi = 0.0
        sr = jnp.where(first, hr, shift(vr, 1))
        si = jnp.where(first, hi, shift(vi, 1))
        for t in range(n_steps):
            rr = jnp.where(valid[t], shift(sr, 1 << t), 0.0)
            ri = jnp.where(valid[t], shift(si, 1 << t), 0.0)
            sr, si = sr + (cr * rr - ci * ri), si + (cr * ri + ci * rr)
            cr, ci = cr * cr - ci * ci, 2.0 * (cr * ci)
        s_ins.append(jnp.concatenate([sr, si], axis=1).astype(BF))
        if want_final:
            fins.append(jnp.concatenate([ar * sr - ai * si + vr, ar * si + ai * sr + vi], axis=1))
    yield
    for g in range(groups):
        r0 = g * S5_GROUP
        y = _dot(mt_ref[g], xts[g]) + lax.dot_general(woutt_ref[g], s_ins[g], (((1,), (1,)), ((), ())),
                                                      preferred_element_type=F32)
        for i in range(T):
            yt_ref[i, r0:r0 + S5_GROUP, :] = y[i * S5_GROUP:(i + 1) * S5_GROUP, :]
        if want_final:
            per_seq = fins[g].reshape(n_seq, K, S5_NS)
            hf_ref[g] = jnp.where(fwd_row, per_seq[:, K - 1, :], per_seq[:, 0, :])


def _s5_core_stage(zt, ops, *, layer, seq_chunks, steps, h0=None, want_final=False):
    nc = zt.shape[2]
    gb = S5_GROUPS // steps
    rows = gb * S5_GROUP
    gspec = lambda r, c: pl.BlockSpec((gb, r, c), lambda i: (layer * steps + i, 0, 0))
    slab = pl.BlockSpec((S5_T, rows, nc), lambda i: (0, i, 0))
    in_specs = [slab, gspec(S5_CL, S5_CL), gspec(S5_CL, S5_NS), gspec(S5_CL, S5_NS),
                gspec(1, S5_NS), gspec(1, S5_NS)]
    args = [zt, ops["mt"], ops["win"], ops["woutt"], ops["at_re"], ops["at_im"]]
    if h0 is not None:
        in_specs.append(pl.BlockSpec((gb, h0.shape[1], S5_NS), lambda i: (i, 0, 0)))
        args.append(h0)
    out_specs = [slab]
    out_shape = [jax.ShapeDtypeStruct(zt.shape, F32)]
    if want_final:
        n_seq = nc // seq_chunks
        out_specs.append(pl.BlockSpec((gb, n_seq, S5_NS), lambda i: (i, 0, 0)))
        out_shape.append(jax.ShapeDtypeStruct((S5_GROUPS, n_seq, S5_NS), F32))
    kern = functools.partial(_s5_core_kernel, groups=gb, seq_chunks=seq_chunks, has_h0=h0 is not None,
                             want_final=want_final)
    return dict(kernel=kern, grid=(steps,), in_specs=in_specs, args=args, out_specs=out_specs, out_shape=out_shape)


def _s5_out_kernel(yt_ref, zt_ref, d_ref, wglut_ref, wa_ref, o_hbm, obuf, sem):
    i = pl.program_id(0)
    last = pl.num_programs(0) - 1
    slot = i % 2
    store = functools.partial(_token_slab_copies, o_hbm, obuf, sem, to_hbm=True)

    @pl.when(i >= 2)
    def _():
        for c in store(i - 2, slot):
            c.wait()

    d = d_ref[...]
    for j in range(0, S5_T, 2):
        pre = jnp.concatenate([yt_ref[j] + d * zt_ref[j], yt_ref[j + 1] + d * zt_ref[j + 1]], axis=1)
        ya = jax.nn.gelu(pre)
        ya = ya * jax.nn.sigmoid(_dot(wglut_ref[...], ya.astype(BF)))
        pa = lax.dot_general(ya.astype(BF), wa_ref[...], (((0,), (0,)), ((), ())), preferred_element_type=F32)
        obuf[slot, j] = pa[:S5_CB, :]
        obuf[slot, j + 1] = pa[S5_CB:, :]
    for c in store(i, slot):
        c.start()

    @pl.when(i == last)
    def _():
        for c in store(i, slot):
            c.wait()

    @pl.when((i == last) & (i >= 1))
    def _():
        for c in store(i - 1, 1 - slot):
            c.wait()


def _s5_out(yt, zt, s5_d_col, w_glu_t, w_br_a, *, layer):
    nc = yt.shape[2]
    slab = pl.BlockSpec((S5_T, S5_WIDTH, S5_CB), lambda i: (0, 0, i))
    out = pl.pallas_call(
        _s5_out_kernel,
        grid=(nc // S5_CB,),
        in_specs=[slab, slab,
                  _const_spec((None, S5_WIDTH, 1), (layer, 0, 0)),
                  _const_spec((None, S5_WIDTH, S5_WIDTH), (layer, 0, 0)),
                  _const_spec((None, S5_WIDTH, D_MODEL), (layer, 0, 0))],
        out_specs=pl.BlockSpec(memory_space=pl.ANY),
        out_shape=jax.ShapeDtypeStruct((nc, S5_T, D_MODEL), F32),
        scratch_shapes=[pltpu.VMEM((2, S5_T, S5_CB, D_MODEL), F32), pltpu.SemaphoreType.DMA((2, S5_T))],
        compiler_params=_params(1),
        name="s5_out",
    )(yt, zt, s5_d_col, w_glu_t, w_br_a)
    return out.reshape(nc * S5_T, D_MODEL)


def _mix_out_kernel(x_ref, mod_ref, g_ref, pa_ref, yb_ref, yc_ref, wb_ref, wc_ref, wgate_ref, bgate_ref,
                    wout_ref, o_ref):
    x = x_ref[...]
    h = _mod_norm(x, g_ref[...], mod_ref[0, 4:5, :], mod_ref[0, 3:4, :]).astype(BF)

    def gate(k):
        lo = k * D_MODEL
        return jax.nn.sigmoid(_dot(h, wgate_ref[:, lo:lo + D_MODEL]) + bgate_ref[:, lo:lo + D_MODEL])

    merged = gate(0) * pa_ref[...]
    merged = merged + gate(1) * _dot(yb_ref[...], wb_ref[...])
    merged = merged + gate(2) * _dot(yc_ref[...], wc_ref[...])
    y = _dot(merged.astype(BF), wout_ref[...])
    o_ref[...] = x + mod_ref[0, 5:6, :] * y


def _mix_out(x, mod, norm_g, pa, yb, yc, w_b, w_c, w_gate, b_gate, w_out, *, layer, seq_tokens):
    n = x.shape[0]
    tm = FFN_TM
    tok = lambda c: pl.BlockSpec((tm, c), lambda i: (i, 0))
    return pl.pallas_call(
        _mix_out_kernel,
        grid=(n // tm,),
        in_specs=[
            tok(D_MODEL),
            _mod_spec(mod.shape[0], seq_tokens // tm),
            _const_spec((None, 1, D_MODEL), (3 * layer + 1, 0, 0)),
            tok(D_MODEL), tok(CONV_WIDTH), tok(SG_WIDTH),
            _const_spec((None, CONV_WIDTH, D_MODEL), (layer, 0, 0)),
            _const_spec((None, SG_WIDTH, D_MODEL), (layer, 0, 0)),
            _const_spec((None, D_MODEL, 3 * D_MODEL), (layer, 0, 0)),
            _const_spec((None, 1, 3 * D_MODEL), (layer, 0, 0)),
            _const_spec((None, D_MODEL, D_MODEL), (layer, 0, 0)),
        ],
        out_specs=tok(D_MODEL),
        out_shape=jax.ShapeDtypeStruct((n, D_MODEL), F32),
        compiler_params=_params(1),
        name="mix_out",
    )(x, mod, norm_g, pa, yb, yc, w_b, w_c, w_gate, b_gate, w_out)


def _grid_pos_embed(n_tokens, dim):
    rows = n_tokens // GRID_W
    rr, cc = jnp.meshgrid(jnp.arange(rows, dtype=F32), jnp.arange(GRID_W, dtype=F32), indexing='ij')
    quarter = dim // 4
    omega = 1.0 / (10000.0 ** (jnp.arange(quarter, dtype=F32) / quarter))

    def emb(p):
        ang = p.reshape(-1)[:, None] * omega[None, :]
        return jnp.concatenate([jnp.sin(ang), jnp.cos(ang)], axis=-1)

    return jnp.concatenate([emb(rr), emb(cc)], axis=-1)


def kernel(x_prompt, x_sample, state_ssm, c, c_ctx, w_mod, b_mod, norm_g, ffn_w1, ffn_w2, w_in, w_gate, b_gate,
           s5_a_re, s5_a_im, s5_log_dt, s5_b_re, s5_b_im, s5_c_re, s5_c_im, s5_d, s5_w_glu, w_br_a, conv_w,
           conv_b, conv_ln_g, conv_ln_b, w_br_b, sg_ln_g, sg_ln_b, sg_w, sg_b, w_br_c, w_out, final_g):
    batch, seq, _ = x_prompt.shape
    dec_batch, dec_seq, _ = x_sample.shape
    assert (batch * seq) % TOK_BLOCK == 0 and TOK_BLOCK % seq == 0 and dec_seq % TOK_BLOCK == 0
    assert seq % SG_CHUNK == 0 and seq % S5_T == 0 and dec_seq % S5_T == 0
    ctx_chunks, smp_chunks = seq // S5_T, dec_seq // S5_T
    assert S5_CB % ctx_chunks == 0 and S5_CB % smp_chunks == 0
    assert ctx_chunks & (ctx_chunks - 1) == 0 and smp_chunks & (smp_chunks - 1) == 0
    assert 1 + dec_batch <= MOD_ROWS

    cond = jnp.zeros((MOD_ROWS, D_MODEL), F32).at[0].set(c_ctx).at[1:1 + dec_batch].set(c)
    mod_stage = _modulation_stage(cond, w_mod, b_mod)
    [mod_all], prep_out, [w1_first], [w2_first] = _call(
        [mod_stage, _s5_prep_stage(s5_a_re, s5_a_im, s5_log_dt, s5_b_re, s5_b_im, s5_c_re, s5_c_im),
         _cast_stage(ffn_w1, (0, 0), mod_stage["grid"][0]), _cast_stage(ffn_w2, (0, 0), mod_stage["grid"][0])],
        "mod_prep")
    mod_all = mod_all.reshape(DEPTH, MOD_ROWS, N_MOD, D_MODEL)
    s5_ops = dict(zip(S5_OPS, prep_out))
    pos = _grid_pos_embed(dec_seq, D_MODEL)

    ffn_w = {(0, 0): (w1_first, w2_first)}
    mixer_w = dict(w_in=w_in, w_gate=w_gate, w_out=w_out, w_a=w_br_a, w_b=w_br_b, w_c=w_br_c)
    w_glu_t = s5_w_glu.transpose(0, 2, 1).astype(BF)
    norm_rows = norm_g.reshape(DEPTH * 3, 1, D_MODEL)
    rows = lambda a: a.reshape(DEPTH, 1, -1)
    sgw_cat = sg_w.transpose(0, 2, 1, 3).reshape(DEPTH, SG_CHUNK, SG_HEADS * SG_CHUNK).astype(BF)
    sgb_full = jnp.repeat(sg_b.transpose(0, 2, 1), SG_HEAD_DIM, axis=2)
    s5_d_col = s5_d.reshape(DEPTH, S5_WIDTH, 1)

    conv_w8 = jnp.broadcast_to(conv_w[:, :, None, :], (DEPTH, CONV_K, SUBLANES, CONV_WIDTH))

    ctx = dict(n_seq=batch, seq_len=seq, mod_rows=slice(0, 1), mod_tokens=batch * seq)
    smp = dict(n_seq=dec_batch, seq_len=dec_seq, mod_rows=slice(1, 1 + dec_batch), mod_tokens=dec_seq)
    x_ctx = x_prompt.reshape(batch * seq, D_MODEL)
    x_smp = x_sample.reshape(dec_batch * dec_seq, D_MODEL)
    ffn_steps = x_ctx.shape[0] // FFN_TM
    assert x_smp.shape[0] // FFN_TM == ffn_steps and S5_GROUPS % ffn_steps == 0

    assert DEPTH * S5_GROUPS // S5_PREP_GB == ffn_steps

    def ffn_stage(gr, x, l, which, **kw):
        return _ffn_stage(x, mod_all[l, gr["mod_rows"]], norm_rows, *ffn_w[l, which], layer=l, which=which,
                          seq_tokens=gr["mod_tokens"], **kw)

    def cast_stages(l, which):
        if l >= DEPTH:
            return []
        return [_cast_stage(ffn_w1, (l, which), ffn_steps), _cast_stage(ffn_w2, (l, which), ffn_steps)]

    def keep_casts(l, which, outs):
        if outs:
            ffn_w[l, which] = (outs[0][0], outs[1][0])

    def branches_in(gr, x, l):
        mod = mod_all[l, gr["mod_rows"]]
        yb, yc = _mix_in(x, mod, norm_rows, mixer_w["w_in"], conv_w8, rows(conv_b), rows(conv_ln_g), rows(conv_ln_b),
                         rows(sg_ln_g), rows(sg_ln_b), sgw_cat, sgb_full, layer=l, seq_len=gr["seq_len"])
        zt = _s5_in(x, mod, norm_rows, mixer_w["w_in"], layer=l, seq_chunks=gr["seq_len"] // S5_T)
        return yb, yc, zt

    def s5_stage(gr, zt, l, **kw):
        return _s5_core_stage(zt, s5_ops, layer=l, seq_chunks=gr["seq_len"] // S5_T, steps=ffn_steps, **kw)

    def branches_out(gr, x, l, yb, yc, zt, yt):
        pa = _s5_out(yt, zt, s5_d_col, w_glu_t, mixer_w["w_a"], layer=l)
        return _mix_out(x, mod_all[l, gr["mod_rows"]], norm_rows, pa, yb, yc, mixer_w["w_b"], mixer_w["w_c"],
                        mixer_w["w_gate"], rows(b_gate), mixer_w["w_out"], layer=l, seq_tokens=gr["mod_tokens"])

    ctx_states = []
    for l in range(DEPTH):
        riders = []
        if l == 0:
            riders = cast_stages(0, 1) + [_cast_stage(w.reshape(-1, w.shape[-1]), (), ffn_steps)
                                          for w in mixer_w.values()]
        [x_ctx], *outs = _call([ffn_stage(ctx, x_ctx, l, 0)] + riders, "ffn")
        if l == 0:
            keep_casts(0, 1, outs[:2])
            mixer_w = {k: o[0].reshape(mixer_w[k].shape) for k, o in zip(mixer_w, outs[2:])}
        yb_c, yc_c, zt_c = branches_in(ctx, x_ctx, l)
        [x_smp], [yt_c, hf], *outs = _call([ffn_stage(smp, x_smp, l, 0, pos=pos if l == 0 else None),
                                             s5_stage(ctx, zt_c, l, want_final=True)] + cast_stages(l + 1, 0),
                                            "ffn_s5")
        keep_casts(l + 1, 0, outs)
        hf = hf.reshape(S5_GROUPS, batch, 2, 2, S5_STATE)
        ctx_states.append(hf.transpose(1, 3, 0, 4, 2))
        x_ctx = branches_out(ctx, x_ctx, l, yb_c, yc_c, zt_c, yt_c)
        yb_s, yc_s, zt_s = branches_in(smp, x_smp, l)
        h0 = state_ssm[:, l].transpose(2, 0, 4, 1, 3).reshape(S5_GROUPS, dec_batch, S5_NS)
        final = dict(final_g=final_g) if l == DEPTH - 1 else {}
        [x_ctx], [yt_s], *outs = _call([ffn_stage(ctx, x_ctx, l, 1, **final), s5_stage(smp, zt_s, l, h0=h0)]
                                       + cast_stages(l + 1, 1), "ffn_s5")
        keep_casts(l + 1, 1, outs)
        x_smp = branches_out(smp, x_smp, l, yb_s, yc_s, zt_s, yt_s)
        [x_smp], = _call([ffn_stage(smp, x_smp, l, 1, **final)], "ffn")
    y_prompt = x_ctx.reshape(batch, seq, D_MODEL)
    y_sample = x_smp.reshape(dec_batch, dec_seq, D_MODEL)
    new_state_ssm = jnp.stack(ctx_states, axis=1)
    return (y_prompt, y_sample, new_state_ssm)
```

```python
import functools
import inspect
import math

import jax
import jax.numpy as jnp
from jax import lax
from jax.experimental import pallas as pl
from jax.experimental.pallas import tpu as pltpu

D_MODEL = 1024
DEPTH = 2
GRID_W = 64
D_FF = 2816
S5_WIDTH = 512
S5_GROUP = 16
S5_GROUPS = 32
S5_STATE = 64
CONV_WIDTH = 256
CONV_K = 31
SG_WIDTH = 256
SG_CHUNK = 128
SG_HEADS = 4
SG_HEAD_DIM = SG_WIDTH // SG_HEADS
BRANCH_COLS = 2 * CONV_WIDTH
IN_COLS = S5_WIDTH + 2 * BRANCH_COLS
N_MOD = 9
EPS = 1e-6

LANES = 128
SUBLANES = 8
S5_T = 16
S5_CL = S5_T * S5_GROUP
S5_NS = 4 * S5_STATE
S5_PREP_GB = 4
S5_POW_ROWS = SUBLANES * (S5_T // SUBLANES + 1)
S5_CB = 128
MOD_ROWS = 16
MOD_TN = 1152
TOK_BLOCK = 1024
FFN_TM = 512
FFN_SUBTILES = 4
CONV_RC = 32
CONV_PAD = 16
CONV_SHIFT_ROWS = SUBLANES * ((CONV_PAD - CONV_K // 2 + CONV_K - 1) // SUBLANES)
VMEM_LIMIT = 56 * 1024 * 1024

BF = jnp.bfloat16
F32 = jnp.float32
HIGHEST = lax.Precision.HIGHEST


def _dot(a, b):
    return jnp.dot(a, b, preferred_element_type=F32)


def _dot_exact(a, b):
    return jnp.dot(a, b, preferred_element_type=F32, precision=HIGHEST)


def _const_spec(block, index):
    return pl.BlockSpec(block, lambda *_: index, pipeline_mode=pl.Buffered(1))


def _params(n_axes=1):
    return pltpu.CompilerParams(dimension_semantics=("arbitrary",) * n_axes,
                                vmem_limit_bytes=VMEM_LIMIT)


def _call(stages, name):
    grid = stages[0]["grid"]
    assert all(s["grid"] == grid for s in stages)
    n_in = [len(s["in_specs"]) for s in stages]
    n_out = [len(s["out_specs"]) for s in stages]
    total_in, total_out = sum(n_in), sum(n_out)

    def body(*refs):
        bound, i, o = [], 0, 0
        for s, ni, no in zip(stages, n_in, n_out):
            bound.append((s["kernel"], refs[i:i + ni] + refs[total_in + o:total_in + o + no]))
            i, o = i + ni, o + no
        live = [run for run in (kern(*r) for kern, r in reversed(bound)) if inspect.isgenerator(run)]
        while live:
            for gen in list(live):
                if next(gen, "done") == "done":
                    live.remove(gen)

    outs = pl.pallas_call(
        body,
        grid=grid,
        in_specs=[spec for s in stages for spec in s["in_specs"]],
        out_specs=[spec for s in stages for spec in s["out_specs"]],
        out_shape=[shape for s in stages for shape in s["out_shape"]],
        compiler_params=_params(len(grid)),
        name=name,
    )(*[a for s in stages for a in s["args"]])
    split, o = [], 0
    for no in n_out:
        split.append(list(outs[o:o + no]))
        o += no
    return split


def _mod_norm(x, g, sc, sh):
    var = jnp.mean(x * x, axis=-1, keepdims=True)
    return x * lax.rsqrt(var + EPS) * (g * (1.0 + sc)) + sh


def _gelu(x):
    c = math.sqrt(2.0 / math.pi)
    return (0.5 * x) * (1.0 + jnp.tanh(x * (c + (c * 0.044715) * (x * x))))


def _layernorm(x, g, b):
    mu = jnp.mean(x, axis=-1, keepdims=True)
    xc = x - mu
    var = jnp.mean(xc * xc, axis=-1, keepdims=True)
    return xc * lax.rsqrt(var + EPS) * g + b


def _mod_spec(n_mod, blocks_per_mod, rows=1):
    if n_mod == 1:
        return pl.BlockSpec((1, N_MOD, D_MODEL), lambda i: (0, 0, 0))
    return pl.BlockSpec((rows, N_MOD, D_MODEL), lambda i: (i // blocks_per_mod, 0, 0))


def _mod_kernel(c_ref, w_ref, b_ref, o_ref):
    c = c_ref[...]
    a = (c * jax.nn.sigmoid(c)).astype(BF)
    o_ref[...] = _dot(a, w_ref[...].astype(BF)) + b_ref[...]


def _modulation_stage(cond, w_mod, b_mod):
    n = N_MOD * D_MODEL
    tiles = n // MOD_TN
    tile = lambda i: (i // tiles, 0, i % tiles)
    return dict(
        kernel=_mod_kernel, grid=(DEPTH * tiles,), args=[cond, w_mod, b_mod.reshape(DEPTH, 1, n)],
        in_specs=[pl.BlockSpec((MOD_ROWS, D_MODEL), lambda i: (0, 0)),
                  pl.BlockSpec((None, D_MODEL, MOD_TN), tile),
                  pl.BlockSpec((None, 1, MOD_TN), tile)],
        out_specs=[pl.BlockSpec((None, MOD_ROWS, MOD_TN), tile)],
        out_shape=[jax.ShapeDtypeStruct((DEPTH, MOD_ROWS, n), F32)])


def _ffn_kernel(*refs, mod_base, add_pos, final):
    refs = list(refs)
    x_ref = refs.pop(0)
    pos_ref = refs.pop(0) if add_pos else None
    mod_ref, g_ref, w1g_ref, w1u_ref, w2_ref = refs[:5]
    refs = refs[5:]
    fg_ref = refs.pop(0) if final else None
    o_ref = refs.pop(0)

    sh = mod_ref[0, mod_base:mod_base + 1, :]
    sc = mod_ref[0, mod_base + 1:mod_base + 2, :]
    gt = mod_ref[0, mod_base + 2:mod_base + 3, :]
    rows = x_ref.shape[0] // FFN_SUBTILES
    sub = [slice(r * rows, (r + 1) * rows) for r in range(FFN_SUBTILES)]
    xs, hs, gs, accs = [], [], [], []
    for r in sub:
        x = x_ref[r, :] + pos_ref[r, :] if add_pos else x_ref[r, :]
        xs.append(x)
        hs.append(_mod_norm(x, g_ref[...], sc, sh).astype(BF))
        gs.append(_dot(hs[-1], w1g_ref[...]))
    yield
    for h, g in zip(hs, gs):
        u = _dot(h, w1u_ref[...])
        accs.append((g * jax.nn.sigmoid(g) * u).astype(BF))
    yield
    for r, x, a in zip(sub, xs, accs):
        y = _dot(a, w2_ref[...])
        xn = x + (0.5 * gt) * y
        if final:
            var = jnp.mean(xn * xn, axis=-1, keepdims=True)
            xn = xn * lax.rsqrt(var + EPS) * fg_ref[...]
        o_ref[r, :] = xn
    yield


def _cast_kernel(x_ref, o_ref):
    o_ref[...] = x_ref[...].astype(o_ref.dtype)


def _cast_stage(w, index, steps):
    rows, cols = w.shape[-2:]
    lead = (None,) * len(index)
    return dict(kernel=_cast_kernel, grid=(steps,), args=[w],
                in_specs=[pl.BlockSpec(lead + (rows // steps, cols), lambda i: tuple(index) + (i, 0))],
                out_specs=[pl.BlockSpec((rows // steps, cols), lambda i: (i, 0))],
                out_shape=[jax.ShapeDtypeStruct((rows, cols), BF)])


def _ffn_stage(x, mod, norm_g, w1, w2, *, layer, which, seq_tokens, pos=None, final_g=None):
    n = x.shape[0]
    tm = FFN_TM
    in_specs = [pl.BlockSpec((tm, D_MODEL), lambda i: (i, 0))]
    args = [x]
    if pos is not None:
        pos_blocks = pos.shape[0] // tm
        in_specs.append(pl.BlockSpec((tm, D_MODEL), lambda i: (i % pos_blocks, 0)))
        args.append(pos)
    in_specs += [
        _mod_spec(mod.shape[0], seq_tokens // tm),
        _const_spec((None, 1, D_MODEL), (3 * layer + 2 * which, 0, 0)),
        _const_spec((D_MODEL, D_FF), (0, 0)),
        _const_spec((D_MODEL, D_FF), (0, 1)),
        _const_spec((D_FF, D_MODEL), (0, 0)),
    ]
    args += [mod, norm_g, w1, w1, w2]
    if final_g is not None:
        in_specs.append(_const_spec((1, D_MODEL), (0, 0)))
        args.append(final_g.reshape(1, D_MODEL))
    kern = functools.partial(_ffn_kernel, mod_base=6 * which, add_pos=pos is not None,
                             final=final_g is not None)
    return dict(kernel=kern, grid=(n // tm,), in_specs=in_specs, args=args,
                out_specs=[pl.BlockSpec((tm, D_MODEL), lambda i: (i, 0))],
                out_shape=[jax.ShapeDtypeStruct((n, D_MODEL), F32)])


def _mix_in_kernel(x_ref, mod_ref, g_ref, winb_ref, winc_ref, cw_ref, cb_ref, clg_ref, clb_ref,
                   sglg_ref, sglb_ref, sgw_ref, sgb_ref, yb_ref, yc_ref, pad_ref, conv_ref, *, seq_len):
    n_seq = TOK_BLOCK // seq_len
    x = x_ref[...]
    h = _mod_norm(x, g_ref[...], mod_ref[0, 4:5, :], mod_ref[0, 3:4, :]).astype(BF)

    zb = _dot(h, winb_ref[...])
    gl = zb[:, :CONV_WIDTH] * jax.nn.sigmoid(zb[:, CONV_WIDTH:])
    zeros = jnp.zeros((CONV_PAD, CONV_WIDTH), F32)
    shifted_rows = seq_len + CONV_SHIFT_ROWS
    padded_rows = seq_len + 2 * CONV_PAD
    for s in range(n_seq):
        padded = jnp.concatenate([zeros, gl[s * seq_len:(s + 1) * seq_len, :], zeros], axis=0)
        pad_ref[0, s] = padded[:shifted_rows, :]
        for r in range(1, SUBLANES):
            pad_ref[r, s] = pltpu.roll(padded, padded_rows - r, 0)[:shifted_rows, :]
    first = CONV_PAD - CONV_K // 2
    tiles = CONV_RC // SUBLANES
    n_q = CONV_SHIFT_ROWS // SUBLANES + 1
    for s in range(n_seq):
        def conv_rows(r, carry, s=s):
            r0 = pl.multiple_of(r * CONV_RC, CONV_RC)
            accs = [None] * n_q
            for shift in range(SUBLANES):
                span = pad_ref[shift, s, pl.ds(r0, CONV_RC + CONV_SHIFT_ROWS), :]
                span = span.reshape(tiles + n_q - 1, SUBLANES, CONV_WIDTH)
                for q in range(n_q):
                    k = SUBLANES * q + shift - first
                    if 0 <= k < CONV_K:
                        term = cw_ref[k] * span[q:q + tiles]
                        accs[q] = term if accs[q] is None else accs[q] + term
            acc = (accs[0] + accs[1]) + (accs[2] + accs[3])
            conv_ref[pl.ds(pl.multiple_of(s * seq_len + r0, CONV_RC), CONV_RC), :] = acc.reshape(CONV_RC, CONV_WIDTH)
            return carry
        lax.fori_loop(0, seq_len // CONV_RC, conv_rows, 0)
    y = _layernorm(conv_ref[...] + cb_ref[...], clg_ref[...], clb_ref[...])
    yb_ref[...] = (y * jax.nn.sigmoid(y)).astype(BF)

    zc = _gelu(_dot(h, winc_ref[...]))
    u = zc[:, :SG_WIDTH]
    v = _layernorm(zc[:, SG_WIDTH:], sglg_ref[...], sglb_ref[...]).astype(BF)
    head = lax.broadcasted_iota(jnp.int32, (SG_CHUNK, SG_WIDTH), 1) // SG_HEAD_DIM
    zero = jnp.zeros((SG_CHUNK, SG_WIDTH), BF)
    for n in range(TOK_BLOCK // SG_CHUNK):
        vn = v[n * SG_CHUNK:(n + 1) * SG_CHUNK, :]
        vcat = jnp.concatenate([jnp.where(head == hh, vn, zero) for hh in range(SG_HEADS)], axis=0)
        sgate = _dot(sgw_ref[...], vcat) + sgb_ref[...]
        yc_ref[n * SG_CHUNK:(n + 1) * SG_CHUNK, :] = (u[n * SG_CHUNK:(n + 1) * SG_CHUNK, :] * sgate).astype(BF)


def _mix_in(x, mod, norm_g, w_in, conv_w, conv_b, conv_ln_g, conv_ln_b, sg_ln_g, sg_ln_b, sgw_cat, sgb_full,
            *, layer, seq_len):
    n = x.shape[0]
    n_seq = TOK_BLOCK // seq_len
    vec = lambda width: _const_spec((None, 1, width), (layer, 0, 0))
    kern = functools.partial(_mix_in_kernel, seq_len=seq_len)
    return pl.pallas_call(
        kern,
        grid=(n // TOK_BLOCK,),
        in_specs=[
            pl.BlockSpec((TOK_BLOCK, D_MODEL), lambda i: (i, 0)),
            _mod_spec(mod.shape[0], max(seq_len // TOK_BLOCK, 1)),
            _const_spec((None, 1, D_MODEL), (3 * layer + 1, 0, 0)),
            _const_spec((None, D_MODEL, BRANCH_COLS), (layer, 0, S5_WIDTH // BRANCH_COLS)),
            _const_spec((None, D_MODEL, BRANCH_COLS), (layer, 0, S5_WIDTH // BRANCH_COLS + 1)),
            _const_spec((None, CONV_K, SUBLANES, CONV_WIDTH), (layer, 0, 0, 0)),
            vec(CONV_WIDTH), vec(CONV_WIDTH), vec(CONV_WIDTH), vec(SG_WIDTH), vec(SG_WIDTH),
            _const_spec((None, SG_CHUNK, SG_HEADS * SG_CHUNK), (layer, 0, 0)),
            _const_spec((None, SG_CHUNK, SG_WIDTH), (layer, 0, 0)),
        ],
        out_specs=[
            pl.BlockSpec((TOK_BLOCK, CONV_WIDTH), lambda i: (i, 0)),
            pl.BlockSpec((TOK_BLOCK, SG_WIDTH), lambda i: (i, 0)),
        ],
        out_shape=[
            jax.ShapeDtypeStruct((n, CONV_WIDTH), BF),
            jax.ShapeDtypeStruct((n, SG_WIDTH), BF),
        ],
        scratch_shapes=[pltpu.VMEM((SUBLANES, n_seq, seq_len + CONV_SHIFT_ROWS, CONV_WIDTH), F32),
                        pltpu.VMEM((TOK_BLOCK, CONV_WIDTH), F32)],
        compiler_params=_params(1),
        name="mix_in",
    )(x, mod, norm_g, w_in, w_in, conv_w, conv_b, conv_ln_g, conv_ln_b, sg_ln_g, sg_ln_b, sgw_cat, sgb_full)


def _s5_prep_kernel(arow_re_ref, arow_im_ref, dtrow_ref, b4_re_ref, b4_im_ref, c4_re_ref, c4_im_ref,
                    mt_ref, win_ref, woutt_ref, at_re_ref, at_im_ref):
    T, H = S5_T, S5_GROUP
    lanes = 2 * S5_CL
    lane = lax.broadcasted_iota(jnp.int32, (1, S5_NS), 1)
    lane_im = lane >= 2 * S5_STATE
    lane_bwd = (lane // S5_STATE) % 2 == 1
    e_col = lax.broadcasted_iota(jnp.int32, (S5_POW_ROWS, 1), 0).astype(F32)
    zeros = jnp.zeros((H, S5_NS), F32)
    for g in range(S5_PREP_GB):
        a_re, a_im = arow_re_ref[g], arow_im_ref[g]
        dt = jnp.exp(dtrow_ref[g])
        mag = jnp.exp(a_re * dt * e_col)
        ang = a_im * dt * e_col
        pw_re, pw_im = mag * jnp.cos(ang), mag * jnp.sin(ang)
        den = a_re * a_re + a_im * a_im
        nr, ni = pw_re[1:2, :] - 1.0, pw_im[1:2, :]
        q_re = (nr * a_re + ni * a_im) / den
        q_im = (ni * a_re - nr * a_im) / den
        b_re, b_im = b4_re_ref[g], b4_im_ref[g]
        bb_re = q_re * b_re - q_im * b_im
        bb_im = q_re * b_im + q_im * b_re
        c_re, c_im = c4_re_ref[g], c4_im_ref[g]

        def power(e_f, e_b, pw_re=pw_re, pw_im=pw_im):
            return (jnp.where(lane_bwd, pw_re[e_b:e_b + 1, :], pw_re[e_f:e_f + 1, :]),
                    jnp.where(lane_bwd, pw_im[e_b:e_b + 1, :], pw_im[e_f:e_f + 1, :]))

        def times_b(e_f, e_b, bb_re=bb_re, bb_im=bb_im, power=power):
            p_re, p_im = power(e_f, e_b)
            return jnp.where(lane_im, p_re * bb_im + p_im * bb_re, p_re * bb_re - p_im * bb_im)

        def times_c(e_f, e_b, c_re=c_re, c_im=c_im, power=power):
            p_re, p_im = power(e_f, e_b)
            return jnp.where(lane_im, -(c_re * p_im + c_im * p_re), c_re * p_re - c_im * p_im)

        win_ref[g] = jnp.concatenate([times_b(T - 1 - j, j) for j in range(T)], axis=0).astype(BF)
        woutt_ref[g] = jnp.concatenate([times_c(i + 1, T - i) for i in range(T)], axis=0).astype(BF)
        blocks = []
        for b in range(2 * T - 1):
            m = T - 1 - b
            w = times_b(abs(m), abs(m))
            if m > 0:
                w = jnp.where(lane_bwd, 0.0, w)
            elif m < 0:
                w = jnp.where(lane_bwd, w, 0.0)
            blocks.append(w)
        ystack = jnp.concatenate(blocks + [zeros], axis=0)
        c_signed = jnp.where(lane_im, -c_im, c_re)
        krev = lax.dot_general(c_signed, ystack, (((1,), (1,)), ((), ())), preferred_element_type=F32,
                               precision=HIGHEST)
        mt_ref[g] = jnp.concatenate(
            [pltpu.roll(krev, (lanes - H * (T - 1 - i)) % lanes, 1)[:, :S5_CL] for i in range(T)],
            axis=0).astype(BF)
        at_re_ref[g] = pw_re[T:T + 1, :]
        at_im_ref[g] = pw_im[T:T + 1, :]


S5_OPS = ("mt", "win", "woutt", "at_re", "at_im")


def _s5_prep_stage(a_re, a_im, log_dt, b_re, b_im, c_re, c_im):
    G, P, H = S5_GROUPS, S5_STATE, S5_GROUP
    n = DEPTH * G
    rep = lambda a: jnp.broadcast_to(a.transpose(0, 2, 1, 3)[:, :, None, :, :],
                                     (DEPTH, G, 2, 2, P)).reshape(n, 1, S5_NS)
    dt4 = rep(jnp.broadcast_to(log_dt[..., None], (DEPTH, 2, G, P)))
    c4 = lambda c: jnp.broadcast_to(c.transpose(0, 2, 3, 1, 4)[:, :, :, None, :, :],
                                    (DEPTH, G, H, 2, 2, P)).reshape(n, H, S5_NS)
    b4 = lambda b: c4(b.transpose(0, 1, 2, 4, 3))
    blk = lambda r, c: pl.BlockSpec((S5_PREP_GB, r, c), lambda i: (i, 0, 0))
    op = jax.ShapeDtypeStruct((n, S5_CL, S5_CL), BF)
    row = jax.ShapeDtypeStruct((n, 1, S5_NS), F32)
    return dict(
        kernel=_s5_prep_kernel, grid=(n // S5_PREP_GB,),
        in_specs=[blk(1, S5_NS), blk(1, S5_NS), blk(1, S5_NS), blk(H, S5_NS), blk(H, S5_NS), blk(H, S5_NS),
                  blk(H, S5_NS)],
        args=[rep(a_re), rep(a_im), dt4, b4(b_re), b4(b_im), c4(c_re), c4(c_im)],
        out_specs=[blk(S5_CL, S5_CL), blk(S5_CL, S5_NS), blk(S5_CL, S5_NS), blk(1, S5_NS), blk(1, S5_NS)],
        out_shape=[op, op, op, row, row])


def _token_slab_copies(hbm_ref, buf_ref, sem_ref, block, slot, to_hbm):
    copies = []
    for j in range(S5_T):
        hbm = hbm_ref.at[pl.ds(block * S5_CB, S5_CB), j, :]
        vmem = buf_ref.at[slot, j]
        src, dst = (vmem, hbm) if to_hbm else (hbm, vmem)
        copies.append(pltpu.make_async_copy(src, dst, sem_ref.at[slot, j]))
    return copies


def _s5_in_kernel(x_hbm, mod_ref, g_ref, w_ref, zt_ref, xbuf, sem, *, mod_rows):
    g = g_ref[...]
    rows = S5_CB // mod_rows
    i = pl.program_id(0)
    slot = i % 2
    fetch = functools.partial(_token_slab_copies, x_hbm, xbuf, sem, to_hbm=False)

    @pl.when(i == 0)
    def _():
        for c in fetch(0, 0):
            c.start()

    @pl.when(i + 1 < pl.num_programs(0))
    def _():
        for c in fetch(i + 1, 1 - slot):
            c.start()

    for c in fetch(i, slot):
        c.wait()

    def normed(j):
        x = xbuf[slot, j]
        parts = [_mod_norm(x[m * rows:(m + 1) * rows, :], g, mod_ref[m, 4:5, :], mod_ref[m, 3:4, :])
                 for m in range(mod_rows)]
        return parts[0] if mod_rows == 1 else jnp.concatenate(parts, axis=0)

    wt = w_ref[...].T
    for j in range(0, S5_T, 2):
        h = jnp.concatenate([normed(j), normed(j + 1)], axis=0).astype(BF)
        zt = lax.dot_general(wt, h, (((1,), (1,)), ((), ())), preferred_element_type=F32)
        zt_ref[j] = zt[:, :S5_CB]
        zt_ref[j + 1] = zt[:, S5_CB:]


def _s5_in(x, mod, norm_g, w_in, *, layer, seq_chunks):
    nc = x.shape[0] // S5_T
    xv = x.reshape(nc, S5_T, D_MODEL)
    mod_rows = 1 if mod.shape[0] == 1 else S5_CB // seq_chunks
    in_specs = [
        pl.BlockSpec(memory_space=pl.ANY),
        _mod_spec(mod.shape[0], 1, rows=mod_rows),
        _const_spec((None, 1, D_MODEL), (3 * layer + 1, 0, 0)),
        _const_spec((None, D_MODEL, S5_WIDTH), (layer, 0, 0)),
    ]
    return pl.pallas_call(
        functools.partial(_s5_in_kernel, mod_rows=mod_rows),
        grid=(nc // S5_CB,),
        in_specs=in_specs,
        out_specs=pl.BlockSpec((S5_T, S5_WIDTH, S5_CB), lambda i: (0, 0, i)),
        out_shape=jax.ShapeDtypeStruct((S5_T, S5_WIDTH, nc), F32),
        scratch_shapes=[pltpu.VMEM((2, S5_T, S5_CB, D_MODEL), F32), pltpu.SemaphoreType.DMA((2, S5_T))],
        compiler_params=_params(1),
        name="s5_in",
    )(xv, mod, norm_g, w_in)


def _s5_core_kernel(*refs, groups, seq_chunks, has_h0, want_final):
    refs = list(refs)
    zt_ref, mt_ref, win_ref, woutt_ref, at_re_ref, at_im_ref = refs[:6]
    refs = refs[6:]
    h0_ref = refs.pop(0) if has_h0 else None
    yt_ref = refs.pop(0)
    hf_ref = refs.pop(0) if want_final else None
    T, K = S5_T, seq_chunks
    half = S5_NS // 2
    nc = zt_ref.shape[2]
    n_seq = nc // K
    n_steps = K.bit_length() - 1
    fwd = lax.broadcasted_iota(jnp.int32, (nc, half), 1) < S5_STATE
    bwd = jnp.logical_not(fwd)
    kpos = lax.broadcasted_iota(jnp.int32, (nc, half), 0) % K
    first = (fwd & (kpos == 0)) | (bwd & (kpos == K - 1))
    valid = [(fwd & (kpos >= (1 << t))) | (bwd & (kpos < K - (1 << t))) for t in range(n_steps)]
    shift = lambda x, s: jnp.where(fwd, pltpu.roll(x, s, 0), pltpu.roll(x, nc - s, 0))
    if want_final:
        fwd_row = lax.broadcasted_iota(jnp.int32, (1, S5_NS), 1) % half < S5_STATE

    xts, vs = [], []
    for g in range(groups):
        r0 = g * S5_GROUP
        xt = jnp.concatenate([zt_ref[j, r0:r0 + S5_GROUP, :] for j in range(T)], axis=0).astype(BF)
        xts.append(xt)
        vs.append(lax.dot_general(xt, win_ref[g], (((0,), (0,)), ((), ())), preferred_element_type=F32))
    yield
    s_ins, fins = [], []
    for g in range(groups):
        vr, vi = vs[g][:, :half], vs[g][:, half:]
        cr, ci = at_re_ref[g, :, :half], at_im_ref[g, :, :half]
        ar, ai = cr, ci
        if has_h0:
            h0 = jnp.concatenate([jnp.broadcast_to(h0_ref[g, s:s + 1, :], (K, S5_NS)) for s in range(n_seq)], axis=0)
            hr, hi = h0[:, :half], h0[:, half:]
        else:
            hr = hi = 0.0
        sr = jnp.where(first, hr, shift(vr, 1))
        si = jnp.where(first, hi, shift(vi, 1))
        for t in range(n_steps):
            rr = jnp.where(valid[t], shift(sr, 1 << t), 0.0)
            ri = jnp.where(valid[t], shift(si, 1 << t), 0.0)
            sr, si = sr + (cr * rr - ci * ri), si + (cr * ri + ci * rr)
            cr, ci = cr * cr - ci * ci, 2.0 * (cr * ci)
        s_ins.append(jnp.concatenate([sr, si], axis=1).astype(BF))
        if want_final:
            fins.append(jnp.concatenate([ar * sr - ai * si + vr, ar * si + ai * sr + vi], axis=1))
    yield
    for g in range(groups):
        r0 = g * S5_GROUP
        y = _dot(mt_ref[g], xts[g]) + lax.dot_general(woutt_ref[g], s_ins[g], (((1,), (1,)), ((), ())),
                                                      preferred_element_type=F32)
        for i in range(T):
            yt_ref[i, r0:r0 + S5_GROUP, :] = y[i * S5_GROUP:(i + 1) * S5_GROUP, :]
        if want_final:
            per_seq = fins[g].reshape(n_seq, K, S5_NS)
            hf_ref[g] = jnp.where(fwd_row, per_seq[:, K - 1, :], per_seq[:, 0, :])


def _s5_core_stage(zt, ops, *, layer, seq_chunks, steps, h0=None, want_final=False):
    nc = zt.shape[2]
    gb = S5_GROUPS // steps
    rows = gb * S5_GROUP
    gspec = lambda r, c: pl.BlockSpec((gb, r, c), lambda i: (layer * steps + i, 0, 0))
    slab = pl.BlockSpec((S5_T, rows, nc), lambda i: (0, i, 0))
    in_specs = [slab, gspec(S5_CL, S5_CL), gspec(S5_CL, S5_NS), gspec(S5_CL, S5_NS),
                gspec(1, S5_NS), gspec(1, S5_NS)]
    args = [zt, ops["mt"], ops["win"], ops["woutt"], ops["at_re"], ops["at_im"]]
    if h0 is not None:
        in_specs.append(pl.BlockSpec((gb, h0.shape[1], S5_NS), lambda i: (i, 0, 0)))
        args.append(h0)
    out_specs = [slab]
    out_shape = [jax.ShapeDtypeStruct(zt.shape, F32)]
    if want_final:
        n_seq = nc // seq_chunks
        out_specs.append(pl.BlockSpec((gb, n_seq, S5_NS), lambda i: (i, 0, 0)))
        out_shape.append(jax.ShapeDtypeStruct((S5_GROUPS, n_seq, S5_NS), F32))
    kern = functools.partial(_s5_core_kernel, groups=gb, seq_chunks=seq_chunks, has_h0=h0 is not None,
                             want_final=want_final)
    return dict(kernel=kern, grid=(steps,), in_specs=in_specs, args=args, out_specs=out_specs, out_shape=out_shape)


def _s5_out_kernel(yt_ref, zt_ref, d_ref, wglut_ref, wa_ref, o_hbm, obuf, sem):
    i = pl.program_id(0)
    last = pl.num_programs(0) - 1
    slot = i % 2
    store = functools.partial(_token_slab_copies, o_hbm, obuf, sem, to_hbm=True)

    @pl.when(i >= 2)
    def _():
        for c in store(i - 2, slot):
            c.wait()

    d = d_ref[...]
    for j in range(0, S5_T, 2):
        pre = jnp.concatenate([yt_ref[j] + d * zt_ref[j], yt_ref[j + 1] + d * zt_ref[j + 1]], axis=1)
        ya = _gelu(pre)
        ya = ya * jax.nn.sigmoid(_dot(wglut_ref[...], ya.astype(BF)))
        pa = lax.dot_general(ya.astype(BF), wa_ref[...], (((0,), (0,)), ((), ())), preferred_element_type=F32)
        obuf[slot, j] = pa[:S5_CB, :]
        obuf[slot, j + 1] = pa[S5_CB:, :]
    for c in store(i, slot):
        c.start()

    @pl.when(i == last)
    def _():
        for c in store(i, slot):
            c.wait()

    @pl.when((i == last) & (i >= 1))
    def _():
        for c in store(i - 1, 1 - slot):
            c.wait()


def _s5_out(yt, zt, s5_d_col, w_glu_t, w_br_a, *, layer):
    nc = yt.shape[2]
    slab = pl.BlockSpec((S5_T, S5_WIDTH, S5_CB), lambda i: (0, 0, i))
    out = pl.pallas_call(
        _s5_out_kernel,
        grid=(nc // S5_CB,),
        in_specs=[slab, slab,
                  _const_spec((None, S5_WIDTH, 1), (layer, 0, 0)),
                  _const_spec((None, S5_WIDTH, S5_WIDTH), (layer, 0, 0)),
                  _const_spec((None, S5_WIDTH, D_MODEL), (layer, 0, 0))],
        out_specs=pl.BlockSpec(memory_space=pl.ANY),
        out_shape=jax.ShapeDtypeStruct((nc, S5_T, D_MODEL), F32),
        scratch_shapes=[pltpu.VMEM((2, S5_T, S5_CB, D_MODEL), F32), pltpu.SemaphoreType.DMA((2, S5_T))],
        compiler_params=_params(1),
        name="s5_out",
    )(yt, zt, s5_d_col, w_glu_t, w_br_a)
    return out.reshape(nc * S5_T, D_MODEL)


def _mix_out_kernel(x_ref, mod_ref, g_ref, pa_ref, yb_ref, yc_ref, wb_ref, wc_ref, wgate_ref, bgate_ref,
                    wout_ref, o_ref):
    x = x_ref[...]
    h = _mod_norm(x, g_ref[...], mod_ref[0, 4:5, :], mod_ref[0, 3:4, :]).astype(BF)

    def gate(k):
        lo = k * D_MODEL
        return jax.nn.sigmoid(_dot(h, wgate_ref[:, lo:lo + D_MODEL]) + bgate_ref[:, lo:lo + D_MODEL])

    merged = gate(0) * pa_ref[...]
    merged = merged + gate(1) * _dot(yb_ref[...], wb_ref[...])
    merged = merged + gate(2) * _dot(yc_ref[...], wc_ref[...])
    y = _dot(merged.astype(BF), wout_ref[...])
    o_ref[...] = x + mod_ref[0, 5:6, :] * y


def _mix_out(x, mod, norm_g, pa, yb, yc, w_b, w_c, w_gate, b_gate, w_out, *, layer, seq_tokens):
    n = x.shape[0]
    tm = FFN_TM
    tok = lambda c: pl.BlockSpec((tm, c), lambda i: (i, 0))
    return pl.pallas_call(
        _mix_out_kernel,
        grid=(n // tm,),
        in_specs=[
            tok(D_MODEL),
            _mod_spec(mod.shape[0], seq_tokens // tm),
            _const_spec((None, 1, D_MODEL), (3 * layer + 1, 0, 0)),
            tok(D_MODEL), tok(CONV_WIDTH), tok(SG_WIDTH),
            _const_spec((None, CONV_WIDTH, D_MODEL), (layer, 0, 0)),
            _const_spec((None, SG_WIDTH, D_MODEL), (layer, 0, 0)),
            _const_spec((None, D_MODEL, 3 * D_MODEL), (layer, 0, 0)),
            _const_spec((None, 1, 3 * D_MODEL), (layer, 0, 0)),
            _const_spec((None, D_MODEL, D_MODEL), (layer, 0, 0)),
        ],
        out_specs=tok(D_MODEL),
        out_shape=jax.ShapeDtypeStruct((n, D_MODEL), F32),
        compiler_params=_params(1),
        name="mix_out",
    )(x, mod, norm_g, pa, yb, yc, w_b, w_c, w_gate, b_gate, w_out)


def _grid_pos_embed(n_tokens, dim):
    rows = n_tokens // GRID_W
    rr, cc = jnp.meshgrid(jnp.arange(rows, dtype=F32), jnp.arange(GRID_W, dtype=F32), indexing='ij')
    quarter = dim // 4
    omega = 1.0 / (10000.0 ** (jnp.arange(quarter, dtype=F32) / quarter))

    def emb(p):
        ang = p.reshape(-1)[:, None] * omega[None, :]
        return jnp.concatenate([jnp.sin(ang), jnp.cos(ang)], axis=-1)

    return jnp.concatenate([emb(rr), emb(cc)], axis=-1)


def kernel(x_prompt, x_sample, state_ssm, c, c_ctx, w_mod, b_mod, norm_g, ffn_w1, ffn_w2, w_in, w_gate, b_gate,
           s5_a_re, s5_a_im, s5_log_dt, s5_b_re, s5_b_im, s5_c_re, s5_c_im, s5_d, s5_w_glu, w_br_a, conv_w,
           conv_b, conv_ln_g, conv_ln_b, w_br_b, sg_ln_g, sg_ln_b, sg_w, sg_b, w_br_c, w_out, final_g):
    batch, seq, _ = x_prompt.shape
    dec_batch, dec_seq, _ = x_sample.shape
    assert (batch * seq) % TOK_BLOCK == 0 and TOK_BLOCK % seq == 0 and dec_seq % TOK_BLOCK == 0
    assert seq % SG_CHUNK == 0 and seq % S5_T == 0 and dec_seq % S5_T == 0
    ctx_chunks, smp_chunks = seq // S5_T, dec_seq // S5_T
    assert S5_CB % ctx_chunks == 0 and S5_CB % smp_chunks == 0
    assert ctx_chunks & (ctx_chunks - 1) == 0 and smp_chunks & (smp_chunks - 1) == 0
    assert 1 + dec_batch <= MOD_ROWS

    cond = jnp.zeros((MOD_ROWS, D_MODEL), F32).at[0].set(c_ctx).at[1:1 + dec_batch].set(c)
    mod_stage = _modulation_stage(cond, w_mod, b_mod)
    [mod_all], prep_out, [w1_first], [w2_first] = _call(
        [mod_stage, _s5_prep_stage(s5_a_re, s5_a_im, s5_log_dt, s5_b_re, s5_b_im, s5_c_re, s5_c_im),
         _cast_stage(ffn_w1, (0, 0), mod_stage["grid"][0]), _cast_stage(ffn_w2, (0, 0), mod_stage["grid"][0])],
        "mod_prep")
    mod_all = mod_all.reshape(DEPTH, MOD_ROWS, N_MOD, D_MODEL)
    s5_ops = dict(zip(S5_OPS, prep_out))
    pos = _grid_pos_embed(dec_seq, D_MODEL)

    ffn_w = {(0, 0): (w1_first, w2_first)}
    mixer_w = dict(w_in=w_in, w_gate=w_gate, w_out=w_out, w_a=w_br_a, w_b=w_br_b, w_c=w_br_c)
    w_glu_t = s5_w_glu.transpose(0, 2, 1).astype(BF)
    norm_rows = norm_g.reshape(DEPTH * 3, 1, D_MODEL)
    rows = lambda a: a.reshape(DEPTH, 1, -1)
    sgw_cat = sg_w.transpose(0, 2, 1, 3).reshape(DEPTH, SG_CHUNK, SG_HEADS * SG_CHUNK).astype(BF)
    sgb_full = jnp.repeat(sg_b.transpose(0, 2, 1), SG_HEAD_DIM, axis=2)
    s5_d_col = s5_d.reshape(DEPTH, S5_WIDTH, 1)

    conv_w8 = jnp.broadcast_to(conv_w[:, :, None, :], (DEPTH, CONV_K, SUBLANES, CONV_WIDTH))

    ctx = dict(n_seq=batch, seq_len=seq, mod_rows=slice(0, 1), mod_tokens=batch * seq)
    smp = dict(n_seq=dec_batch, seq_len=dec_seq, mod_rows=slice(1, 1 + dec_batch), mod_tokens=dec_seq)
    x_ctx = x_prompt.reshape(batch * seq, D_MODEL)
    x_smp = x_sample.reshape(dec_batch * dec_seq, D_MODEL)
    ffn_steps = x_ctx.shape[0] // FFN_TM
    assert x_smp.shape[0] // FFN_TM == ffn_steps and S5_GROUPS % ffn_steps == 0

    assert DEPTH * S5_GROUPS // S5_PREP_GB == ffn_steps

    def ffn_stage(gr, x, l, which, **kw):
        return _ffn_stage(x, mod_all[l, gr["mod_rows"]], norm_rows, *ffn_w[l, which], layer=l, which=which,
                          seq_tokens=gr["mod_tokens"], **kw)

    def cast_stages(l, which):
        if l >= DEPTH:
            return []
        return [_cast_stage(ffn_w1, (l, which), ffn_steps), _cast_stage(ffn_w2, (l, which), ffn_steps)]

    def keep_casts(l, which, outs):
        if outs:
            ffn_w[l, which] = (outs[0][0], outs[1][0])

    def branches_in(gr, x, l):
        mod = mod_all[l, gr["mod_rows"]]
        yb, yc = _mix_in(x, mod, norm_rows, mixer_w["w_in"], conv_w8, rows(conv_b), rows(conv_ln_g), rows(conv_ln_b),
                         rows(sg_ln_g), rows(sg_ln_b), sgw_cat, sgb_full, layer=l, seq_len=gr["seq_len"])
        zt = _s5_in(x, mod, norm_rows, mixer_w["w_in"], layer=l, seq_chunks=gr["seq_len"] // S5_T)
        return yb, yc, zt

    def s5_stage(gr, zt, l, **kw):
        return _s5_core_stage(zt, s5_ops, layer=l, seq_chunks=gr["seq_len"] // S5_T, steps=ffn_steps, **kw)

    def branches_out(gr, x, l, yb, yc, zt, yt):
        pa = _s5_out(yt, zt, s5_d_col, w_glu_t, mixer_w["w_a"], layer=l)
        return _mix_out(x, mod_all[l, gr["mod_rows"]], norm_rows, pa, yb, yc, mixer_w["w_b"], mixer_w["w_c"],
                        mixer_w["w_gate"], rows(b_gate), mixer_w["w_out"], layer=l, seq_tokens=gr["mod_tokens"])

    ctx_states = []
    for l in range(DEPTH):
        riders = []
        if l == 0:
            riders = cast_stages(0, 1) + [_cast_stage(w.reshape(-1, w.shape[-1]), (), ffn_steps)
                                          for w in mixer_w.values()]
        [x_ctx], *outs = _call([ffn_stage(ctx, x_ctx, l, 0)] + riders, "ffn")
        if l == 0:
            keep_casts(0, 1, outs[:2])
            mixer_w = {k: o[0].reshape(mixer_w[k].shape) for k, o in zip(mixer_w, outs[2:])}
        yb_c, yc_c, zt_c = branches_in(ctx, x_ctx, l)
        [x_smp], [yt_c, hf], *outs = _call([ffn_stage(smp, x_smp, l, 0, pos=pos if l == 0 else None),
                                             s5_stage(ctx, zt_c, l, want_final=True)] + cast_stages(l + 1, 0),
                                            "ffn_s5")
        keep_casts(l + 1, 0, outs)
        hf = hf.reshape(S5_GROUPS, batch, 2, 2, S5_STATE)
        ctx_states.append(hf.transpose(1, 3, 0, 4, 2))
        x_ctx = branches_out(ctx, x_ctx, l, yb_c, yc_c, zt_c, yt_c)
        yb_s, yc_s, zt_s = branches_in(smp, x_smp, l)
        h0 = state_ssm[:, l].transpose(2, 0, 4, 1, 3).reshape(S5_GROUPS, dec_batch, S5_NS)
        final = dict(final_g=final_g) if l == DEPTH - 1 else {}
        [x_ctx], [yt_s], *outs = _call([ffn_stage(ctx, x_ctx, l, 1, **final), s5_stage(smp, zt_s, l, h0=h0)]
                                       + cast_stages(l + 1, 1), "ffn_s5")
        keep_casts(l + 1, 1, outs)
        x_smp = branches_out(smp, x_smp, l, yb_s, yc_s, zt_s, yt_s)
        [x_smp], = _call([ffn_stage(smp, x_smp, l, 1, **final)], "ffn")
    y_prompt = x_ctx.reshape(batch, seq, D_MODEL)
    y_sample = x_smp.reshape(dec_batch, dec_seq, D_MODEL)
    new_state_ssm = jnp.stack(ctx_states, axis=1)
    return (y_prompt, y_sample, new_state_ssm)
```
